```python
import math, functools
import jax, jax.numpy as jnp
from jax import lax
import numpy as np

D_MODEL = 1024
BATCH = 8
SEQ = 2048
DEPTH = 1
DEC_BATCH = 128
DEC_SEQ = 1
PAST_LEN = 16384
PAGE_SIZE = 128

A_HEADS = 8
QK_NOPE = 64
QK_ROPE = 32
V_DIM = 64
Q_LORA = 384
KV_LORA = 256
ROPE_BASE = 10000.0
Q_BLOCK = 128
ATTN_SCALE = (QK_NOPE + QK_ROPE) ** -0.5
S_HEADS = 16
S_HEAD_DIM = 64
D_INNER = S_HEADS * S_HEAD_DIM
S_GROUPS = 2
D_STATE = 128
CONV_W = 4
CONV_CH = D_INNER + 2 * S_GROUPS * D_STATE
CHUNK = 128
N_EXPERTS = 64
TOP_K = 6
N_EXP_GROUPS = 8
TOPK_GROUPS = 4
F_EXPERT = 256
F_SHARED = 256
ROUTE_SCALE = 2.5
EPS = 1e-6
IN_SPLITS = (Q_LORA, KV_LORA, QK_ROPE, D_INNER, CONV_CH, S_HEADS, D_MODEL, D_MODEL)
D_IN = Q_LORA + KV_LORA + QK_ROPE + D_INNER + CONV_CH + S_HEADS + 2 * D_MODEL

kernel_name = 'hybrid_mla_ssd_moe_step'


def rms_norm(x, g):
    xf = x.astype(jnp.float32)
    xf = xf * lax.rsqrt(jnp.mean(xf * xf, axis=-1, keepdims=True) + EPS)
    return (xf * g.astype(jnp.float32)).astype(x.dtype)


def rope_cos_sin(pos):
    half = QK_ROPE // 2
    inv = ROPE_BASE ** (-jnp.arange(half, dtype=jnp.float32) / half)
    ang = pos.astype(jnp.float32)[:, None] * inv[None, :]
    return jnp.cos(ang), jnp.sin(ang)


def apply_rope(x, cos, sin):
    x1, x2 = jnp.split(x.astype(jnp.float32), 2, axis=-1)
    return jnp.concatenate([x1 * cos - x2 * sin, x2 * cos + x1 * sin], axis=-1).astype(x.dtype)


def adaln(c, p):
    mod = jax.nn.silu(c) @ p['w_ada'] + p['b_ada']
    return jnp.split(mod[:, None, :], 6, axis=-1)


def mla_qkv(q_a, kv_a, kpe_raw, pos, p):
    b, s = q_a.shape[:2]
    q = (rms_norm(q_a, p['g_q_a']) @ p['w_q_b']).reshape(b, s, A_HEADS, QK_NOPE + QK_ROPE)
    cos, sin = rope_cos_sin(pos)
    q_nope = q[..., :QK_NOPE]
    q_pe = apply_rope(q[..., QK_NOPE:], cos[:, None, :], sin[:, None, :])
    k_pe = apply_rope(kpe_raw, cos, sin)
    ckv = rms_norm(kv_a, p['g_kv_a'])
    return q_nope, q_pe, ckv, k_pe


def mla_attend_prompt(q_nope, q_pe, ckv, k_pe, p):
    b, s = q_nope.shape[:2]
    k_nope = jnp.einsum('bkc,chd->bkhd', ckv, p['w_uk'])
    v = jnp.einsum('bkc,chd->bkhd', ckv, p['w_uv'])
    nb = s // Q_BLOCK
    qn = q_nope.reshape(b, nb, Q_BLOCK, A_HEADS, QK_NOPE).swapaxes(0, 1)
    qp = q_pe.reshape(b, nb, Q_BLOCK, A_HEADS, QK_ROPE).swapaxes(0, 1)
    key_pos = jnp.arange(s)

    def one_block(args):
        qn_b, qp_b, blk = args
        sc = (jnp.einsum('bqhd,bkhd->bhqk', qn_b, k_nope)
              + jnp.einsum('bqhr,bkr->bhqk', qp_b, k_pe)).astype(jnp.float32) * ATTN_SCALE
        q_pos = blk * Q_BLOCK + jnp.arange(Q_BLOCK)
        sc = jnp.where(key_pos[None, :] <= q_pos[:, None], sc, -jnp.inf)
        pr = jax.nn.softmax(sc, axis=-1).astype(v.dtype)
        return jnp.einsum('bhqk,bkhd->bqhd', pr, v)

    o = lax.map(one_block, (qn, qp, jnp.arange(nb)))
    return o.swapaxes(0, 1).reshape(b, s, A_HEADS * V_DIM)


def mla_attend_sample(q_nope, q_pe, ckv, k_pe, cache_ckv, cache_kpe, layer, page_table, p):
    b, s = q_nope.shape[:2]
    ckv_past = cache_ckv[layer, page_table].reshape(b, -1, KV_LORA)
    kpe_past = cache_kpe[layer, page_table].reshape(b, -1, QK_ROPE)
    n_past = ckv_past.shape[1]
    q_lat = jnp.einsum('bshd,chd->bshc', q_nope, p['w_uk'])
    sc_past = jnp.einsum('bshc,btc->bhst', q_lat, ckv_past) + jnp.einsum('bshr,btr->bhst', q_pe, kpe_past)
    sc_new = jnp.einsum('bshc,btc->bhst', q_lat, ckv) + jnp.einsum('bshr,btr->bhst', q_pe, k_pe)
    causal = jnp.tril(jnp.ones((s, s), dtype=bool))
    sc_new = jnp.where(causal, sc_new.astype(jnp.float32), -jnp.inf)
    sc = jnp.concatenate([sc_past.astype(jnp.float32), sc_new], axis=-1) * ATTN_SCALE
    pr = jax.nn.softmax(sc, axis=-1).astype(ckv.dtype)
    o_lat = (jnp.einsum('bhst,btc->bshc', pr[..., :n_past], ckv_past)
             + jnp.einsum('bhst,btc->bshc', pr[..., n_past:], ckv))
    return jnp.einsum('bshc,chd->bshd', o_lat, p['w_uv']).reshape(b, s, A_HEADS * V_DIM)


def causal_conv(xbc, conv_state, p):
    xpad = jnp.concatenate([conv_state.astype(xbc.dtype), xbc], axis=1)
    out = lax.conv_general_dilated(xpad, p['conv_w'][:, None, :], window_strides=(1,), padding='VALID',
                                   dimension_numbers=('NWC', 'WIO', 'NWC'), feature_group_count=CONV_CH)
    return jax.nn.silu(out + p['conv_b']), xpad[:, -(CONV_W - 1):]


def ssd_split(xbc, dt_raw, p):
    b, s = xbc.shape[:2]
    gn = S_GROUPS * D_STATE
    rep = S_HEADS // S_GROUPS
    xs = xbc[..., :D_INNER].reshape(b, s, S_HEADS, S_HEAD_DIM)
    bs = jnp.repeat(xbc[..., D_INNER:D_INNER + gn].reshape(b, s, S_GROUPS, D_STATE), rep, axis=2)
    cs = jnp.repeat(xbc[..., D_INNER + gn:].reshape(b, s, S_GROUPS, D_STATE), rep, axis=2)
    dt = jax.nn.softplus((dt_raw + p['dt_bias']).astype(jnp.float32))
    a = dt * (-jnp.exp(p['a_log'].astype(jnp.float32)))
    return xs, bs, cs, dt, a


def ssd_chunked(xs, bs, cs, dt, a, h0):
    b, s = xs.shape[:2]
    nc = s // CHUNK
    xdt = (xs.astype(jnp.float32) * dt[..., None]).reshape(b, nc, CHUNK, S_HEADS, S_HEAD_DIM)
    bc = bs.astype(jnp.float32).reshape(b, nc, CHUNK, S_HEADS, D_STATE)
    cc = cs.astype(jnp.float32).reshape(b, nc, CHUNK, S_HEADS, D_STATE)
    a_cum = jnp.cumsum(a.reshape(b, nc, CHUNK, S_HEADS), axis=2)
    seg = a_cum[:, :, :, None, :] - a_cum[:, :, None, :, :]
    causal = jnp.tril(jnp.ones((CHUNK, CHUNK), dtype=bool))[None, None, :, :, None]
    lmat = jnp.exp(jnp.where(causal, seg, -jnp.inf))
    y_diag = jnp.einsum('bclsh,bcshp->bclhp', jnp.einsum('bclhn,bcshn->bclsh', cc, bc) * lmat, xdt)
    decay_end = jnp.exp(a_cum[:, :, -1:, :] - a_cum)
    chunk_states = jnp.einsum('bclhn,bclh,bclhp->bchpn', bc, decay_end, xdt)
    chunk_decay = jnp.exp(a_cum[:, :, -1, :])

    def step(h, inp):
        st, dec = inp
        return h * dec[:, :, None, None] + st, h

    h_final, h_prev = lax.scan(step, h0, (chunk_states.swapaxes(0, 1), chunk_decay.swapaxes(0, 1)))
    h_prev = h_prev.swapaxes(0, 1)
    y_off = jnp.einsum('bclhn,bchpn,bclh->bclhp', cc, h_prev, jnp.exp(a_cum))
    return (y_diag + y_off).reshape(b, s, S_HEADS, S_HEAD_DIM), h_final


def ssd_recurrent(xs, bs, cs, dt, a, h0):
    def step(h, inp):
        x_t, b_t, c_t, dt_t, a_t = inp
        h = h * jnp.exp(a_t)[..., None, None] + jnp.einsum('bhp,bhn->bhpn', x_t * dt_t[..., None], b_t)
        return h, jnp.einsum('bhpn,bhn->bhp', h, c_t)

    seq = (xs.astype(jnp.float32).swapaxes(0, 1), bs.astype(jnp.float32).swapaxes(0, 1),
           cs.astype(jnp.float32).swapaxes(0, 1), dt.swapaxes(0, 1), a.swapaxes(0, 1))
    h_final, ys = lax.scan(step, h0, seq)
    return ys.swapaxes(0, 1), h_final


def ssd_output(y, xs, z, p):
    b, s = y.shape[:2]
    y = y + p['d_skip'].astype(jnp.float32)[:, None] * xs.astype(jnp.float32)
    y = y.reshape(b, s, D_INNER) * jax.nn.silu(z.astype(jnp.float32))
    y = rms_norm(y.reshape(b, s, S_GROUPS, D_INNER // S_GROUPS), p['g_ssd'].reshape(S_GROUPS, -1))
    return y.reshape(b, s, D_INNER).astype(z.dtype)


def mixer_core(h, pos, p, attend, conv_state, ssm_state, scan):
    u = h @ p['w_in']
    offs = tuple(int(o) for o in np.cumsum(IN_SPLITS)[:-1])
    q_a, kv_a, kpe_raw, z, xbc, dt_raw, gate_a, gate_b = jnp.split(u, offs, axis=-1)
    q_nope, q_pe, ckv, k_pe = mla_qkv(q_a, kv_a, kpe_raw, pos, p)
    o_attn = attend(q_nope, q_pe, ckv, k_pe) @ p['w_o_attn']
    xbc_c, new_conv = causal_conv(xbc, conv_state, p)
    xs, bs, cs, dt, a = ssd_split(xbc_c, dt_raw, p)
    y, new_ssm = scan(xs, bs, cs, dt, a, ssm_state.astype(jnp.float32))
    o_ssd = ssd_output(y, xs, z, p) @ p['w_o_ssd']
    merged = jax.nn.sigmoid(gate_a) * o_attn + jax.nn.sigmoid(gate_b) * o_ssd
    return merged @ p['w_out'], (ckv, k_pe, new_conv, new_ssm.astype(h.dtype))


def moe_ffn(h, p):
    b, s, d = h.shape
    t = h.reshape(b * s, d)
    scores = jax.nn.sigmoid((t @ p['w_router']).astype(jnp.float32))
    biased = scores + p['b_router'].astype(jnp.float32)
    grp = biased.reshape(-1, N_EXP_GROUPS, N_EXPERTS // N_EXP_GROUPS)
    grp_score = lax.top_k(grp, 2)[0].sum(-1)
    _, g_idx = lax.top_k(grp_score, TOPK_GROUPS)
    g_mask = jax.nn.one_hot(g_idx, N_EXP_GROUPS, dtype=jnp.float32).sum(-2) > 0
    e_mask = jnp.repeat(g_mask, N_EXPERTS // N_EXP_GROUPS, axis=-1)
    _, e_idx = lax.top_k(jnp.where(e_mask, biased, -jnp.inf), TOP_K)
    w = jnp.take_along_axis(scores, e_idx, axis=-1)
    w = w / jnp.sum(w, axis=-1, keepdims=True) * ROUTE_SCALE
    combine = jnp.einsum('tk,tke->te', w, jax.nn.one_hot(e_idx, N_EXPERTS, dtype=jnp.float32)).astype(h.dtype)
    gte = jnp.einsum('td,edf->tef', t, p['w_gate_e'])
    upe = jnp.einsum('td,edf->tef', t, p['w_up_e'])
    routed = jnp.einsum('tef,efd->td', jax.nn.silu(gte) * upe * combine[:, :, None], p['w_down_e'])
    shared = (jax.nn.silu(t @ p['w_gate_s']) * (t @ p['w_up_s'])) @ p['w_down_s']
    return (routed + shared).reshape(b, s, d)


def block(x, c, p, mix):
    sh1, sc1, g1, sh2, sc2, g2 = adaln(c, p)
    h = rms_norm(x, p['g_pre_mix']) * (1 + sc1) + sh1
    m, st = mix(h)
    x = x + g1 * rms_norm(m, p['g_post_mix'])
    h = rms_norm(x, p['g_pre_ffn']) * (1 + sc2) + sh2
    x = x + g2 * rms_norm(moe_ffn(h, p), p['g_post_ffn'])
    return x, st


def setup_inputs(seed: int = 0) -> dict:
    key = jax.random.key(seed)
    ks = iter(jax.random.split(key, 64))
    f32 = jnp.float32

    def nrm(shape, scale):
        return jax.random.normal(next(ks), shape, f32) * scale

    def gain(n):
        return 1.0 + 0.02 * jax.random.normal(next(ks), (DEPTH, n), f32)

    n_pages = PAST_LEN // PAGE_SIZE
    n_used = DEC_BATCH * n_pages
    n_pool = n_used + max(1, n_used // 4)
    perm = jax.random.permutation(next(ks), n_pool)
    page_table = perm[:n_used].reshape(DEC_BATCH, n_pages).astype(jnp.int32)
    dt0 = jnp.exp(jax.random.uniform(next(ks), (DEPTH, S_HEADS), f32, math.log(1e-3), math.log(1e-1)))
    dt_bias = dt0 + jnp.log(-jnp.expm1(-dt0))
    a_log = jnp.log(jax.random.uniform(next(ks), (DEPTH, S_HEADS), f32, 1.0, 16.0))
    inp = {
        'x_prompt': nrm((BATCH, SEQ, D_MODEL), 1.0),
        'x_sample': nrm((DEC_BATCH, DEC_SEQ, D_MODEL), 1.0),
        'cache_ckv': nrm((DEPTH, n_pool, PAGE_SIZE, KV_LORA), 1.0),
        'cache_kpe': nrm((DEPTH, n_pool, PAGE_SIZE, QK_ROPE), 1.0),
        'state_conv': nrm((DEPTH, DEC_BATCH, CONV_W - 1, CONV_CH), 1.0),
        'state_ssm': nrm((DEPTH, DEC_BATCH, S_HEADS, S_HEAD_DIM, D_STATE), 0.5),
        'page_table': page_table,
        'c_prompt': nrm((BATCH, D_MODEL), 1.0),
        'c_sample': nrm((DEC_BATCH, D_MODEL), 1.0),
        'w_ada': nrm((DEPTH, D_MODEL, 6 * D_MODEL), 0.3 * D_MODEL ** -0.5),
        'b_ada': nrm((DEPTH, 6 * D_MODEL), 0.02),
        'g_pre_mix': gain(D_MODEL),
        'g_post_mix': gain(D_MODEL),
        'g_pre_ffn': gain(D_MODEL),
        'g_post_ffn': gain(D_MODEL),
        'w_in': nrm((DEPTH, D_MODEL, D_IN), D_MODEL ** -0.5),
        'g_q_a': gain(Q_LORA),
        'w_q_b': nrm((DEPTH, Q_LORA, A_HEADS * (QK_NOPE + QK_ROPE)), Q_LORA ** -0.5),
        'g_kv_a': gain(KV_LORA),
        'w_uk': nrm((DEPTH, KV_LORA, A_HEADS, QK_NOPE), KV_LORA ** -0.5),
        'w_uv': nrm((DEPTH, KV_LORA, A_HEADS, V_DIM), KV_LORA ** -0.5),
        'w_o_attn': nrm((DEPTH, A_HEADS * V_DIM, D_MODEL), (A_HEADS * V_DIM) ** -0.5),
        'conv_w': nrm((DEPTH, CONV_W, CONV_CH), CONV_W ** -0.5),
        'conv_b': nrm((DEPTH, CONV_CH), 0.02),
        'dt_bias': dt_bias,
        'a_log': a_log,
        'd_skip': 1.0 + nrm((DEPTH, S_HEADS), 0.1),
        'g_ssd': gain(D_INNER),
        'w_o_ssd': nrm((DEPTH, D_INNER, D_MODEL), D_INNER ** -0.5),
        'w_out': nrm((DEPTH, D_MODEL, D_MODEL), D_MODEL ** -0.5),
        'w_router': nrm((DEPTH, D_MODEL, N_EXPERTS), D_MODEL ** -0.5),
        'b_router': nrm((DEPTH, N_EXPERTS), 0.01),
        'w_gate_e': nrm((DEPTH, N_EXPERTS, D_MODEL, F_EXPERT), D_MODEL ** -0.5),
        'w_up_e': nrm((DEPTH, N_EXPERTS, D_MODEL, F_EXPERT), D_MODEL ** -0.5),
        'w_down_e': nrm((DEPTH, N_EXPERTS, F_EXPERT, D_MODEL), F_EXPERT ** -0.5),
        'w_gate_s': nrm((DEPTH, D_MODEL, F_SHARED), D_MODEL ** -0.5),
        'w_up_s': nrm((DEPTH, D_MODEL, F_SHARED), D_MODEL ** -0.5),
        'w_down_s': nrm((DEPTH, F_SHARED, D_MODEL), F_SHARED ** -0.5),
    }
    return inp


def reference(x_prompt, x_sample, cache_ckv, cache_kpe, state_conv, state_ssm, page_table, c_prompt, c_sample,
              w_ada, b_ada, g_pre_mix, g_post_mix, g_pre_ffn, g_post_ffn, w_in, g_q_a, w_q_b, g_kv_a, w_uk, w_uv,
              w_o_attn, conv_w, conv_b, dt_bias, a_log, d_skip, g_ssd, w_o_ssd, w_out, w_router, b_router,
              w_gate_e, w_up_e, w_down_e, w_gate_s, w_up_s, w_down_s):
    bp, sp = x_prompt.shape[:2]
    ds = x_sample.shape[1]
    pos_p = jnp.arange(sp)
    pos_s = PAST_LEN + jnp.arange(ds)
    y_prompt, y_sample = x_prompt, x_sample
    new_p, new_s = [], []
    for l in range(DEPTH):
        p = {'w_ada': w_ada[l], 'b_ada': b_ada[l], 'g_pre_mix': g_pre_mix[l], 'g_post_mix': g_post_mix[l],
             'g_pre_ffn': g_pre_ffn[l], 'g_post_ffn': g_post_ffn[l], 'w_in': w_in[l], 'g_q_a': g_q_a[l],
             'w_q_b': w_q_b[l], 'g_kv_a': g_kv_a[l], 'w_uk': w_uk[l], 'w_uv': w_uv[l], 'w_o_attn': w_o_attn[l],
             'conv_w': conv_w[l], 'conv_b': conv_b[l], 'dt_bias': dt_bias[l], 'a_log': a_log[l],
             'd_skip': d_skip[l], 'g_ssd': g_ssd[l], 'w_o_ssd': w_o_ssd[l], 'w_out': w_out[l],
             'w_router': w_router[l], 'b_router': b_router[l], 'w_gate_e': w_gate_e[l], 'w_up_e': w_up_e[l],
             'w_down_e': w_down_e[l], 'w_gate_s': w_gate_s[l], 'w_up_s': w_up_s[l], 'w_down_s': w_down_s[l]}
        mix_p = functools.partial(
            mixer_core, pos=pos_p, p=p, attend=functools.partial(mla_attend_prompt, p=p),
            conv_state=jnp.zeros((bp, CONV_W - 1, CONV_CH), x_prompt.dtype),
            ssm_state=jnp.zeros((bp, S_HEADS, S_HEAD_DIM, D_STATE), jnp.float32), scan=ssd_chunked)
        y_prompt, st_p = block(y_prompt, c_prompt, p, mix_p)
        mix_s = functools.partial(
            mixer_core, pos=pos_s, p=p,
            attend=functools.partial(mla_attend_sample, cache_ckv=cache_ckv, cache_kpe=cache_kpe, layer=l,
                                     page_table=page_table, p=p),
            conv_state=state_conv[l], ssm_state=state_ssm[l], scan=ssd_recurrent)
        y_sample, st_s = block(y_sample, c_sample, p, mix_s)
        new_p.append(st_p)
        new_s.append(st_s)
    ckv_p = jnp.stack([t[0] for t in new_p])
    kpe_p = jnp.stack([t[1] for t in new_p])
    conv_p = jnp.stack([t[2] for t in new_p])
    ssm_p = jnp.stack([t[3] for t in new_p])
    ckv_s = jnp.stack([t[0] for t in new_s])
    kpe_s = jnp.stack([t[1] for t in new_s])
    conv_s = jnp.stack([t[2] for t in new_s])
    ssm_s = jnp.stack([t[3] for t in new_s])
    return (y_prompt, y_sample, ckv_p, kpe_p, conv_p, ssm_p, ckv_s, kpe_s, conv_s, ssm_s)
```

```python
import functools

import jax
import jax.numpy as jnp
from jax import lax
from jax.experimental import pallas as pl
from jax.experimental.pallas import tpu as pltpu

F32 = jnp.float32
BF16 = jnp.bfloat16

D_MODEL = 1024
PAGE_SIZE = 128
A_HEADS = 8
QK_NOPE = 64
QK_ROPE = 32
V_DIM = 64
Q_LORA = 384
KV_LORA = 256
ROPE_BASE = 10000.0
ATTN_SCALE = (QK_NOPE + QK_ROPE) ** -0.5
S_HEADS = 16
S_HEAD_DIM = 64
D_INNER = S_HEADS * S_HEAD_DIM
S_GROUPS = 2
D_STATE = 128
CONV_W = 4
CONV_CH = D_INNER + 2 * S_GROUPS * D_STATE
CHUNK = 128
N_EXPERTS = 64
TOP_K = 6
N_EXP_GROUPS = 8
TOPK_GROUPS = 4
F_EXPERT = 256
ROUTE_SCALE = 2.5
EPS = 1e-6
IN_SPLITS = (Q_LORA, KV_LORA, QK_ROPE, D_INNER, CONV_CH, S_HEADS, D_MODEL, D_MODEL)

LANES = 128
HEAD_PAD = LANES
GROUP_HEADS = S_HEADS // S_GROUPS
GROUP_CH = D_INNER // S_GROUPS
EXP_PER_GROUP = N_EXPERTS // N_EXP_GROUPS
VMEM_LIMIT_BYTES = 56 * 1024 * 1024

NEG_INF = float("-inf")


def _params(*semantics):
    return pltpu.CompilerParams(dimension_semantics=semantics, vmem_limit_bytes=VMEM_LIMIT_BYTES)


def _const_spec(arr):
    nd = arr.ndim
    return pl.BlockSpec(arr.shape, lambda *_: (0,) * nd)


def _dot(a, b):
    return jnp.dot(a, b, preferred_element_type=F32)


def _dot_nt(a, b):
    return lax.dot_general(a, b, (((1,), (1,)), ((), ())), preferred_element_type=F32)


def _dot_tn(a, b):
    return lax.dot_general(a, b, (((0,), (0,)), ((), ())), preferred_element_type=F32)


def _rms(x, g):
    return x * lax.rsqrt(jnp.mean(x * x, axis=-1, keepdims=True) + EPS) * g


def _silu(x):
    return x * (1.0 / (1.0 + jnp.exp(-x)))


def _sigmoid(x):
    return 1.0 / (1.0 + jnp.exp(-x))


def _softplus(x):
    return jnp.maximum(x, 0.0) + jnp.log(1.0 + jnp.exp(-jnp.abs(x)))


def _split3(x):
    hi = x.astype(BF16)
    r1 = x - hi.astype(F32)
    mid = r1.astype(BF16)
    lo = (r1 - mid.astype(F32)).astype(BF16)
    return hi, mid, lo


def _adaln_kernel(c_ref, w_ref, b_ref, o_ref):
    c = _silu(c_ref[...]).astype(BF16)
    o_ref[...] = _dot(c, w_ref[...].astype(BF16)) + b_ref[...]


def _adaln(c_all, w_ada, b_ada):
    rows, d = c_all.shape
    n = w_ada.shape[1]
    tn = 512
    return pl.pallas_call(
        _adaln_kernel,
        grid=(n // tn,),
        in_specs=[pl.BlockSpec((rows, d), lambda j: (0, 0)),
                  pl.BlockSpec((d, tn), lambda j: (0, j)),
                  pl.BlockSpec((1, tn), lambda j: (0, j))],
        out_specs=pl.BlockSpec((rows, tn), lambda j: (0, j)),
        out_shape=jax.ShapeDtypeStruct((rows, n), F32),
        compiler_params=_params("arbitrary"),
        name="adaln",
    )(c_all, w_ada, b_ada)


def _premix_kernel(x_ref, sh_ref, sc_ref, g_ref, ct_ref, st_ref,
                   wqa, wkva, wkpe, wkper, wz, wxbc, wdt, wdtt, wga, wgb,
                   gqa, wq, wqr, gkv, wuk, wuv,
                   q_o, k_o, v_o, ckv_o, kpe_o, z_o, xbc_o, dt_o, dtt_o, ga_o, gb_o):
    x = x_ref[0]
    h = _rms(x, g_ref[...]) * (1.0 + sc_ref[0]) + sh_ref[0]
    hb = h.astype(BF16)
    ct = ct_ref[...]
    st = st_ref[...]

    qn = _rms(_dot(hb, wqa[...]), gqa[...]).astype(BF16)
    q = _dot(qn, wq[...])
    qr = _dot(qn, wqr[...])
    for hd in range(A_HEADS):
        sl = slice(hd * HEAD_PAD, (hd + 1) * HEAD_PAD)
        q_o[0, :, sl] = (q[:, sl] * ct + qr[:, sl] * st).astype(BF16)

    ckv = _rms(_dot(hb, wkva[...]), gkv[...])
    ckv_o[0] = ckv
    cb = ckv.astype(BF16)
    kpe = _dot(hb, wkpe[...]) * ct + _dot(hb, wkper[...]) * st
    kpe_o[0] = kpe
    kn = _dot(cb, wuk[...])
    for hd in range(A_HEADS):
        sl = slice(hd * HEAD_PAD, (hd + 1) * HEAD_PAD)
        k_o[0, :, sl] = (kn[:, sl] + kpe).astype(BF16)
    v_o[0] = _dot(cb, wuv[...]).astype(BF16)

    z_o[0] = _dot(hb, wz[...])
    xbc_o[0] = _dot(hb, wxbc[...])
    dt_o[0] = _dot(hb, wdt[...])
    dtt_o[0] = _dot_nt(wdtt[...], hb)
    ga_o[0] = _dot(hb, wga[...])
    gb_o[0] = _dot(hb, wgb[...])


def _premix(x, sh, sc, g, ctab, stab, weights, tm):
    b, s, d = x.shape
    per_row_mod = sh.shape[1] != 1
    per_row_tab = ctab.shape[0] != 1
    mod_spec = (pl.BlockSpec((1, tm, d), lambda i, j: (i, j, 0)) if per_row_mod
                else pl.BlockSpec((1, 1, d), lambda i, j: (i, 0, 0)))
    tab_spec = (pl.BlockSpec((tm, LANES), lambda i, j: (j, 0)) if per_row_tab
                else pl.BlockSpec((1, LANES), lambda i, j: (0, 0)))

    def tok(n):
        return pl.BlockSpec((1, tm, n), lambda i, j: (i, j, 0))

    out_cols = [(A_HEADS * HEAD_PAD, BF16), (A_HEADS * HEAD_PAD, BF16), (A_HEADS * V_DIM, BF16),
                (KV_LORA, F32), (LANES, F32), (D_INNER, F32), (CONV_CH, F32), (LANES, F32)]
    out_shape = [jax.ShapeDtypeStruct((b, s, n), dt) for n, dt in out_cols]
    out_specs = [tok(n) for n, _ in out_cols]
    out_shape.append(jax.ShapeDtypeStruct((b, S_HEADS, s), F32))
    out_specs.append(pl.BlockSpec((1, S_HEADS, tm), lambda i, j: (i, 0, j)))
    out_shape += [jax.ShapeDtypeStruct((b, s, D_MODEL), F32)] * 2
    out_specs += [tok(D_MODEL)] * 2
    order = [out_shape[i] for i in (0, 1, 2, 3, 4, 5, 6, 7, 8, 9, 10)]
    return pl.pallas_call(
        _premix_kernel,
        grid=(b, s // tm),
        in_specs=[tok(d), mod_spec, mod_spec, _const_spec(g), tab_spec, tab_spec]
        + [_const_spec(w) for w in weights],
        out_specs=out_specs,
        out_shape=order,
        compiler_params=_params("arbitrary", "arbitrary"),
        name="premix",
    )(x, sh, sc, g, ctab, stab, *weights)


def _flash_kernel(q_ref, k_ref, v_ref, o_ref, m_ref, l_ref, acc_ref, *, tq, tk):
    qi = pl.program_id(2)
    ki = pl.program_id(3)
    nk = pl.num_programs(3)

    @pl.when(ki == 0)
    def _():
        m_ref[...] = jnp.full(m_ref.shape, NEG_INF, F32)
        l_ref[...] = jnp.zeros(l_ref.shape, F32)
        acc_ref[...] = jnp.zeros(acc_ref.shape, F32)

    @pl.when(ki * tk <= qi * tq + (tq - 1))
    def _():
        row = qi * tq + lax.broadcasted_iota(jnp.int32, (tq, tk), 0)
        col = ki * tk + lax.broadcasted_iota(jnp.int32, (tq, tk), 1)
        causal = col <= row
        vp = v_ref[0]
        for hh in range(2):
            sl = slice(hh * HEAD_PAD, (hh + 1) * HEAD_PAD)
            s = _dot_nt(q_ref[0, :, sl], k_ref[0, :, sl]) * ATTN_SCALE
            s = jnp.where(causal, s, NEG_INF)
            m_prev = m_ref[hh]
            m_new = jnp.maximum(m_prev, jnp.max(s, axis=-1, keepdims=True))
            alpha = jnp.exp(m_prev - m_new)
            p = jnp.exp(s - m_new)
            l_ref[hh] = alpha * l_ref[hh] + jnp.sum(p, axis=-1, keepdims=True)
            acc_ref[hh] = alpha * acc_ref[hh] + _dot(p.astype(BF16), vp)
            m_ref[hh] = m_new

    @pl.when(ki == nk - 1)
    def _():
        lane = lax.broadcasted_iota(jnp.int32, (tq, 2 * V_DIM), 1)
        o0 = acc_ref[0] / l_ref[0]
        o1 = acc_ref[1] / l_ref[1]
        o_ref[0] = jnp.where(lane < V_DIM, o0, o1).astype(BF16)


def _flash_attention(q, k, v, tq, tk):
    b, s, _ = q.shape
    nq, nk = s // tq, s // tk

    def kv_idx(bi, hp, qi, ki):
        last = (qi * tq + tq - 1) // tk
        return (bi, jnp.minimum(ki, last), hp)

    return pl.pallas_call(
        functools.partial(_flash_kernel, tq=tq, tk=tk),
        grid=(b, A_HEADS // 2, nq, nk),
        in_specs=[pl.BlockSpec((1, tq, 2 * HEAD_PAD), lambda bi, hp, qi, ki: (bi, qi, hp)),
                  pl.BlockSpec((1, tk, 2 * HEAD_PAD), kv_idx),
                  pl.BlockSpec((1, tk, 2 * V_DIM), kv_idx)],
        out_specs=pl.BlockSpec((1, tq, 2 * V_DIM), lambda bi, hp, qi, ki: (bi, qi, hp)),
        out_shape=jax.ShapeDtypeStruct((b, s, A_HEADS * V_DIM), BF16),
        scratch_shapes=[pltpu.VMEM((2, tq, 1), F32), pltpu.VMEM((2, tq, 1), F32),
                        pltpu.VMEM((2, tq, 2 * V_DIM), F32)],
        compiler_params=_params("arbitrary", "arbitrary", "arbitrary", "arbitrary"),
        name="flash_attention",
    )(q, k, v)


def _ssd_chunk_kernel(xbc_ref, z_ref, dt_ref, dtt_ref, cw_ref, cb_ref, dtb_ref, dtbc_ref,
                      na_ref, nac_ref, dsk_ref, gs_ref, y_o, st_o, xc_ref, state_ref, y_ref):
    c = pl.program_id(1)
    nc = pl.num_programs(1)

    @pl.when(c == 0)
    def _():
        xc_ref[0:8, :] = jnp.zeros((8, CONV_CH), F32)
        state_ref[...] = jnp.zeros(state_ref.shape, F32)

    xc_ref[8:8 + CHUNK, :] = xbc_ref[0]
    conv = cb_ref[...] + cw_ref[CONV_W - 1:CONV_W, :] * xc_ref[8:8 + CHUNK, :]
    for j in range(1, CONV_W):
        conv = conv + cw_ref[CONV_W - 1 - j:CONV_W - j, :] * xc_ref[8 - j:8 - j + CHUNK, :]
    xc_ref[0:8, :] = xc_ref[CHUNK:CHUNK + 8, :]
    xbc = _silu(conv)

    dt_c = _softplus(dt_ref[0] + dtb_ref[...])
    a_c = dt_c * na_ref[...]
    dt_r = _softplus(dtt_ref[0] + dtbc_ref[...])
    a_r = dt_r * nac_ref[...]
    li = lax.broadcasted_iota(jnp.int32, (CHUNK, CHUNK), 0)
    si = lax.broadcasted_iota(jnp.int32, (CHUNK, CHUNK), 1)
    lower = (si <= li)
    tri = lower.astype(BF16)
    acum_c = sum(_dot(tri, part) for part in _split3(a_c))
    acum_r = sum(_dot_nt(part, tri) for part in _split3(a_r))
    a_last = acum_c[CHUNK - 1:CHUNK, :]
    dec_end = jnp.exp(a_last - acum_c)
    exp_ac = jnp.exp(acum_c)
    dec_all = jnp.exp(a_last)

    for g in range(S_GROUPS):
        bm = xbc[:, D_INNER + g * D_STATE:D_INNER + (g + 1) * D_STATE]
        cm = xbc[:, D_INNER + (S_GROUPS + g) * D_STATE:D_INNER + (S_GROUPS + g + 1) * D_STATE]
        bmb = bm.astype(BF16)
        cmb = cm.astype(BF16)
        gmat = _dot_nt(cmb, bmb)
        gsl = slice(g * GROUP_CH, (g + 1) * GROUP_CH)
        y_off = _dot(cmb, state_ref[:, gsl].astype(BF16))
        xde_parts = []
        for hh in range(GROUP_HEADS):
            hd = g * GROUP_HEADS + hh
            hsl = slice(hd * S_HEAD_DIM, (hd + 1) * S_HEAD_DIM)
            xs = xbc[:, hsl]
            xdt = xs * dt_c[:, hd:hd + 1]
            seg = acum_c[:, hd:hd + 1] - acum_r[hd:hd + 1, :]
            lmat = jnp.exp(jnp.where(lower, seg, NEG_INF))
            y_diag = _dot((gmat * lmat).astype(BF16), xdt.astype(BF16))
            yo = y_off[:, hh * S_HEAD_DIM:(hh + 1) * S_HEAD_DIM] * exp_ac[:, hd:hd + 1]
            y_ref[:, hsl] = y_diag + yo
            xde_parts.append((xdt * dec_end[:, hd:hd + 1]).astype(BF16))
        xde = jnp.concatenate(xde_parts, axis=1)
        new_states = _dot_tn(bmb, xde)
        for hh in range(GROUP_HEADS):
            hd = g * GROUP_HEADS + hh
            hsl = slice(hd * S_HEAD_DIM, (hd + 1) * S_HEAD_DIM)
            lsl = slice(hh * S_HEAD_DIM, (hh + 1) * S_HEAD_DIM)
            state_ref[:, hsl] = state_ref[:, hsl] * dec_all[:, hd:hd + 1] + new_states[:, lsl]

    y = y_ref[...] + dsk_ref[...] * xbc[:, :D_INNER]
    y = y * _silu(z_ref[0])
    for g in range(S_GROUPS):
        gsl = slice(g * GROUP_CH, (g + 1) * GROUP_CH)
        y_o[0, :, gsl] = _rms(y[:, gsl], gs_ref[:, gsl]).astype(BF16)

    @pl.when(c == nc - 1)
    def _():
        st_o[0] = state_ref[...]


def _ssd_prompt(xbc, z, dt, dtt, conv_w, conv_b, dtb_row, dtb_col, na_row, na_col, dskip, g_ssd):
    b, s, _ = xbc.shape
    nc = s // CHUNK
    consts = (conv_w, conv_b, dtb_row, dtb_col, na_row, na_col, dskip, g_ssd)
    return pl.pallas_call(
        _ssd_chunk_kernel,
        grid=(b, nc),
        in_specs=[pl.BlockSpec((1, CHUNK, CONV_CH), lambda i, j: (i, j, 0)),
                  pl.BlockSpec((1, CHUNK, D_INNER), lambda i, j: (i, j, 0)),
                  pl.BlockSpec((1, CHUNK, LANES), lambda i, j: (i, j, 0)),
                  pl.BlockSpec((1, S_HEADS, CHUNK), lambda i, j: (i, 0, j))]
        + [_const_spec(a) for a in consts],
        out_specs=[pl.BlockSpec((1, CHUNK, D_INNER), lambda i, j: (i, j, 0)),
                   pl.BlockSpec((1, D_STATE, D_INNER), lambda i, j: (i, 0, 0))],
        out_shape=[jax.ShapeDtypeStruct((b, s, D_INNER), BF16),
                   jax.ShapeDtypeStruct((b, D_STATE, D_INNER), F32)],
        scratch_shapes=[pltpu.VMEM((CHUNK + 8, CONV_CH), F32),
                        pltpu.VMEM((D_STATE, D_INNER), F32),
                        pltpu.VMEM((CHUNK, D_INNER), F32)],
        compiler_params=_params("arbitrary", "arbitrary"),
        name="ssd_prompt",
    )(xbc, z, dt, dtt, *consts)


def _merge_kernel(x_ref, oa_ref, ys_ref, ga_ref, gb_ref, g1_ref, sh2_ref, sc2_ref,
                  woa, wos, wout, gpost, gpre, x1_o, h2_o):
    o_attn = _dot(oa_ref[0], woa[...])
    o_ssd = _dot(ys_ref[0], wos[...])
    merged = _sigmoid(ga_ref[0]) * o_attn + _sigmoid(gb_ref[0]) * o_ssd
    m = _dot(merged.astype(BF16), wout[...])
    x1 = x_ref[0] + g1_ref[0] * _rms(m, gpost[...])
    x1_o[0] = x1
    h2_o[0] = (_rms(x1, gpre[...]) * (1.0 + sc2_ref[0]) + sh2_ref[0]).astype(BF16)


def _merge(x, o_attn, y_ssd, ga, gb, g1, sh2, sc2, weights, tm):
    b, s, d = x.shape
    per_row_mod = g1.shape[1] != 1
    mod_spec = (pl.BlockSpec((1, tm, d), lambda i, j: (i, j, 0)) if per_row_mod
                else pl.BlockSpec((1, 1, d), lambda i, j: (i, 0, 0)))

    def tok(n):
        return pl.BlockSpec((1, tm, n), lambda i, j: (i, j, 0))

    return pl.pallas_call(
        _merge_kernel,
        grid=(b, s // tm),
        in_specs=[tok(d), tok(A_HEADS * V_DIM), tok(D_INNER), tok(d), tok(d),
                  mod_spec, mod_spec, mod_spec] + [_const_spec(w) for w in weights],
        out_specs=[tok(d), tok(d)],
        out_shape=[jax.ShapeDtypeStruct((b, s, d), F32), jax.ShapeDtypeStruct((b, s, d), BF16)],
        compiler_params=_params("arbitrary", "arbitrary"),
        name="merge",
    )(x, o_attn, y_ssd, ga, gb, g1, sh2, sc2, *weights)


def _first_max(vals, idx, big):
    m = vals[0]
    for v in vals[1:]:
        m = jnp.maximum(m, v)
    m = jnp.max(m, axis=0, keepdims=True)
    cand = [jnp.where(v == m, i, big) for v, i in zip(vals, idx)]
    a = cand[0]
    for cnd in cand[1:]:
        a = jnp.minimum(a, cnd)
    a = jnp.min(a, axis=0, keepdims=True)
    return m, a


def _router_kernel(h_ref, wt_ref, b_ref, o_ref, *, tm):
    logits = _dot_nt(wt_ref[...], h_ref[0])
    scores = _sigmoid(logits)
    biased = scores + b_ref[...]
    sub = lax.broadcasted_iota(jnp.int32, (EXP_PER_GROUP, tm), 0)
    slabs = [biased[g * EXP_PER_GROUP:(g + 1) * EXP_PER_GROUP, :] for g in range(N_EXP_GROUPS)]
    big = jnp.int32(N_EXPERTS)

    gscore = []
    for g in range(N_EXP_GROUPS):
        m1, a1 = _first_max([slabs[g]], [sub], big)
        rest = jnp.where(sub == a1, NEG_INF, slabs[g])
        m2 = jnp.max(rest, axis=0, keepdims=True)
        gscore.append(m1 + m2)
    gs = jnp.full((N_EXP_GROUPS, tm), NEG_INF, F32)
    for g in range(N_EXP_GROUPS):
        gs = jnp.where(sub == g, gscore[g], gs)
    gsel = jnp.zeros((N_EXP_GROUPS, tm), F32)
    for _ in range(TOPK_GROUPS):
        _, a = _first_max([gs], [sub], big)
        hit = sub == a
        gsel = jnp.where(hit, 1.0, gsel)
        gs = jnp.where(hit, NEG_INF, gs)

    masked = [jnp.where(gsel[g:g + 1, :] > 0.5, slabs[g], NEG_INF) for g in range(N_EXP_GROUPS)]
    flat = [sub + g * EXP_PER_GROUP for g in range(N_EXP_GROUPS)]
    chosen = [jnp.zeros((EXP_PER_GROUP, tm), jnp.bool_) for _ in range(N_EXP_GROUPS)]
    for _ in range(TOP_K):
        _, a = _first_max(masked, flat, big)
        for g in range(N_EXP_GROUPS):
            hit = flat[g] == a
            chosen[g] = jnp.logical_or(chosen[g], hit)
            masked[g] = jnp.where(hit, NEG_INF, masked[g])
    w = [jnp.where(chosen[g], scores[g * EXP_PER_GROUP:(g + 1) * EXP_PER_GROUP, :], 0.0)
         for g in range(N_EXP_GROUPS)]
    tot = w[0]
    for part in w[1:]:
        tot = tot + part
    tot = jnp.sum(tot, axis=0, keepdims=True)
    comb_t = jnp.concatenate([part / tot * ROUTE_SCALE for part in w]
                             + [jnp.zeros((LANES - N_EXPERTS, tm), F32)], axis=0)
    o_ref[0] = comb_t.T


def _router(h2, w_router_t, b_col, tm):
    b, s, d = h2.shape
    return pl.pallas_call(
        functools.partial(_router_kernel, tm=tm),
        grid=(b, s // tm),
        in_specs=[pl.BlockSpec((1, tm, d), lambda i, j: (i, j, 0)),
                  _const_spec(w_router_t), _const_spec(b_col)],
        out_specs=pl.BlockSpec((1, tm, LANES), lambda i, j: (i, j, 0)),
        out_shape=jax.ShapeDtypeStruct((b, s, LANES), F32),
        compiler_params=_params("arbitrary", "arbitrary"),
        name="router",
    )(h2, w_router_t, b_col)


def _moe_kernel(h_ref, c_ref, x1_ref, g2_ref, wg_ref, wu_ref, wd_ref, wgs, wus, wds, gpost,
                y_o, acc_ref, *, tm):
    e = pl.program_id(2)
    ne = pl.num_programs(2)
    hb = h_ref[0]

    @pl.when(e == 0)
    def _():
        hid = _silu(_dot(hb, wgs[...])) * _dot(hb, wus[...])
        acc_ref[...] = _dot(hid.astype(BF16), wds[...])

    lane = lax.broadcasted_iota(jnp.int32, (tm, LANES), 1)
    ce = jnp.sum(jnp.where(lane == e, c_ref[0], 0.0), axis=-1, keepdims=True)
    gte = _dot(hb, wg_ref[0].astype(BF16))
    upe = _dot(hb, wu_ref[0].astype(BF16))
    act = (_silu(gte) * upe * ce).astype(BF16)
    acc_ref[...] += _dot(act, wd_ref[0].astype(BF16))

    @pl.when(e == ne - 1)
    def _():
        y_o[0] = x1_ref[0] + g2_ref[0] * _rms(acc_ref[...], gpost[...])


def _moe(h2, comb, x1, g2, w_gate_e, w_up_e, w_down_e, shared, gpost, tm):
    b, s, d = h2.shape
    per_row_mod = g2.shape[1] != 1
    mod_spec = (pl.BlockSpec((1, tm, d), lambda i, j, e: (i, j, 0)) if per_row_mod
                else pl.BlockSpec((1, 1, d), lambda i, j, e: (i, 0, 0)))

    def tok(n):
        return pl.BlockSpec((1, tm, n), lambda i, j, e: (i, j, 0))

    return pl.pallas_call(
        functools.partial(_moe_kernel, tm=tm),
        grid=(b, s // tm, N_EXPERTS),
        in_specs=[tok(d), tok(LANES), tok(d), mod_spec,
                  pl.BlockSpec((1, d, F_EXPERT), lambda i, j, e: (e, 0, 0)),
                  pl.BlockSpec((1, d, F_EXPERT), lambda i, j, e: (e, 0, 0)),
                  pl.BlockSpec((1, F_EXPERT, d), lambda i, j, e: (e, 0, 0))]
        + [_const_spec(w) for w in shared] + [_const_spec(gpost)],
        out_specs=tok(d),
        out_shape=jax.ShapeDtypeStruct((b, s, d), F32),
        scratch_shapes=[pltpu.VMEM((tm, d), F32)],
        compiler_params=_params("arbitrary", "arbitrary", "arbitrary"),
        name="moe",
    )(h2, comb, x1, g2, w_gate_e, w_up_e, w_down_e, *shared, gpost)


def _qlat_kernel(q_ref, wuk_ref, o_ref):
    for hd in range(A_HEADS):
        qn = q_ref[:, hd * HEAD_PAD:hd * HEAD_PAD + QK_NOPE]
        o_ref[hd] = _dot_nt(qn, wuk_ref[hd]).astype(BF16)


def _qlat(q, wuk_heads):
    nb = q.shape[0]
    return pl.pallas_call(
        _qlat_kernel,
        in_specs=[_const_spec(q), _const_spec(wuk_heads)],
        out_specs=pl.BlockSpec((A_HEADS, nb, KV_LORA), lambda: (0, 0, 0)),
        out_shape=jax.ShapeDtypeStruct((A_HEADS, nb, KV_LORA), BF16),
        grid=(),
        name="q_latent",
    )(q, wuk_heads)


def _decode_kernel(pt_ref, ql_ref, qp_ref, cn_ref, kn_ref, ckv_hbm, kpe_hbm, o_ref,
                   cbuf, kbuf, sem, *, pages_per_chunk, n_chunks):
    b = pl.program_id(0)
    cp = pages_per_chunk

    def copies(chunk, slot):
        out = []
        for p in range(cp):
            page = pt_ref[b, chunk * cp + p]
            out.append(pltpu.make_async_copy(ckv_hbm.at[page], cbuf.at[slot, p], sem.at[0, slot]))
            out.append(pltpu.make_async_copy(kpe_hbm.at[page], kbuf.at[slot, p], sem.at[1, slot]))
        return out

    for cpy in copies(0, 0):
        cpy.start()

    ql = ql_ref[0]
    qp = qp_ref[0]
    c_new = cn_ref[0]
    k_new = kn_ref[0]
    s_new = (jnp.sum(ql.astype(F32) * c_new, axis=-1, keepdims=True)
             + jnp.sum(qp.astype(F32) * k_new, axis=-1, keepdims=True)) * ATTN_SCALE
    m0 = s_new
    l0 = jnp.ones((A_HEADS, 1), F32)
    acc0 = jnp.broadcast_to(c_new, (A_HEADS, KV_LORA))

    def body(chunk, carry):
        m_prev, l_prev, acc = carry
        slot = lax.rem(chunk, 2)

        @pl.when(chunk + 1 < n_chunks)
        def _():
            for cpy in copies(chunk + 1, 1 - slot):
                cpy.start()

        for cpy in copies(chunk, slot):
            cpy.wait()
        kc = cbuf[slot].reshape(cp * PAGE_SIZE, KV_LORA).astype(BF16)
        kr = kbuf[slot].reshape(cp * PAGE_SIZE, QK_ROPE).astype(BF16)
        s = (_dot_nt(ql, kc) + _dot_nt(qp, kr)) * ATTN_SCALE
        m_new = jnp.maximum(m_prev, jnp.max(s, axis=-1, keepdims=True))
        alpha = jnp.exp(m_prev - m_new)
        p = jnp.exp(s - m_new)
        l_new = alpha * l_prev + jnp.sum(p, axis=-1, keepdims=True)
        acc = alpha * acc + _dot(p.astype(BF16), kc)
        return m_new, l_new, acc

    _, l_fin, acc = lax.fori_loop(0, n_chunks, body, (m0, l0, acc0))
    o_ref[0] = acc / l_fin


def _decode_attention(page_table, qlat, qpe, ckv_new, kpe_new, cache_ckv, cache_kpe):
    nb, n_pages = page_table.shape
    cp = min(16, n_pages)
    n_chunks = n_pages // cp
    grid_spec = pltpu.PrefetchScalarGridSpec(
        num_scalar_prefetch=1,
        grid=(nb,),
        in_specs=[pl.BlockSpec((1, A_HEADS, KV_LORA), lambda i, pt: (i, 0, 0)),
                  pl.BlockSpec((1, A_HEADS, QK_ROPE), lambda i, pt: (i, 0, 0)),
                  pl.BlockSpec((1, 1, KV_LORA), lambda i, pt: (i, 0, 0)),
                  pl.BlockSpec((1, 1, QK_ROPE), lambda i, pt: (i, 0, 0)),
                  pl.BlockSpec(memory_space=pl.ANY),
                  pl.BlockSpec(memory_space=pl.ANY)],
        out_specs=pl.BlockSpec((1, A_HEADS, KV_LORA), lambda i, pt: (i, 0, 0)),
        scratch_shapes=[pltpu.VMEM((2, cp, PAGE_SIZE, KV_LORA), F32),
                        pltpu.VMEM((2, cp, PAGE_SIZE, QK_ROPE), F32),
                        pltpu.SemaphoreType.DMA((2, 2))],
    )
    return pl.pallas_call(
        functools.partial(_decode_kernel, pages_per_chunk=cp, n_chunks=n_chunks),
        grid_spec=grid_spec,
        out_shape=jax.ShapeDtypeStruct((nb, A_HEADS, KV_LORA), F32),
        compiler_params=_params("arbitrary"),
        name="decode_attention",
    )(page_table, qlat, qpe, ckv_new, kpe_new, cache_ckv, cache_kpe)


def _vup_kernel(ol_ref, wuv_ref, o_ref):
    for hd in range(A_HEADS):
        o_ref[:, hd * V_DIM:(hd + 1) * V_DIM] = _dot(ol_ref[hd].astype(BF16), wuv_ref[hd]).astype(BF16)


def _value_up(o_lat_heads, wuv_heads):
    nb = o_lat_heads.shape[1]
    return pl.pallas_call(
        _vup_kernel,
        in_specs=[_const_spec(o_lat_heads), _const_spec(wuv_heads)],
        out_specs=pl.BlockSpec((nb, A_HEADS * V_DIM), lambda: (0, 0)),
        out_shape=jax.ShapeDtypeStruct((nb, A_HEADS * V_DIM), BF16),
        grid=(),
        name="value_up",
    )(o_lat_heads, wuv_heads)


def _ssd_step_kernel(xbc_ref, cs_ref, z_ref, dt_ref, st_ref, cw_ref, cb_ref, dtb_ref, na_ref,
                     dsk_ref, gs_ref, y_o, st_o, y_ref):
    conv = cb_ref[...] + cw_ref[CONV_W - 1:CONV_W, :] * xbc_ref[0]
    for j in range(CONV_W - 1):
        conv = conv + cw_ref[j:j + 1, :] * cs_ref[0, j:j + 1, :]
    xbc = _silu(conv)
    dt = _softplus(dt_ref[0] + dtb_ref[...])
    dec = jnp.exp(dt * na_ref[...])
    rows = lax.broadcasted_iota(jnp.int32, (S_HEAD_DIM, S_HEAD_DIM), 0)
    cols = lax.broadcasted_iota(jnp.int32, (S_HEAD_DIM, S_HEAD_DIM), 1)
    eye = rows == cols
    for hd in range(S_HEADS):
        g = hd // GROUP_HEADS
        hsl = slice(hd * S_HEAD_DIM, (hd + 1) * S_HEAD_DIM)
        bm = xbc[:, D_INNER + g * D_STATE:D_INNER + (g + 1) * D_STATE]
        cm = xbc[:, D_INNER + (S_GROUPS + g) * D_STATE:D_INNER + (S_GROUPS + g + 1) * D_STATE]
        xdt = xbc[:, hsl] * dt[:, hd:hd + 1]
        diag = jnp.where(eye, jnp.broadcast_to(xdt, (S_HEAD_DIM, S_HEAD_DIM)), 0.0)
        bb = jnp.broadcast_to(bm, (S_HEAD_DIM, D_STATE))
        upd = sum(_dot(part, bb.astype(BF16)) for part in _split3(diag))
        new = st_ref[0, hd] * dec[:, hd:hd + 1] + upd
        st_o[0, hd] = new
        cb8 = jnp.broadcast_to(cm, (8, D_STATE)).astype(BF16)
        y_ref[:, hsl] = _dot_nt(cb8, new.astype(BF16))
    y = y_ref[0:1, :] + dsk_ref[...] * xbc[:, :D_INNER]
    y = y * _silu(z_ref[0])
    for g in range(S_GROUPS):
        gsl = slice(g * GROUP_CH, (g + 1) * GROUP_CH)
        y_o[0, :, gsl] = _rms(y[:, gsl], gs_ref[:, gsl]).astype(BF16)


def _ssd_sample(xbc, conv_state, z, dt, ssm_state, conv_w, conv_b, dtb_row, na_row, dskip, g_ssd):
    nb = xbc.shape[0]
    consts = (conv_w, conv_b, dtb_row, na_row, dskip, g_ssd)
    st_spec = pl.BlockSpec((1, S_HEADS, S_HEAD_DIM, D_STATE), lambda i: (i, 0, 0, 0))
    return pl.pallas_call(
        _ssd_step_kernel,
        grid=(nb,),
        in_specs=[pl.BlockSpec((1, 1, CONV_CH), lambda i: (i, 0, 0)),
                  pl.BlockSpec((1, CONV_W - 1, CONV_CH), lambda i: (i, 0, 0)),
                  pl.BlockSpec((1, 1, D_INNER), lambda i: (i, 0, 0)),
                  pl.BlockSpec((1, 1, LANES), lambda i: (i, 0, 0)),
                  st_spec] + [_const_spec(a) for a in consts],
        out_specs=[pl.BlockSpec((1, 1, D_INNER), lambda i: (i, 0, 0)), st_spec],
        out_shape=[jax.ShapeDtypeStruct((nb, 1, D_INNER), BF16),
                   jax.ShapeDtypeStruct(ssm_state.shape, F32)],
        scratch_shapes=[pltpu.VMEM((8, D_INNER), F32)],
        compiler_params=_params("arbitrary"),
        name="ssd_sample",
    )(xbc, conv_state, z, dt, ssm_state, *consts)


def _rot_half(w):
    half = QK_ROPE // 2
    return jnp.concatenate([-w[..., half:], w[..., :half]], axis=-1)


def _pad_cols(w, start, total):
    return jnp.pad(w, ((0, 0), (start, total - start - w.shape[1])))


def _head_pad(w_nope, w_rope):
    k = w_nope.shape[0]
    pad = jnp.zeros((k, A_HEADS, HEAD_PAD - QK_NOPE - QK_ROPE), w_nope.dtype)
    return jnp.concatenate([w_nope, w_rope, pad], axis=-1).reshape(k, A_HEADS * HEAD_PAD)


def _rope_tables(pos):
    half = QK_ROPE // 2
    inv = ROPE_BASE ** (-jnp.arange(half, dtype=F32) / half)
    ang = pos.astype(F32)[:, None] * inv[None, :]
    cos, sin = jnp.cos(ang), jnp.sin(ang)
    n = pos.shape[0]
    ctab = jnp.concatenate([jnp.ones((n, QK_NOPE), F32), cos, cos,
                            jnp.zeros((n, HEAD_PAD - QK_NOPE - QK_ROPE), F32)], axis=1)
    stab = jnp.concatenate([jnp.zeros((n, QK_NOPE), F32), sin, sin,
                            jnp.zeros((n, HEAD_PAD - QK_NOPE - QK_ROPE), F32)], axis=1)
    return ctab, stab


def _pick_tile(n, target):
    t = min(n, target)
    assert n % t == 0, (n, t)
    return t


def kernel(x_prompt, x_sample, cache_ckv, cache_kpe, state_conv, state_ssm, page_table, c_prompt, c_sample, w_ada, b_ada, g_pre_mix, g_post_mix, g_pre_ffn, g_post_ffn, w_in, g_q_a, w_q_b, g_kv_a, w_uk, w_uv, w_o_attn, conv_w, conv_b, dt_bias, a_log, d_skip, g_ssd, w_o_ssd, w_out, w_router, b_router, w_gate_e, w_up_e, w_down_e, w_gate_s, w_up_s, w_down_s):
    bp, sp, d = x_prompt.shape
    nb, ds, _ = x_sample.shape
    depth = w_in.shape[0]
    assert depth == 1 and ds == 1 and d == D_MODEL
    n_pages = page_table.shape[1]
    past_len = n_pages * PAGE_SIZE
    lyr = 0

    offs = [0]
    for n in IN_SPLITS:
        offs.append(offs[-1] + n)
    win = w_in[lyr]
    w_qa, w_kva, w_kpe, w_z, w_xbc, w_dt, w_ga, w_gb = (win[:, offs[i]:offs[i + 1]] for i in range(8))
    wqb = w_q_b[lyr].reshape(Q_LORA, A_HEADS, QK_NOPE + QK_ROPE)
    wq_pad = _head_pad(wqb[..., :QK_NOPE], wqb[..., QK_NOPE:])
    wq_rot = _head_pad(jnp.zeros_like(wqb[..., :QK_NOPE]), _rot_half(wqb[..., QK_NOPE:]))
    wuk_pad = _head_pad(w_uk[lyr], jnp.zeros((KV_LORA, A_HEADS, QK_ROPE), F32))
    premix_w = [
        w_qa.astype(BF16), w_kva.astype(BF16),
        _pad_cols(w_kpe, QK_NOPE, LANES).astype(BF16),
        _pad_cols(_rot_half(w_kpe), QK_NOPE, LANES).astype(BF16),
        w_z.astype(BF16), w_xbc.astype(BF16),
        _pad_cols(w_dt, 0, LANES).astype(BF16), w_dt.T.astype(BF16),
        w_ga.astype(BF16), w_gb.astype(BF16),
        g_q_a[lyr][None, :], wq_pad.astype(BF16), wq_rot.astype(BF16),
        g_kv_a[lyr][None, :], wuk_pad.astype(BF16),
        w_uv[lyr].reshape(KV_LORA, A_HEADS * V_DIM).astype(BF16),
    ]
    merge_w = [w_o_attn[lyr].astype(BF16), w_o_ssd[lyr].astype(BF16), w_out[lyr].astype(BF16),
               g_post_mix[lyr][None, :], g_pre_ffn[lyr][None, :]]
    shared_w = [w_gate_s[lyr].astype(BF16), w_up_s[lyr].astype(BF16), w_down_s[lyr].astype(BF16)]
    w_router_t = w_router[lyr].T.astype(BF16)
    b_router_col = b_router[lyr][:, None]
    g_pre = g_pre_mix[lyr][None, :]
    g_post_ffn_row = g_post_ffn[lyr][None, :]
    cw = conv_w[lyr]
    cb = conv_b[lyr][None, :]
    neg_a = -jnp.exp(a_log[lyr].astype(F32))
    dtb_row = jnp.pad(dt_bias[lyr], (0, LANES - S_HEADS))[None, :]
    na_row = jnp.pad(neg_a, (0, LANES - S_HEADS))[None, :]
    dtb_col = dt_bias[lyr][:, None]
    na_col = neg_a[:, None]
    dskip_row = jnp.repeat(d_skip[lyr].astype(F32), S_HEAD_DIM)[None, :]
    gssd_row = g_ssd[lyr][None, :]

    n_mod_rows = bp + nb
    pad_rows = (-n_mod_rows) % 16
    c_all = jnp.concatenate([c_prompt, c_sample, jnp.zeros((pad_rows, d), F32)], axis=0)
    mod = _adaln(c_all, w_ada[lyr], b_ada[lyr][None, :])
    mods = [mod[:, i * d:(i + 1) * d] for i in range(6)]
    mods_p = [m[:bp][:, None, :] for m in mods]
    mods_s = [m[bp:bp + nb][None, :, :] for m in mods]

    ctab_p, stab_p = _rope_tables(jnp.arange(sp))
    tm = _pick_tile(sp, 256)
    (q_p, k_p, v_p, ckv_p, kpe_pad_p, z_p, xbc_p, dt_p, dtt_p, ga_p, gb_p) = _premix(
        x_prompt, mods_p[0], mods_p[1], g_pre, ctab_p, stab_p, premix_w, tm)
    ta = _pick_tile(sp, 512)
    o_attn_p = _flash_attention(q_p, k_p, v_p, ta, ta)
    y_ssd_p, ssm_t_p = _ssd_prompt(xbc_p, z_p, dt_p, dtt_p, cw, cb, dtb_row, dtb_col, na_row, na_col,
                                   dskip_row, gssd_row)
    x1_p, h2_p = _merge(x_prompt, o_attn_p, y_ssd_p, ga_p, gb_p, mods_p[2], mods_p[3], mods_p[4],
                        merge_w, tm)
    comb_p = _router(h2_p, w_router_t, b_router_col, tm)
    y_prompt = _moe(h2_p, comb_p, x1_p, mods_p[5], w_gate_e[lyr], w_up_e[lyr], w_down_e[lyr],
                    shared_w, g_post_ffn_row, _pick_tile(sp, 1024))

    xs = x_sample.reshape(1, nb, d)
    ctab_s, stab_s = _rope_tables(past_len + jnp.arange(ds))
    (q_s, _, _, ckv_s, kpe_pad_s, z_s, xbc_s, dt_s, _, ga_s, gb_s) = _premix(
        xs, mods_s[0], mods_s[1], g_pre, ctab_s, stab_s, premix_w, nb)
    kpe_s = kpe_pad_s[0, :, QK_NOPE:QK_NOPE + QK_ROPE]
    wuk_heads = jnp.transpose(w_uk[lyr], (1, 0, 2)).astype(BF16)
    wuv_heads = jnp.transpose(w_uv[lyr], (1, 0, 2)).astype(BF16)
    qlat = jnp.transpose(_qlat(q_s[0], wuk_heads), (1, 0, 2))
    qpe = q_s[0].reshape(nb, A_HEADS, HEAD_PAD)[:, :, QK_NOPE:QK_NOPE + QK_ROPE]
    o_lat = _decode_attention(page_table, qlat, qpe, ckv_s.reshape(nb, 1, KV_LORA),
                              kpe_s.reshape(nb, 1, QK_ROPE),
                              cache_ckv.reshape(cache_ckv.shape[1:]), cache_kpe.reshape(cache_kpe.shape[1:]))
    o_attn_s = _value_up(jnp.transpose(o_lat, (1, 0, 2)), wuv_heads)[None]
    y_ssd_s, ssm_s = _ssd_sample(xbc_s.reshape(nb, 1, CONV_CH), state_conv[lyr],
                                 z_s.reshape(nb, 1, D_INNER), dt_s.reshape(nb, 1, LANES),
                                 state_ssm.reshape(state_ssm.shape[1:]), cw, cb, dtb_row, na_row,
                                 dskip_row, gssd_row)
    x1_s, h2_s = _merge(xs, o_attn_s, y_ssd_s.reshape(1, nb, D_INNER), ga_s, gb_s,
                        mods_s[2], mods_s[3], mods_s[4], merge_w, nb)
    comb_s = _router(h2_s, w_router_t, b_router_col, nb)
    y_s = _moe(h2_s, comb_s, x1_s, mods_s[5], w_gate_e[lyr], w_up_e[lyr], w_down_e[lyr],
               shared_w, g_post_ffn_row, nb)

    kpe_p = kpe_pad_p[:, :, QK_NOPE:QK_NOPE + QK_ROPE]
    conv_p = xbc_p[:, sp - (CONV_W - 1):, :]
    ssm_p = jnp.transpose(ssm_t_p.reshape(bp, D_STATE, S_HEADS, S_HEAD_DIM), (0, 2, 3, 1))
    conv_s = jnp.concatenate([state_conv[lyr][:, 1:, :], xbc_s.reshape(nb, 1, CONV_CH)], axis=1)
    return (y_prompt, y_s.reshape(nb, ds, d),
            ckv_p[None], kpe_p[None], conv_p[None], ssm_p[None].astype(x_prompt.dtype),
            ckv_s.reshape(1, nb, ds, KV_LORA), kpe_s.reshape(1, nb, ds, QK_ROPE),
            conv_s[None], ssm_s[None].astype(x_sample.dtype))
```

```python
import functools

import jax
import jax.numpy as jnp
from jax import lax
from jax.experimental import pallas as pl
from jax.experimental.pallas import tpu as pltpu

F32 = jnp.float32
BF16 = jnp.bfloat16

D_MODEL = 1024
PAGE_SIZE = 128
A_HEADS = 8
QK_NOPE = 64
QK_ROPE = 32
V_DIM = 64
Q_LORA = 384
KV_LORA = 256
ROPE_BASE = 10000.0
ATTN_SCALE = (QK_NOPE + QK_ROPE) ** -0.5
S_HEADS = 16
S_HEAD_DIM = 64
D_INNER = S_HEADS * S_HEAD_DIM
S_GROUPS = 2
D_STATE = 128
CONV_W = 4
CONV_CH = D_INNER + 2 * S_GROUPS * D_STATE
CHUNK = 128
N_EXPERTS = 64
TOP_K = 6
N_EXP_GROUPS = 8
TOPK_GROUPS = 4
F_EXPERT = 256
ROUTE_SCALE = 2.5
EPS = 1e-6
IN_SPLITS = (Q_LORA, KV_LORA, QK_ROPE, D_INNER, CONV_CH, S_HEADS, D_MODEL, D_MODEL)

LANES = 128
HEAD_PAD = LANES
GROUP_HEADS = S_HEADS // S_GROUPS
GROUP_CH = D_INNER // S_GROUPS
EXP_PER_GROUP = N_EXPERTS // N_EXP_GROUPS
KEY_GROUP = 8
PAGE_ROWS = PAGE_SIZE // KEY_GROUP
CKV_ROW = KEY_GROUP * KV_LORA
KPE_ROW = KEY_GROUP * QK_ROPE
VMEM_LIMIT_BYTES = 56 * 1024 * 1024

NEG_INF = float("-inf")
LOG2_E = 1.4426950408889634


def _params(*semantics):
    return pltpu.CompilerParams(dimension_semantics=semantics, vmem_limit_bytes=VMEM_LIMIT_BYTES)


def _const_spec(arr):
    nd = arr.ndim
    return pl.BlockSpec(arr.shape, lambda *_: (0,) * nd)


def _dot(a, b):
    return jnp.dot(a, b, preferred_element_type=F32)


def _dot_nt(a, b):
    return lax.dot_general(a, b, (((1,), (1,)), ((), ())), preferred_element_type=F32)


def _dot_tn(a, b):
    return lax.dot_general(a, b, (((0,), (0,)), ((), ())), preferred_element_type=F32)


def _rms(x, g):
    return x * lax.rsqrt(jnp.mean(x * x, axis=-1, keepdims=True) + EPS) * g


def _silu(x):
    return x * (1.0 / (1.0 + jnp.exp(-x)))


def _sigmoid(x):
    return 1.0 / (1.0 + jnp.exp(-x))


def _softplus(x):
    return jnp.maximum(x, 0.0) + jnp.log(1.0 + jnp.exp(-jnp.abs(x)))


def _split3(x):
    hi = x.astype(BF16)
    r1 = x - hi.astype(F32)
    mid = r1.astype(BF16)
    lo = (r1 - mid.astype(F32)).astype(BF16)
    return hi, mid, lo


def _adaln_kernel(c_ref, w_ref, b_ref, o_ref):
    c = _silu(c_ref[...]).astype(BF16)
    o_ref[...] = _dot(c, w_ref[...].astype(BF16)) + b_ref[...]


def _adaln(c_all, w_ada, b_ada):
    rows, d = c_all.shape
    n = w_ada.shape[1]
    tn = 512
    return pl.pallas_call(
        _adaln_kernel,
        grid=(n // tn,),
        in_specs=[pl.BlockSpec((rows, d), lambda j: (0, 0)),
                  pl.BlockSpec((d, tn), lambda j: (0, j)),
                  pl.BlockSpec((1, tn), lambda j: (0, j))],
        out_specs=pl.BlockSpec((rows, tn), lambda j: (0, j)),
        out_shape=jax.ShapeDtypeStruct((rows, n), F32),
        compiler_params=_params("arbitrary"),
        name="adaln",
    )(c_all, w_ada, b_ada)


def _premix_kernel(x_ref, sh_ref, sc_ref, g_ref, ct_ref, st_ref,
                   wqa, wkva, wkpe, wkper, wz, wxbc, wdt, wdtt, wga, wgb,
                   gqa, wq, wqr, gkv, wuk, wuv,
                   q_o, k_o, v_o, ckv_o, kpe_o, z_o, xbc_o, dt_o, dtt_o, ga_o, gb_o):
    x = x_ref[0]
    h = _rms(x, g_ref[...]) * (1.0 + sc_ref[0]) + sh_ref[0]
    hb = h.astype(BF16)
    ct = ct_ref[...]
    st = st_ref[...]

    qn = _rms(_dot(hb, wqa[...]), gqa[...]).astype(BF16)
    q = _dot(qn, wq[...])
    qr = _dot(qn, wqr[...])
    for hd in range(A_HEADS):
        sl = slice(hd * HEAD_PAD, (hd + 1) * HEAD_PAD)
        q_o[0, :, sl] = (q[:, sl] * ct + qr[:, sl] * st).astype(BF16)

    ckv = _rms(_dot(hb, wkva[...]), gkv[...])
    ckv_o[0] = ckv
    cb = ckv.astype(BF16)
    kpe = _dot(hb, wkpe[...]) * ct + _dot(hb, wkper[...]) * st
    kpe_o[0] = kpe
    kn = _dot(cb, wuk[...])
    for hd in range(A_HEADS):
        sl = slice(hd * HEAD_PAD, (hd + 1) * HEAD_PAD)
        k_o[0, :, sl] = (kn[:, sl] + kpe).astype(BF16)
    v_o[0] = _dot(cb, wuv[...]).astype(BF16)

    z_o[0] = _dot(hb, wz[...])
    xbc_o[0] = _dot(hb, wxbc[...])
    dt_o[0] = _dot(hb, wdt[...])
    dtt_o[0] = _dot_nt(wdtt[...], hb)
    ga_o[0] = _dot(hb, wga[...])
    gb_o[0] = _dot(hb, wgb[...])


def _premix(x, sh, sc, g, ctab, stab, weights, tm):
    b, s, d = x.shape
    per_row_mod = sh.shape[1] != 1
    per_row_tab = ctab.shape[0] != 1
    mod_spec = (pl.BlockSpec((1, tm, d), lambda i, j: (i, j, 0)) if per_row_mod
                else pl.BlockSpec((1, 1, d), lambda i, j: (i, 0, 0)))
    tab_spec = (pl.BlockSpec((tm, LANES), lambda i, j: (j, 0)) if per_row_tab
                else pl.BlockSpec((1, LANES), lambda i, j: (0, 0)))

    def tok(n):
        return pl.BlockSpec((1, tm, n), lambda i, j: (i, j, 0))

    out_cols = [(A_HEADS * HEAD_PAD, BF16), (A_HEADS * HEAD_PAD, BF16), (A_HEADS * V_DIM, BF16),
                (KV_LORA, F32), (LANES, F32), (D_INNER, F32), (CONV_CH, F32), (LANES, F32)]
    out_shape = [jax.ShapeDtypeStruct((b, s, n), dt) for n, dt in out_cols]
    out_specs = [tok(n) for n, _ in out_cols]
    out_shape.append(jax.ShapeDtypeStruct((b, S_HEADS, s), F32))
    out_specs.append(pl.BlockSpec((1, S_HEADS, tm), lambda i, j: (i, 0, j)))
    out_shape += [jax.ShapeDtypeStruct((b, s, D_MODEL), F32)] * 2
    out_specs += [tok(D_MODEL)] * 2
    order = [out_shape[i] for i in (0, 1, 2, 3, 4, 5, 6, 7, 8, 9, 10)]
    return pl.pallas_call(
        _premix_kernel,
        grid=(b, s // tm),
        in_specs=[tok(d), mod_spec, mod_spec, _const_spec(g), tab_spec, tab_spec]
        + [_const_spec(w) for w in weights],
        out_specs=out_specs,
        out_shape=order,
        compiler_params=_params("arbitrary", "arbitrary"),
        name="premix",
    )(x, sh, sc, g, ctab, stab, *weights)


def _flash_kernel(q_ref, k_ref, v_ref, o_ref, m_ref, l_ref, acc_ref, *, tq, heads):
    qi = pl.program_id(2)
    m_ref[...] = jnp.full(m_ref.shape, NEG_INF, F32)
    l_ref[...] = jnp.zeros(l_ref.shape, F32)
    acc_ref[...] = jnp.zeros(acc_ref.shape, F32)
    row = lax.broadcasted_iota(jnp.int32, (tq, tq), 0)
    col = lax.broadcasted_iota(jnp.int32, (tq, tq), 1)
    c2 = ATTN_SCALE * LOG2_E

    def tile(kj, on_diagonal):
        koff = pl.multiple_of(kj * tq, tq)
        for hh in range(heads):
            sl = slice(hh * HEAD_PAD, (hh + 1) * HEAD_PAD)
            vsl = slice((hh // 2) * 2 * V_DIM, (hh // 2 + 1) * 2 * V_DIM)
            s = _dot_nt(q_ref[0, :, sl], k_ref[0, pl.ds(koff, tq), sl])
            if on_diagonal:
                s = jnp.where(col <= row, s, NEG_INF)
            m_prev = m_ref[hh]
            m_new = jnp.maximum(m_prev, jnp.max(s, axis=-1, keepdims=True))
            alpha = jnp.exp2((m_prev - m_new) * c2)
            p_parts = [jnp.exp2((s[:, j * LANES:(j + 1) * LANES] - m_new) * c2)
                       for j in range(tq // LANES)]
            l_ref[hh] = alpha * l_ref[hh] + sum(p_parts)
            p = jnp.concatenate(p_parts, axis=1).astype(BF16)
            acc_ref[hh] = alpha * acc_ref[hh] + _dot(p, v_ref[0, pl.ds(koff, tq), vsl])
            m_ref[hh] = m_new

    def body(kj, carry):
        tile(kj, False)
        return carry

    lax.fori_loop(0, qi, body, 0)
    tile(qi, True)
    lane = lax.broadcasted_iota(jnp.int32, (tq, 2 * V_DIM), 1)
    for pair in range(heads // 2):
        o0 = acc_ref[2 * pair] / jnp.sum(l_ref[2 * pair], axis=-1, keepdims=True)
        o1 = acc_ref[2 * pair + 1] / jnp.sum(l_ref[2 * pair + 1], axis=-1, keepdims=True)
        o_ref[0, :, pair * 2 * V_DIM:(pair + 1) * 2 * V_DIM] = jnp.where(lane < V_DIM, o0, o1).astype(BF16)


def _flash_attention(q, k, v, tq, heads):
    b, s, _ = q.shape
    return pl.pallas_call(
        functools.partial(_flash_kernel, tq=tq, heads=heads),
        grid=(b, A_HEADS // heads, s // tq),
        in_specs=[pl.BlockSpec((1, tq, heads * HEAD_PAD), lambda bi, hg, qi: (bi, qi, hg)),
                  pl.BlockSpec((1, s, heads * HEAD_PAD), lambda bi, hg, qi: (bi, 0, hg)),
                  pl.BlockSpec((1, s, heads * V_DIM), lambda bi, hg, qi: (bi, 0, hg))],
        out_specs=pl.BlockSpec((1, tq, heads * V_DIM), lambda bi, hg, qi: (bi, qi, hg)),
        out_shape=jax.ShapeDtypeStruct((b, s, A_HEADS * V_DIM), BF16),
        scratch_shapes=[pltpu.VMEM((heads, tq, LANES), F32), pltpu.VMEM((heads, tq, LANES), F32),
                        pltpu.VMEM((heads, tq, 2 * V_DIM), F32)],
        compiler_params=_params("arbitrary", "arbitrary", "arbitrary"),
        name="flash_attention",
    )(q, k, v)


def _ssd_chunk_kernel(xbc_ref, z_ref, dt_ref, dtt_ref, cw_ref, cb_ref, dtb_ref, dtbc_ref,
                      na_ref, nac_ref, dsk_ref, gs_ref, y_o, st_o, xc_ref, state_ref, y_ref):
    c = pl.program_id(1)
    nc = pl.num_programs(1)

    @pl.when(c == 0)
    def _():
        xc_ref[0:8, :] = jnp.zeros((8, CONV_CH), F32)
        state_ref[...] = jnp.zeros(state_ref.shape, F32)

    xc_ref[8:8 + CHUNK, :] = xbc_ref[0]
    conv = cb_ref[...] + cw_ref[CONV_W - 1:CONV_W, :] * xc_ref[8:8 + CHUNK, :]
    for j in range(1, CONV_W):
        conv = conv + cw_ref[CONV_W - 1 - j:CONV_W - j, :] * xc_ref[8 - j:8 - j + CHUNK, :]
    xc_ref[0:8, :] = xc_ref[CHUNK:CHUNK + 8, :]
    xbc = _silu(conv)

    dt_c = _softplus(dt_ref[0] + dtb_ref[...])
    a_c = dt_c * na_ref[...]
    dt_r = _softplus(dtt_ref[0] + dtbc_ref[...])
    a_r = dt_r * nac_ref[...]
    li = lax.broadcasted_iota(jnp.int32, (CHUNK, CHUNK), 0)
    si = lax.broadcasted_iota(jnp.int32, (CHUNK, CHUNK), 1)
    lower = (si <= li)
    tri = lower.astype(BF16)
    acum_c = sum(_dot(tri, part) for part in _split3(a_c))
    acum_r = sum(_dot_nt(part, tri) for part in _split3(a_r))
    a_last = acum_c[CHUNK - 1:CHUNK, :]
    dec_end = jnp.exp(a_last - acum_c)
    exp_ac = jnp.exp(acum_c)
    dec_all = jnp.exp(a_last)

    for g in range(S_GROUPS):
        bm = xbc[:, D_INNER + g * D_STATE:D_INNER + (g + 1) * D_STATE]
        cm = xbc[:, D_INNER + (S_GROUPS + g) * D_STATE:D_INNER + (S_GROUPS + g + 1) * D_STATE]
        bmb = bm.astype(BF16)
        cmb = cm.astype(BF16)
        gmat = _dot_nt(cmb, bmb)
        gsl = slice(g * GROUP_CH, (g + 1) * GROUP_CH)
        y_off = _dot(cmb, state_ref[:, gsl].astype(BF16))
        xde_parts = []
        for hh in range(GROUP_HEADS):
            hd = g * GROUP_HEADS + hh
            hsl = slice(hd * S_HEAD_DIM, (hd + 1) * S_HEAD_DIM)
            xs = xbc[:, hsl]
            xdt = xs * dt_c[:, hd:hd + 1]
            seg = acum_c[:, hd:hd + 1] - acum_r[hd:hd + 1, :]
            lmat = jnp.exp(jnp.where(lower, seg, NEG_INF))
            y_diag = _dot((gmat * lmat).astype(BF16), xdt.astype(BF16))
            yo = y_off[:, hh * S_HEAD_DIM:(hh + 1) * S_HEAD_DIM] * exp_ac[:, hd:hd + 1]
            y_ref[:, hsl] = y_diag + yo
            xde_parts.append((xdt * dec_end[:, hd:hd + 1]).astype(BF16))
        xde = jnp.concatenate(xde_parts, axis=1)
        new_states = _dot_tn(bmb, xde)
        for hh in range(GROUP_HEADS):
            hd = g * GROUP_HEADS + hh
            hsl = slice(hd * S_HEAD_DIM, (hd + 1) * S_HEAD_DIM)
            lsl = slice(hh * S_HEAD_DIM, (hh + 1) * S_HEAD_DIM)
            state_ref[:, hsl] = state_ref[:, hsl] * dec_all[:, hd:hd + 1] + new_states[:, lsl]

    y = y_ref[...] + dsk_ref[...] * xbc[:, :D_INNER]
    y = y * _silu(z_ref[0])
    for g in range(S_GROUPS):
        gsl = slice(g * GROUP_CH, (g + 1) * GROUP_CH)
        y_o[0, :, gsl] = _rms(y[:, gsl], gs_ref[:, gsl]).astype(BF16)

    @pl.when(c == nc - 1)
    def _():
        st_o[0] = state_ref[...]


def _ssd_prompt(xbc, z, dt, dtt, conv_w, conv_b, dtb_row, dtb_col, na_row, na_col, dskip, g_ssd):
    b, s, _ = xbc.shape
    nc = s // CHUNK
    consts = (conv_w, conv_b, dtb_row, dtb_col, na_row, na_col, dskip, g_ssd)
    return pl.pallas_call(
        _ssd_chunk_kernel,
        grid=(b, nc),
        in_specs=[pl.BlockSpec((1, CHUNK, CONV_CH), lambda i, j: (i, j, 0)),
                  pl.BlockSpec((1, CHUNK, D_INNER), lambda i, j: (i, j, 0)),
                  pl.BlockSpec((1, CHUNK, LANES), lambda i, j: (i, j, 0)),
                  pl.BlockSpec((1, S_HEADS, CHUNK), lambda i, j: (i, 0, j))]
        + [_const_spec(a) for a in consts],
        out_specs=[pl.BlockSpec((1, CHUNK, D_INNER), lambda i, j: (i, j, 0)),
                   pl.BlockSpec((1, D_STATE, D_INNER), lambda i, j: (i, 0, 0))],
        out_shape=[jax.ShapeDtypeStruct((b, s, D_INNER), BF16),
                   jax.ShapeDtypeStruct((b, D_STATE, D_INNER), F32)],
        scratch_shapes=[pltpu.VMEM((CHUNK + 8, CONV_CH), F32),
                        pltpu.VMEM((D_STATE, D_INNER), F32),
                        pltpu.VMEM((CHUNK, D_INNER), F32)],
        compiler_params=_params("arbitrary", "arbitrary"),
        name="ssd_prompt",
    )(xbc, z, dt, dtt, *consts)


def _merge_kernel(x_ref, oa_ref, ys_ref, ga_ref, gb_ref, g1_ref, sh2_ref, sc2_ref,
                  woa, wos, wout, gpost, gpre, x1_o, h2_o):
    o_attn = _dot(oa_ref[0], woa[...])
    o_ssd = _dot(ys_ref[0], wos[...])
    merged = _sigmoid(ga_ref[0]) * o_attn + _sigmoid(gb_ref[0]) * o_ssd
    m = _dot(merged.astype(BF16), wout[...])
    x1 = x_ref[0] + g1_ref[0] * _rms(m, gpost[...])
    x1_o[0] = x1
    h2_o[0] = (_rms(x1, gpre[...]) * (1.0 + sc2_ref[0]) + sh2_ref[0]).astype(BF16)


def _merge(x, o_attn, y_ssd, ga, gb, g1, sh2, sc2, weights, tm):
    b, s, d = x.shape
    per_row_mod = g1.shape[1] != 1
    mod_spec = (pl.BlockSpec((1, tm, d), lambda i, j: (i, j, 0)) if per_row_mod
                else pl.BlockSpec((1, 1, d), lambda i, j: (i, 0, 0)))

    def tok(n):
        return pl.BlockSpec((1, tm, n), lambda i, j: (i, j, 0))

    return pl.pallas_call(
        _merge_kernel,
        grid=(b, s // tm),
        in_specs=[tok(d), tok(A_HEADS * V_DIM), tok(D_INNER), tok(d), tok(d),
                  mod_spec, mod_spec, mod_spec] + [_const_spec(w) for w in weights],
        out_specs=[tok(d), tok(d)],
        out_shape=[jax.ShapeDtypeStruct((b, s, d), F32), jax.ShapeDtypeStruct((b, s, d), BF16)],
        compiler_params=_params("arbitrary", "arbitrary"),
        name="merge",
    )(x, o_attn, y_ssd, ga, gb, g1, sh2, sc2, *weights)


def _first_max(vals, idx, big):
    m = vals[0]
    for v in vals[1:]:
        m = jnp.maximum(m, v)
    m = jnp.max(m, axis=0, keepdims=True)
    cand = [jnp.where(v == m, i, big) for v, i in zip(vals, idx)]
    a = cand[0]
    for cnd in cand[1:]:
        a = jnp.minimum(a, cnd)
    a = jnp.min(a, axis=0, keepdims=True)
    return m, a


def _router_kernel(h_ref, wt_ref, b_ref, o_ref, *, tm):
    logits = _dot_nt(wt_ref[...], h_ref[0])
    scores = _sigmoid(logits)
    biased = scores + b_ref[...]
    sub = lax.broadcasted_iota(jnp.int32, (EXP_PER_GROUP, tm), 0)
    slabs = [biased[g * EXP_PER_GROUP:(g + 1) * EXP_PER_GROUP, :] for g in range(N_EXP_GROUPS)]
    big = jnp.int32(N_EXPERTS)

    gscore = []
    for g in range(N_EXP_GROUPS):
        m1, a1 = _first_max([slabs[g]], [sub], big)
        rest = jnp.where(sub == a1, NEG_INF, slabs[g])
        m2 = jnp.max(rest, axis=0, keepdims=True)
        gscore.append(m1 + m2)
    gs = jnp.full((N_EXP_GROUPS, tm), NEG_INF, F32)
    for g in range(N_EXP_GROUPS):
        gs = jnp.where(sub == g, gscore[g], gs)
    gsel = jnp.zeros((N_EXP_GROUPS, tm), F32)
    for _ in range(TOPK_GROUPS):
        _, a = _first_max([gs], [sub], big)
        hit = sub == a
        gsel = jnp.where(hit, 1.0, gsel)
        gs = jnp.where(hit, NEG_INF, gs)

    masked = [jnp.where(gsel[g:g + 1, :] > 0.5, slabs[g], NEG_INF) for g in range(N_EXP_GROUPS)]
    flat = [sub + g * EXP_PER_GROUP for g in range(N_EXP_GROUPS)]
    chosen = [jnp.zeros((EXP_PER_GROUP, tm), jnp.bool_) for _ in range(N_EXP_GROUPS)]
    for _ in range(TOP_K):
        _, a = _first_max(masked, flat, big)
        for g in range(N_EXP_GROUPS):
            hit = flat[g] == a
            chosen[g] = jnp.logical_or(chosen[g], hit)
            masked[g] = jnp.where(hit, NEG_INF, masked[g])
    w = [jnp.where(chosen[g], scores[g * EXP_PER_GROUP:(g + 1) * EXP_PER_GROUP, :], 0.0)
         for g in range(N_EXP_GROUPS)]
    tot = w[0]
    for part in w[1:]:
        tot = tot + part
    tot = jnp.sum(tot, axis=0, keepdims=True)
    comb_t = jnp.concatenate([part / tot * ROUTE_SCALE for part in w]
                             + [jnp.zeros((LANES - N_EXPERTS, tm), F32)], axis=0)
    o_ref[0] = comb_t.T


def _router(h2, w_router_t, b_col, tm):
    b, s, d = h2.shape
    return pl.pallas_call(
        functools.partial(_router_kernel, tm=tm),
        grid=(b, s // tm),
        in_specs=[pl.BlockSpec((1, tm, d), lambda i, j: (i, j, 0)),
                  _const_spec(w_router_t), _const_spec(b_col)],
        out_specs=pl.BlockSpec((1, tm, LANES), lambda i, j: (i, j, 0)),
        out_shape=jax.ShapeDtypeStruct((b, s, LANES), F32),
        compiler_params=_params("arbitrary", "arbitrary"),
        name="router",
    )(h2, w_router_t, b_col)


def _moe_kernel(h_ref, c_ref, x1_ref, g2_ref, wg_ref, wu_ref, wd_ref, wgs, wus, wds, gpost,
                y_o, acc_ref, *, tm):
    e = pl.program_id(2)
    ne = pl.num_programs(2)
    hb = h_ref[0]

    @pl.when(e == 0)
    def _():
        hid = _silu(_dot(hb, wgs[...])) * _dot(hb, wus[...])
        acc_ref[...] = _dot(hid.astype(BF16), wds[...])

    lane = lax.broadcasted_iota(jnp.int32, (tm, LANES), 1)
    ce = jnp.sum(jnp.where(lane == e, c_ref[0], 0.0), axis=-1, keepdims=True)
    gte = _dot(hb, wg_ref[0].astype(BF16))
    upe = _dot(hb, wu_ref[0].astype(BF16))
    act = (_silu(gte) * upe * ce).astype(BF16)
    acc_ref[...] += _dot(act, wd_ref[0].astype(BF16))

    @pl.when(e == ne - 1)
    def _():
        y_o[0] = x1_ref[0] + g2_ref[0] * _rms(acc_ref[...], gpost[...])


def _moe(h2, comb, x1, g2, w_gate_e, w_up_e, w_down_e, shared, gpost, tm):
    b, s, d = h2.shape
    per_row_mod = g2.shape[1] != 1
    mod_spec = (pl.BlockSpec((1, tm, d), lambda i, j, e: (i, j, 0)) if per_row_mod
                else pl.BlockSpec((1, 1, d), lambda i, j, e: (i, 0, 0)))

    def tok(n):
        return pl.BlockSpec((1, tm, n), lambda i, j, e: (i, j, 0))

    return pl.pallas_call(
        functools.partial(_moe_kernel, tm=tm),
        grid=(b, s // tm, N_EXPERTS),
        in_specs=[tok(d), tok(LANES), tok(d), mod_spec,
                  pl.BlockSpec((1, d, F_EXPERT), lambda i, j, e: (e, 0, 0)),
                  pl.BlockSpec((1, d, F_EXPERT), lambda i, j, e: (e, 0, 0)),
                  pl.BlockSpec((1, F_EXPERT, d), lambda i, j, e: (e, 0, 0))]
        + [_const_spec(w) for w in shared] + [_const_spec(gpost)],
        out_specs=tok(d),
        out_shape=jax.ShapeDtypeStruct((b, s, d), F32),
        scratch_shapes=[pltpu.VMEM((tm, d), F32)],
        compiler_params=_params("arbitrary", "arbitrary", "arbitrary"),
        name="moe",
    )(h2, comb, x1, g2, w_gate_e, w_up_e, w_down_e, *shared, gpost)


def _qlat_kernel(q_ref, wuk_ref, o_ref):
    for hd in range(A_HEADS):
        qn = q_ref[:, hd * HEAD_PAD:hd * HEAD_PAD + QK_NOPE]
        o_ref[hd] = _dot_nt(qn, wuk_ref[hd]).astype(BF16)


def _qlat(q, wuk_heads):
    nb = q.shape[0]
    return pl.pallas_call(
        _qlat_kernel,
        in_specs=[_const_spec(q), _const_spec(wuk_heads)],
        out_specs=pl.BlockSpec((A_HEADS, nb, KV_LORA), lambda: (0, 0, 0)),
        out_shape=jax.ShapeDtypeStruct((A_HEADS, nb, KV_LORA), BF16),
        grid=(),
        name="q_latent",
    )(q, wuk_heads)


def _decode_kernel(pt_ref, ql_ref, qpt_ref, cn_ref, kn_ref, ckv_hbm, kpe_hbm, o_ref,
                   cbuf, kbuf, sem, *, pages_per_chunk, n_chunks):
    b = pl.program_id(0)
    nb = pl.num_programs(0)
    cp = pages_per_chunk

    def copies(bb, chunk, slot):
        out = []
        for p in range(cp):
            page = pt_ref[bb, chunk * cp + p]
            out.append(pltpu.make_async_copy(ckv_hbm.at[page], cbuf.at[slot, p], sem.at[0, slot]))
            out.append(pltpu.make_async_copy(kpe_hbm.at[page], kbuf.at[slot, p], sem.at[1, slot]))
        return out

    @pl.when(b == 0)
    def _():
        for cpy in copies(0, 0, 0):
            cpy.start()

    ql = ql_ref[0].astype(F32)
    qpt = qpt_ref[0].astype(F32)
    c_new = cn_ref[0]
    k_new = kn_ref[0]
    s_new = (jnp.sum(ql * c_new, axis=-1, keepdims=True)
             + jnp.sum(qpt[:, :QK_ROPE] * k_new, axis=-1, keepdims=True)) * ATTN_SCALE
    m0 = s_new
    l0 = jnp.ones((A_HEADS, 1), F32)
    acc0 = jnp.broadcast_to(c_new, (A_HEADS, KV_LORA))

    rows = KEY_GROUP * A_HEADS
    ql_rep = jnp.concatenate([ql] * KEY_GROUP, axis=0)
    qp_rep = jnp.concatenate([qpt] * KEY_GROUP, axis=0)
    rblk = lax.broadcasted_iota(jnp.int32, (rows, KV_LORA), 0) // A_HEADS
    lblk = lax.broadcasted_iota(jnp.int32, (rows, KPE_ROW), 1) // QK_ROPE
    wq_nope = jnp.concatenate([jnp.where(rblk == i, ql_rep, 0.0) for i in range(KEY_GROUP)],
                              axis=1).astype(BF16)
    wq_pe = jnp.where(rblk == lblk, qp_rep, 0.0).astype(BF16)

    def fold(x):
        return [x[i * A_HEADS:(i + 1) * A_HEADS] for i in range(KEY_GROUP)]

    def body(chunk, carry):
        m_prev, l_prev, acc = carry
        slot = lax.rem(b * n_chunks + chunk, 2)

        @pl.when(chunk + 1 < n_chunks)
        def _():
            for cpy in copies(b, chunk + 1, 1 - slot):
                cpy.start()

        @pl.when(jnp.logical_and(chunk + 1 == n_chunks, b + 1 < nb))
        def _():
            for cpy in copies(b + 1, 0, 1 - slot):
                cpy.start()

        for cpy in copies(b, chunk, slot):
            cpy.wait()
        kc = cbuf[slot].reshape(cp * PAGE_ROWS, CKV_ROW).astype(BF16)
        kr = kbuf[slot].reshape(cp * PAGE_ROWS, KPE_ROW).astype(BF16)
        s = (_dot_nt(wq_nope, kc) + _dot_nt(wq_pe, kr)) * ATTN_SCALE
        m_chunk = functools.reduce(jnp.maximum, fold(jnp.max(s, axis=-1, keepdims=True)))
        m_new = jnp.maximum(m_prev, m_chunk)
        alpha = jnp.exp(m_prev - m_new)
        p = jnp.exp(s - jnp.concatenate([m_new] * KEY_GROUP, axis=0))
        l_new = alpha * l_prev + sum(fold(jnp.sum(p, axis=-1, keepdims=True)))
        o_all = _dot(p.astype(BF16), kc)
        o_new = sum(blk[:, i * KV_LORA:(i + 1) * KV_LORA] for i, blk in enumerate(fold(o_all)))
        return m_new, l_new, alpha * acc + o_new

    _, l_fin, acc = lax.fori_loop(0, n_chunks, body, (m0, l0, acc0))
    o_ref[0] = acc / l_fin


def _decode_attention(page_table, qlat, qpe_tiled, ckv_new, kpe_new, cache_ckv, cache_kpe):
    nb, n_pages = page_table.shape
    cp = min(16, n_pages)
    n_chunks = n_pages // cp
    grid_spec = pltpu.PrefetchScalarGridSpec(
        num_scalar_prefetch=1,
        grid=(nb,),
        in_specs=[pl.BlockSpec((1, A_HEADS, KV_LORA), lambda i, pt: (i, 0, 0)),
                  pl.BlockSpec((1, A_HEADS, KPE_ROW), lambda i, pt: (i, 0, 0)),
                  pl.BlockSpec((1, 1, KV_LORA), lambda i, pt: (i, 0, 0)),
                  pl.BlockSpec((1, 1, QK_ROPE), lambda i, pt: (i, 0, 0)),
                  pl.BlockSpec(memory_space=pl.ANY),
                  pl.BlockSpec(memory_space=pl.ANY)],
        out_specs=pl.BlockSpec((1, A_HEADS, KV_LORA), lambda i, pt: (i, 0, 0)),
        scratch_shapes=[pltpu.VMEM((2, cp, PAGE_ROWS, CKV_ROW), F32),
                        pltpu.VMEM((2, cp, PAGE_ROWS, KPE_ROW), F32),
                        pltpu.SemaphoreType.DMA((2, 2))],
    )
    return pl.pallas_call(
        functools.partial(_decode_kernel, pages_per_chunk=cp, n_chunks=n_chunks),
        grid_spec=grid_spec,
        out_shape=jax.ShapeDtypeStruct((nb, A_HEADS, KV_LORA), F32),
        compiler_params=_params("arbitrary"),
        name="decode_attention",
    )(page_table, qlat, qpe_tiled, ckv_new, kpe_new, cache_ckv, cache_kpe)


def _vup_kernel(ol_ref, wuv_ref, o_ref):
    for hd in range(A_HEADS):
        o_ref[:, hd * V_DIM:(hd + 1) * V_DIM] = _dot(ol_ref[hd].astype(BF16), wuv_ref[hd]).astype(BF16)


def _value_up(o_lat_heads, wuv_heads):
    nb = o_lat_heads.shape[1]
    return pl.pallas_call(
        _vup_kernel,
        in_specs=[_const_spec(o_lat_heads), _const_spec(wuv_heads)],
        out_specs=pl.BlockSpec((nb, A_HEADS * V_DIM), lambda: (0, 0)),
        out_shape=jax.ShapeDtypeStruct((nb, A_HEADS * V_DIM), BF16),
        grid=(),
        name="value_up",
    )(o_lat_heads, wuv_heads)


def _ssd_step_kernel(xbc_ref, cs_ref, z_ref, dt_ref, st_ref, cw_ref, cb_ref, dtb_ref, na_ref,
                     dsk_ref, gs_ref, y_o, st_o, y_ref):
    conv = cb_ref[...] + cw_ref[CONV_W - 1:CONV_W, :] * xbc_ref[0]
    for j in range(CONV_W - 1):
        conv = conv + cw_ref[j:j + 1, :] * cs_ref[0, j:j + 1, :]
    xbc = _silu(conv)
    dt = _softplus(dt_ref[0] + dtb_ref[...])
    dec = jnp.exp(dt * na_ref[...])
    hi, mid, lo = (part.astype(F32) for part in _split3(xbc[:, :D_INNER]))
    r8 = lax.broadcasted_iota(jnp.int32, (8, D_INNER), 0)
    x8 = jnp.where(r8 == 0, hi, jnp.where(r8 == 1, mid, jnp.where(r8 == 2, lo, 0.0))).astype(BF16)
    ones8 = (lax.broadcasted_iota(jnp.int32, (8, D_STATE), 0) < 3).astype(BF16)
    x_col = _dot_tn(x8, ones8)
    for hd in range(S_HEADS):
        g = hd // GROUP_HEADS
        hsl = slice(hd * S_HEAD_DIM, (hd + 1) * S_HEAD_DIM)
        bm = xbc[:, D_INNER + g * D_STATE:D_INNER + (g + 1) * D_STATE]
        cm = xbc[:, D_INNER + (S_GROUPS + g) * D_STATE:D_INNER + (S_GROUPS + g + 1) * D_STATE]
        new = st_ref[0, hd] * dec[:, hd:hd + 1] + x_col[hsl, :] * (bm * dt[:, hd:hd + 1])
        st_o[0, hd] = new
        cb8 = jnp.broadcast_to(cm, (8, D_STATE)).astype(BF16)
        y_ref[:, hsl] = _dot_nt(cb8, new.astype(BF16))
    y = y_ref[0:1, :] + dsk_ref[...] * xbc[:, :D_INNER]
    y = y * _silu(z_ref[0])
    for g in range(S_GROUPS):
        gsl = slice(g * GROUP_CH, (g + 1) * GROUP_CH)
        y_o[0, :, gsl] = _rms(y[:, gsl], gs_ref[:, gsl]).astype(BF16)


def _ssd_sample(xbc, conv_state, z, dt, ssm_state, conv_w, conv_b, dtb_row, na_row, dskip, g_ssd):
    nb = xbc.shape[0]
    consts = (conv_w, conv_b, dtb_row, na_row, dskip, g_ssd)
    st_spec = pl.BlockSpec((1, S_HEADS, S_HEAD_DIM, D_STATE), lambda i: (i, 0, 0, 0))
    return pl.pallas_call(
        _ssd_step_kernel,
        grid=(nb,),
        in_specs=[pl.BlockSpec((1, 1, CONV_CH), lambda i: (i, 0, 0)),
                  pl.BlockSpec((1, CONV_W - 1, CONV_CH), lambda i: (i, 0, 0)),
                  pl.BlockSpec((1, 1, D_INNER), lambda i: (i, 0, 0)),
                  pl.BlockSpec((1, 1, LANES), lambda i: (i, 0, 0)),
                  st_spec] + [_const_spec(a) for a in consts],
        out_specs=[pl.BlockSpec((1, 1, D_INNER), lambda i: (i, 0, 0)), st_spec],
        out_shape=[jax.ShapeDtypeStruct((nb, 1, D_INNER), BF16),
                   jax.ShapeDtypeStruct(ssm_state.shape, F32)],
        scratch_shapes=[pltpu.VMEM((8, D_INNER), F32)],
        compiler_params=_params("arbitrary"),
        name="ssd_sample",
    )(xbc, conv_state, z, dt, ssm_state, *consts)


def _rot_half(w):
    half = QK_ROPE // 2
    return jnp.concatenate([-w[..., half:], w[..., :half]], axis=-1)


def _pad_cols(w, start, total):
    return jnp.pad(w, ((0, 0), (start, total - start - w.shape[1])))


def _head_pad(w_nope, w_rope):
    k = w_nope.shape[0]
    pad = jnp.zeros((k, A_HEADS, HEAD_PAD - QK_NOPE - QK_ROPE), w_nope.dtype)
    return jnp.concatenate([w_nope, w_rope, pad], axis=-1).reshape(k, A_HEADS * HEAD_PAD)


def _rope_tables(pos):
    half = QK_ROPE // 2
    inv = ROPE_BASE ** (-jnp.arange(half, dtype=F32) / half)
    ang = pos.astype(F32)[:, None] * inv[None, :]
    cos, sin = jnp.cos(ang), jnp.sin(ang)
    n = pos.shape[0]
    ctab = jnp.concatenate([jnp.ones((n, QK_NOPE), F32), cos, cos,
                            jnp.zeros((n, HEAD_PAD - QK_NOPE - QK_ROPE), F32)], axis=1)
    stab = jnp.concatenate([jnp.zeros((n, QK_NOPE), F32), sin, sin,
                            jnp.zeros((n, HEAD_PAD - QK_NOPE - QK_ROPE), F32)], axis=1)
    return ctab, stab


def _pick_tile(n, target):
    t = min(n, target)
    assert n % t == 0, (n, t)
    return t


def kernel(x_prompt, x_sample, cache_ckv, cache_kpe, state_conv, state_ssm, page_table, c_prompt, c_sample, w_ada, b_ada, g_pre_mix, g_post_mix, g_pre_ffn, g_post_ffn, w_in, g_q_a, w_q_b, g_kv_a, w_uk, w_uv, w_o_attn, conv_w, conv_b, dt_bias, a_log, d_skip, g_ssd, w_o_ssd, w_out, w_router, b_router, w_gate_e, w_up_e, w_down_e, w_gate_s, w_up_s, w_down_s):
    bp, sp, d = x_prompt.shape
    nb, ds, _ = x_sample.shape
    depth = w_in.shape[0]
    assert depth == 1 and ds == 1 and d == D_MODEL
    n_pages = page_table.shape[1]
    past_len = n_pages * PAGE_SIZE
    lyr = 0

    offs = [0]
    for n in IN_SPLITS:
        offs.append(offs[-1] + n)
    win = w_in[lyr]
    w_qa, w_kva, w_kpe, w_z, w_xbc, w_dt, w_ga, w_gb = (win[:, offs[i]:offs[i + 1]] for i in range(8))
    wqb = w_q_b[lyr].reshape(Q_LORA, A_HEADS, QK_NOPE + QK_ROPE)
    wq_pad = _head_pad(wqb[..., :QK_NOPE], wqb[..., QK_NOPE:])
    wq_rot = _head_pad(jnp.zeros_like(wqb[..., :QK_NOPE]), _rot_half(wqb[..., QK_NOPE:]))
    wuk_pad = _head_pad(w_uk[lyr], jnp.zeros((KV_LORA, A_HEADS, QK_ROPE), F32))
    premix_w = [
        w_qa.astype(BF16), w_kva.astype(BF16),
        _pad_cols(w_kpe, QK_NOPE, LANES).astype(BF16),
        _pad_cols(_rot_half(w_kpe), QK_NOPE, LANES).astype(BF16),
        w_z.astype(BF16), w_xbc.astype(BF16),
        _pad_cols(w_dt, 0, LANES).astype(BF16), w_dt.T.astype(BF16),
        w_ga.astype(BF16), w_gb.astype(BF16),
        g_q_a[lyr][None, :], wq_pad.astype(BF16), wq_rot.astype(BF16),
        g_kv_a[lyr][None, :], wuk_pad.astype(BF16),
        w_uv[lyr].reshape(KV_LORA, A_HEADS * V_DIM).astype(BF16),
    ]
    merge_w = [w_o_attn[lyr].astype(BF16), w_o_ssd[lyr].astype(BF16), w_out[lyr].astype(BF16),
               g_post_mix[lyr][None, :], g_pre_ffn[lyr][None, :]]
    shared_w = [w_gate_s[lyr].astype(BF16), w_up_s[lyr].astype(BF16), w_down_s[lyr].astype(BF16)]
    w_router_t = w_router[lyr].T.astype(BF16)
    b_router_col = b_router[lyr][:, None]
    g_pre = g_pre_mix[lyr][None, :]
    g_post_ffn_row = g_post_ffn[lyr][None, :]
    cw = conv_w[lyr]
    cb = conv_b[lyr][None, :]
    neg_a = -jnp.exp(a_log[lyr].astype(F32))
    dtb_row = jnp.pad(dt_bias[lyr], (0, LANES - S_HEADS))[None, :]
    na_row = jnp.pad(neg_a, (0, LANES - S_HEADS))[None, :]
    dtb_col = dt_bias[lyr][:, None]
    na_col = neg_a[:, None]
    dskip_row = jnp.repeat(d_skip[lyr].astype(F32), S_HEAD_DIM)[None, :]
    gssd_row = g_ssd[lyr][None, :]

    n_mod_rows = bp + nb
    pad_rows = (-n_mod_rows) % 16
    c_all = jnp.concatenate([c_prompt, c_sample, jnp.zeros((pad_rows, d), F32)], axis=0)
    mod = _adaln(c_all, w_ada[lyr], b_ada[lyr][None, :])
    mods = [mod[:, i * d:(i + 1) * d] for i in range(6)]
    mods_p = [m[:bp][:, None, :] for m in mods]
    mods_s = [m[bp:bp + nb][None, :, :] for m in mods]

    ctab_p, stab_p = _rope_tables(jnp.arange(sp))
    tm = _pick_tile(sp, 256)
    (q_p, k_p, v_p, ckv_p, kpe_pad_p, z_p, xbc_p, dt_p, dtt_p, ga_p, gb_p) = _premix(
        x_prompt, mods_p[0], mods_p[1], g_pre, ctab_p, stab_p, premix_w, tm)
    o_attn_p = _flash_attention(q_p, k_p, v_p, _pick_tile(sp, 512), 2)
    y_ssd_p, ssm_t_p = _ssd_prompt(xbc_p, z_p, dt_p, dtt_p, cw, cb, dtb_row, dtb_col, na_row, na_col,
                                   dskip_row, gssd_row)
    x1_p, h2_p = _merge(x_prompt, o_attn_p, y_ssd_p, ga_p, gb_p, mods_p[2], mods_p[3], mods_p[4],
                        merge_w, tm)
    comb_p = _router(h2_p, w_router_t, b_router_col, tm)
    y_prompt = _moe(h2_p, comb_p, x1_p, mods_p[5], w_gate_e[lyr], w_up_e[lyr], w_down_e[lyr],
                    shared_w, g_post_ffn_row, _pick_tile(sp, 1024))

    xs = x_sample.reshape(1, nb, d)
    ctab_s, stab_s = _rope_tables(past_len + jnp.arange(ds))
    (q_s, _, _, ckv_s, kpe_pad_s, z_s, xbc_s, dt_s, _, ga_s, gb_s) = _premix(
        xs, mods_s[0], mods_s[1], g_pre, ctab_s, stab_s, premix_w, nb)
    kpe_s = kpe_pad_s[0, :, QK_NOPE:QK_NOPE + QK_ROPE]
    wuk_heads = jnp.transpose(w_uk[lyr], (1, 0, 2)).astype(BF16)
    wuv_heads = jnp.transpose(w_uv[lyr], (1, 0, 2)).astype(BF16)
    qlat = jnp.transpose(_qlat(q_s[0], wuk_heads), (1, 0, 2))
    qpe = q_s[0].reshape(nb, A_HEADS, HEAD_PAD)[:, :, QK_NOPE:QK_NOPE + QK_ROPE]
    n_pool = cache_ckv.shape[1]
    o_lat = _decode_attention(page_table, qlat, jnp.tile(qpe, (1, 1, KEY_GROUP)),
                              ckv_s.reshape(nb, 1, KV_LORA), kpe_s.reshape(nb, 1, QK_ROPE),
                              cache_ckv.reshape(n_pool, PAGE_ROWS, CKV_ROW),
                              cache_kpe.reshape(n_pool, PAGE_ROWS, KPE_ROW))
    o_attn_s = _value_up(jnp.transpose(o_lat, (1, 0, 2)), wuv_heads)[None]
    y_ssd_s, ssm_s = _ssd_sample(xbc_s.reshape(nb, 1, CONV_CH), state_conv[lyr],
                                 z_s.reshape(nb, 1, D_INNER), dt_s.reshape(nb, 1, LANES),
                                 state_ssm.reshape(state_ssm.shape[1:]), cw, cb, dtb_row, na_row,
                                 dskip_row, gssd_row)
    x1_s, h2_s = _merge(xs, o_attn_s, y_ssd_s.reshape(1, nb, D_INNER), ga_s, gb_s,
                        mods_s[2], mods_s[3], mods_s[4], merge_w, nb)
    comb_s = _router(h2_s, w_router_t, b_router_col, nb)
    y_s = _moe(h2_s, comb_s, x1_s, mods_s[5], w_gate_e[lyr], w_up_e[lyr], w_down_e[lyr],
               shared_w, g_post_ffn_row, nb)

    kpe_p = kpe_pad_p[:, :, QK_NOPE:QK_NOPE + QK_ROPE]
    conv_p = xbc_p[:, sp - (CONV_W - 1):, :]
    ssm_p = jnp.transpose(ssm_t_p.reshape(bp, D_STATE, S_HEADS, S_HEAD_DIM), (0, 2, 3, 1))
    conv_s = jnp.concatenate([state_conv[lyr][:, 1:, :], xbc_s.reshape(nb, 1, CONV_CH)], axis=1)
    return (y_prompt, y_s.reshape(nb, ds, d),
            ckv_p[None], kpe_p[None], conv_p[None], ssm_p[None].astype(x_prompt.dtype),
            ckv_s.reshape(1, nb, ds, KV_LORA), kpe_s.reshape(1, nb, ds, QK_ROPE),
            conv_s[None], ssm_s[None].astype(x_sample.dtype))
```

```python
import functools

import jax
import jax.numpy as jnp
from jax import lax
from jax.experimental import pallas as pl
from jax.experimental.pallas import tpu as pltpu

F32 = jnp.float32
BF16 = jnp.bfloat16

D_MODEL = 1024
PAGE_SIZE = 128
A_HEADS = 8
QK_NOPE = 64
QK_ROPE = 32
V_DIM = 64
Q_LORA = 384
KV_LORA = 256
ROPE_BASE = 10000.0
ATTN_SCALE = (QK_NOPE + QK_ROPE) ** -0.5
S_HEADS = 16
S_HEAD_DIM = 64
D_INNER = S_HEADS * S_HEAD_DIM
S_GROUPS = 2
D_STATE = 128
CONV_W = 4
CONV_CH = D_INNER + 2 * S_GROUPS * D_STATE
CHUNK = 128
N_EXPERTS = 64
TOP_K = 6
N_EXP_GROUPS = 8
TOPK_GROUPS = 4
F_EXPERT = 256
ROUTE_SCALE = 2.5
EPS = 1e-6
IN_SPLITS = (Q_LORA, KV_LORA, QK_ROPE, D_INNER, CONV_CH, S_HEADS, D_MODEL, D_MODEL)

LANES = 128
HEAD_PAD = LANES
GROUP_HEADS = S_HEADS // S_GROUPS
GROUP_CH = D_INNER // S_GROUPS
EXP_PER_GROUP = N_EXPERTS // N_EXP_GROUPS
VMEM_LIMIT_BYTES = 56 * 1024 * 1024

NEG_INF = float("-inf")
LOG2_E = 1.4426950408889634


def _params(*semantics):
    return pltpu.CompilerParams(dimension_semantics=semantics, vmem_limit_bytes=VMEM_LIMIT_BYTES)


def _const_spec(arr):
    nd = arr.ndim
    return pl.BlockSpec(arr.shape, lambda *_: (0,) * nd)


def _dot(a, b):
    return jnp.dot(a, b, preferred_element_type=F32)


def _dot_nt(a, b):
    return lax.dot_general(a, b, (((1,), (1,)), ((), ())), preferred_element_type=F32)


def _dot_tn(a, b):
    return lax.dot_general(a, b, (((0,), (0,)), ((), ())), preferred_element_type=F32)


def _rms(x, g):
    return x * lax.rsqrt(jnp.mean(x * x, axis=-1, keepdims=True) + EPS) * g


def _silu(x):
    return x * (1.0 / (1.0 + jnp.exp(-x)))


def _sigmoid(x):
    return 1.0 / (1.0 + jnp.exp(-x))


def _softplus(x):
    return jnp.maximum(x, 0.0) + jnp.log(1.0 + jnp.exp(-jnp.abs(x)))


def _split3(x):
    hi = x.astype(BF16)
    r1 = x - hi.astype(F32)
    mid = r1.astype(BF16)
    lo = (r1 - mid.astype(F32)).astype(BF16)
    return hi, mid, lo


def _adaln_kernel(c_ref, w_ref, b_ref, o_ref):
    c = _silu(c_ref[...]).astype(BF16)
    o_ref[...] = _dot(c, w_ref[...].astype(BF16)) + b_ref[...]


def _adaln(c_all, w_ada, b_ada):
    rows, d = c_all.shape
    n = w_ada.shape[1]
    tn = 512
    return pl.pallas_call(
        _adaln_kernel,
        grid=(n // tn,),
        in_specs=[pl.BlockSpec((rows, d), lambda j: (0, 0)),
                  pl.BlockSpec((d, tn), lambda j: (0, j)),
                  pl.BlockSpec((1, tn), lambda j: (0, j))],
        out_specs=pl.BlockSpec((rows, tn), lambda j: (0, j)),
        out_shape=jax.ShapeDtypeStruct((rows, n), F32),
        compiler_params=_params("arbitrary"),
        name="adaln",
    )(c_all, w_ada, b_ada)


def _premix_kernel(x_ref, sh_ref, sc_ref, g_ref, ct_ref, st_ref,
                   wqa, wkva, wkpe, wkper, wz, wxbc, wdt, wdtt, wga, wgb,
                   gqa, wq, wqr, gkv, wuk, wuv,
                   q_o, k_o, v_o, ckv_o, kpe_o, z_o, xbc_o, dt_o, dtt_o, ga_o, gb_o):
    x = x_ref[0]
    h = _rms(x, g_ref[...]) * (1.0 + sc_ref[0]) + sh_ref[0]
    hb = h.astype(BF16)
    ct = ct_ref[...]
    st = st_ref[...]

    qn = _rms(_dot(hb, wqa[...]), gqa[...]).astype(BF16)
    q = _dot(qn, wq[...])
    qr = _dot(qn, wqr[...])
    for hd in range(A_HEADS):
        sl = slice(hd * HEAD_PAD, (hd + 1) * HEAD_PAD)
        q_o[0, :, sl] = (q[:, sl] * ct + qr[:, sl] * st).astype(BF16)

    ckv = _rms(_dot(hb, wkva[...]), gkv[...])
    ckv_o[0] = ckv
    cb = ckv.astype(BF16)
    kpe = _dot(hb, wkpe[...]) * ct + _dot(hb, wkper[...]) * st
    kpe_o[0] = kpe
    kn = _dot(cb, wuk[...])
    for hd in range(A_HEADS):
        sl = slice(hd * HEAD_PAD, (hd + 1) * HEAD_PAD)
        k_o[0, :, sl] = (kn[:, sl] + kpe).astype(BF16)
    v_o[0] = _dot(cb, wuv[...]).astype(BF16)

    z_o[0] = _dot(hb, wz[...])
    xbc_o[0] = _dot(hb, wxbc[...])
    dt_o[0] = _dot(hb, wdt[...])
    dtt_o[0] = _dot_nt(wdtt[...], hb)
    ga_o[0] = _dot(hb, wga[...])
    gb_o[0] = _dot(hb, wgb[...])


def _premix(x, sh, sc, g, ctab, stab, weights, tm):
    b, s, d = x.shape
    per_row_mod = sh.shape[1] != 1
    per_row_tab = ctab.shape[0] != 1
    mod_spec = (pl.BlockSpec((1, tm, d), lambda i, j: (i, j, 0)) if per_row_mod
                else pl.BlockSpec((1, 1, d), lambda i, j: (i, 0, 0)))
    tab_spec = (pl.BlockSpec((tm, LANES), lambda i, j: (j, 0)) if per_row_tab
                else pl.BlockSpec((1, LANES), lambda i, j: (0, 0)))

    def tok(n):
        return pl.BlockSpec((1, tm, n), lambda i, j: (i, j, 0))

    out_cols = [(A_HEADS * HEAD_PAD, BF16), (A_HEADS * HEAD_PAD, BF16), (A_HEADS * V_DIM, BF16),
                (KV_LORA, F32), (LANES, F32), (D_INNER, F32), (CONV_CH, F32), (LANES, F32)]
    out_shape = [jax.ShapeDtypeStruct((b, s, n), dt) for n, dt in out_cols]
    out_specs = [tok(n) for n, _ in out_cols]
    out_shape.append(jax.ShapeDtypeStruct((b, S_HEADS, s), F32))
    out_specs.append(pl.BlockSpec((1, S_HEADS, tm), lambda i, j: (i, 0, j)))
    out_shape += [jax.ShapeDtypeStruct((b, s, D_MODEL), F32)] * 2
    out_specs += [tok(D_MODEL)] * 2
    order = [out_shape[i] for i in (0, 1, 2, 3, 4, 5, 6, 7, 8, 9, 10)]
    return pl.pallas_call(
        _premix_kernel,
        grid=(b, s // tm),
        in_specs=[tok(d), mod_spec, mod_spec, _const_spec(g), tab_spec, tab_spec]
        + [_const_spec(w) for w in weights],
        out_specs=out_specs,
        out_shape=order,
        compiler_params=_params("arbitrary", "arbitrary"),
        name="premix",
    )(x, sh, sc, g, ctab, stab, *weights)


def _flash_kernel(q_ref, k_ref, v_ref, o_ref, m_ref, l_ref, acc_ref, *, tq, heads):
    qi = pl.program_id(2)
    m_ref[...] = jnp.full(m_ref.shape, NEG_INF, F32)
    l_ref[...] = jnp.zeros(l_ref.shape, F32)
    acc_ref[...] = jnp.zeros(acc_ref.shape, F32)
    row = lax.broadcasted_iota(jnp.int32, (tq, tq), 0)
    col = lax.broadcasted_iota(jnp.int32, (tq, tq), 1)
    c2 = ATTN_SCALE * LOG2_E

    def tile(kj, on_diagonal):
        koff = pl.multiple_of(kj * tq, tq)
        for hh in range(heads):
            sl = slice(hh * HEAD_PAD, (hh + 1) * HEAD_PAD)
            vsl = slice((hh // 2) * 2 * V_DIM, (hh // 2 + 1) * 2 * V_DIM)
            s = _dot_nt(q_ref[0, :, sl], k_ref[0, pl.ds(koff, tq), sl])
            if on_diagonal:
                s = jnp.where(col <= row, s, NEG_INF)
            m_prev = m_ref[hh]
            m_new = jnp.maximum(m_prev, jnp.max(s, axis=-1, keepdims=True))
            alpha = jnp.exp2((m_prev - m_new) * c2)
            p_parts = [jnp.exp2((s[:, j * LANES:(j + 1) * LANES] - m_new) * c2)
                       for j in range(tq // LANES)]
            l_ref[hh] = alpha * l_ref[hh] + sum(p_parts)
            p = jnp.concatenate(p_parts, axis=1).astype(BF16)
            acc_ref[hh] = alpha * acc_ref[hh] + _dot(p, v_ref[0, pl.ds(koff, tq), vsl])
            m_ref[hh] = m_new

    def body(kj, carry):
        tile(kj, False)
        return carry

    lax.fori_loop(0, qi, body, 0)
    tile(qi, True)
    lane = lax.broadcasted_iota(jnp.int32, (tq, 2 * V_DIM), 1)
    for pair in range(heads // 2):
        o0 = acc_ref[2 * pair] / jnp.sum(l_ref[2 * pair], axis=-1, keepdims=True)
        o1 = acc_ref[2 * pair + 1] / jnp.sum(l_ref[2 * pair + 1], axis=-1, keepdims=True)
        o_ref[0, :, pair * 2 * V_DIM:(pair + 1) * 2 * V_DIM] = jnp.where(lane < V_DIM, o0, o1).astype(BF16)


def _flash_attention(q, k, v, tq, heads):
    b, s, _ = q.shape
    return pl.pallas_call(
        functools.partial(_flash_kernel, tq=tq, heads=heads),
        grid=(b, A_HEADS // heads, s // tq),
        in_specs=[pl.BlockSpec((1, tq, heads * HEAD_PAD), lambda bi, hg, qi: (bi, qi, hg)),
                  pl.BlockSpec((1, s, heads * HEAD_PAD), lambda bi, hg, qi: (bi, 0, hg)),
                  pl.BlockSpec((1, s, heads * V_DIM), lambda bi, hg, qi: (bi, 0, hg))],
        out_specs=pl.BlockSpec((1, tq, heads * V_DIM), lambda bi, hg, qi: (bi, qi, hg)),
        out_shape=jax.ShapeDtypeStruct((b, s, A_HEADS * V_DIM), BF16),
        scratch_shapes=[pltpu.VMEM((heads, tq, LANES), F32), pltpu.VMEM((heads, tq, LANES), F32),
                        pltpu.VMEM((heads, tq, 2 * V_DIM), F32)],
        compiler_params=_params("arbitrary", "arbitrary", "arbitrary"),
        name="flash_attention",
    )(q, k, v)


def _ssd_chunk_kernel(xbc_ref, z_ref, dt_ref, dtt_ref, cw_ref, cb_ref, dtb_ref, dtbc_ref,
                      na_ref, nac_ref, dsk_ref, gs_ref, y_o, st_o, xc_ref, state_ref, y_ref):
    c = pl.program_id(1)
    nc = pl.num_programs(1)

    @pl.when(c == 0)
    def _():
        xc_ref[0:8, :] = jnp.zeros((8, CONV_CH), F32)
        state_ref[...] = jnp.zeros(state_ref.shape, F32)

    xc_ref[8:8 + CHUNK, :] = xbc_ref[0]
    conv = cb_ref[...] + cw_ref[CONV_W - 1:CONV_W, :] * xc_ref[8:8 + CHUNK, :]
    for j in range(1, CONV_W):
        conv = conv + cw_ref[CONV_W - 1 - j:CONV_W - j, :] * xc_ref[8 - j:8 - j + CHUNK, :]
    xc_ref[0:8, :] = xc_ref[CHUNK:CHUNK + 8, :]
    xbc = _silu(conv)

    dt_c = _softplus(dt_ref[0] + dtb_ref[...])
    a_c = dt_c * na_ref[...]
    dt_r = _softplus(dtt_ref[0] + dtbc_ref[...])
    a_r = dt_r * nac_ref[...]
    li = lax.broadcasted_iota(jnp.int32, (CHUNK, CHUNK), 0)
    si = lax.broadcasted_iota(jnp.int32, (CHUNK, CHUNK), 1)
    lower = (si <= li)
    tri = lower.astype(BF16)
    acum_c = sum(_dot(tri, part) for part in _split3(a_c))
    acum_r = sum(_dot_nt(part, tri) for part in _split3(a_r))
    a_last = acum_c[CHUNK - 1:CHUNK, :]
    dec_end = jnp.exp(a_last - acum_c)
    exp_ac = jnp.exp(acum_c)
    dec_all = jnp.exp(a_last)

    for g in range(S_GROUPS):
        bm = xbc[:, D_INNER + g * D_STATE:D_INNER + (g + 1) * D_STATE]
        cm = xbc[:, D_INNER + (S_GROUPS + g) * D_STATE:D_INNER + (S_GROUPS + g + 1) * D_STATE]
        bmb = bm.astype(BF16)
        cmb = cm.astype(BF16)
        gmat = _dot_nt(cmb, bmb)
        gsl = slice(g * GROUP_CH, (g + 1) * GROUP_CH)
        y_off = _dot(cmb, state_ref[:, gsl].astype(BF16))
        xde_parts = []
        for hh in range(GROUP_HEADS):
            hd = g * GROUP_HEADS + hh
            hsl = slice(hd * S_HEAD_DIM, (hd + 1) * S_HEAD_DIM)
            xs = xbc[:, hsl]
            xdt = xs * dt_c[:, hd:hd + 1]
            seg = acum_c[:, hd:hd + 1] - acum_r[hd:hd + 1, :]
            lmat = jnp.exp(jnp.where(lower, seg, NEG_INF))
            y_diag = _dot((gmat * lmat).astype(BF16), xdt.astype(BF16))
            yo = y_off[:, hh * S_HEAD_DIM:(hh + 1) * S_HEAD_DIM] * exp_ac[:, hd:hd + 1]
            y_ref[:, hsl] = y_diag + yo
            xde_parts.append((xdt * dec_end[:, hd:hd + 1]).astype(BF16))
        xde = jnp.concatenate(xde_parts, axis=1)
        new_states = _dot_tn(bmb, xde)
        for hh in range(GROUP_HEADS):
            hd = g * GROUP_HEADS + hh
            hsl = slice(hd * S_HEAD_DIM, (hd + 1) * S_HEAD_DIM)
            lsl = slice(hh * S_HEAD_DIM, (hh + 1) * S_HEAD_DIM)
            state_ref[:, hsl] = state_ref[:, hsl] * dec_all[:, hd:hd + 1] + new_states[:, lsl]

    y = y_ref[...] + dsk_ref[...] * xbc[:, :D_INNER]
    y = y * _silu(z_ref[0])
    for g in range(S_GROUPS):
        gsl = slice(g * GROUP_CH, (g + 1) * GROUP_CH)
        y_o[0, :, gsl] = _rms(y[:, gsl], gs_ref[:, gsl]).astype(BF16)

    @pl.when(c == nc - 1)
    def _():
        st_o[0] = state_ref[...]


def _ssd_prompt(xbc, z, dt, dtt, conv_w, conv_b, dtb_row, dtb_col, na_row, na_col, dskip, g_ssd):
    b, s, _ = xbc.shape
    nc = s // CHUNK
    consts = (conv_w, conv_b, dtb_row, dtb_col, na_row, na_col, dskip, g_ssd)
    return pl.pallas_call(
        _ssd_chunk_kernel,
        grid=(b, nc),
        in_specs=[pl.BlockSpec((1, CHUNK, CONV_CH), lambda i, j: (i, j, 0)),
                  pl.BlockSpec((1, CHUNK, D_INNER), lambda i, j: (i, j, 0)),
                  pl.BlockSpec((1, CHUNK, LANES), lambda i, j: (i, j, 0)),
                  pl.BlockSpec((1, S_HEADS, CHUNK), lambda i, j: (i, 0, j))]
        + [_const_spec(a) for a in consts],
        out_specs=[pl.BlockSpec((1, CHUNK, D_INNER), lambda i, j: (i, j, 0)),
                   pl.BlockSpec((1, D_STATE, D_INNER), lambda i, j: (i, 0, 0))],
        out_shape=[jax.ShapeDtypeStruct((b, s, D_INNER), BF16),
                   jax.ShapeDtypeStruct((b, D_STATE, D_INNER), F32)],
        scratch_shapes=[pltpu.VMEM((CHUNK + 8, CONV_CH), F32),
                        pltpu.VMEM((D_STATE, D_INNER), F32),
                        pltpu.VMEM((CHUNK, D_INNER), F32)],
        compiler_params=_params("arbitrary", "arbitrary"),
        name="ssd_prompt",
    )(xbc, z, dt, dtt, *consts)


def _merge_kernel(x_ref, oa_ref, ys_ref, ga_ref, gb_ref, g1_ref, sh2_ref, sc2_ref,
                  woa, wos, wout, gpost, gpre, x1_o, h2_o):
    o_attn = _dot(oa_ref[0], woa[...])
    o_ssd = _dot(ys_ref[0], wos[...])
    merged = _sigmoid(ga_ref[0]) * o_attn + _sigmoid(gb_ref[0]) * o_ssd
    m = _dot(merged.astype(BF16), wout[...])
    x1 = x_ref[0] + g1_ref[0] * _rms(m, gpost[...])
    x1_o[0] = x1
    h2_o[0] = (_rms(x1, gpre[...]) * (1.0 + sc2_ref[0]) + sh2_ref[0]).astype(BF16)


def _merge(x, o_attn, y_ssd, ga, gb, g1, sh2, sc2, weights, tm):
    b, s, d = x.shape
    per_row_mod = g1.shape[1] != 1
    mod_spec = (pl.BlockSpec((1, tm, d), lambda i, j: (i, j, 0)) if per_row_mod
                else pl.BlockSpec((1, 1, d), lambda i, j: (i, 0, 0)))

    def tok(n):
        return pl.BlockSpec((1, tm, n), lambda i, j: (i, j, 0))

    return pl.pallas_call(
        _merge_kernel,
        grid=(b, s // tm),
        in_specs=[tok(d), tok(A_HEADS * V_DIM), tok(D_INNER), tok(d), tok(d),
                  mod_spec, mod_spec, mod_spec] + [_const_spec(w) for w in weights],
        out_specs=[tok(d), tok(d)],
        out_shape=[jax.ShapeDtypeStruct((b, s, d), F32), jax.ShapeDtypeStruct((b, s, d), BF16)],
        compiler_params=_params("arbitrary", "arbitrary"),
        name="merge",
    )(x, o_attn, y_ssd, ga, gb, g1, sh2, sc2, *weights)


def _first_max(vals, idx, big):
    m = vals[0]
    for v in vals[1:]:
        m = jnp.maximum(m, v)
    m = jnp.max(m, axis=0, keepdims=True)
    cand = [jnp.where(v == m, i, big) for v, i in zip(vals, idx)]
    a = cand[0]
    for cnd in cand[1:]:
        a = jnp.minimum(a, cnd)
    a = jnp.min(a, axis=0, keepdims=True)
    return m, a


def _router_kernel(h_ref, wt_ref, b_ref, o_ref, *, tm):
    logits = _dot_nt(wt_ref[...], h_ref[0])
    scores = _sigmoid(logits)
    biased = scores + b_ref[...]
    sub = lax.broadcasted_iota(jnp.int32, (EXP_PER_GROUP, tm), 0)
    slabs = [biased[g * EXP_PER_GROUP:(g + 1) * EXP_PER_GROUP, :] for g in range(N_EXP_GROUPS)]
    big = jnp.int32(N_EXPERTS)

    gscore = []
    for g in range(N_EXP_GROUPS):
        m1, a1 = _first_max([slabs[g]], [sub], big)
        rest = jnp.where(sub == a1, NEG_INF, slabs[g])
        m2 = jnp.max(rest, axis=0, keepdims=True)
        gscore.append(m1 + m2)
    gs = jnp.full((N_EXP_GROUPS, tm), NEG_INF, F32)
    for g in range(N_EXP_GROUPS):
        gs = jnp.where(sub == g, gscore[g], gs)
    gsel = jnp.zeros((N_EXP_GROUPS, tm), F32)
    for _ in range(TOPK_GROUPS):
        _, a = _first_max([gs], [sub], big)
        hit = sub == a
        gsel = jnp.where(hit, 1.0, gsel)
        gs = jnp.where(hit, NEG_INF, gs)

    masked = [jnp.where(gsel[g:g + 1, :] > 0.5, slabs[g], NEG_INF) for g in range(N_EXP_GROUPS)]
    flat = [sub + g * EXP_PER_GROUP for g in range(N_EXP_GROUPS)]
    chosen = [jnp.zeros((EXP_PER_GROUP, tm), jnp.bool_) for _ in range(N_EXP_GROUPS)]
    for _ in range(TOP_K):
        _, a = _first_max(masked, flat, big)
        for g in range(N_EXP_GROUPS):
            hit = flat[g] == a
            chosen[g] = jnp.logical_or(chosen[g], hit)
            masked[g] = jnp.where(hit, NEG_INF, masked[g])
    w = [jnp.where(chosen[g], scores[g * EXP_PER_GROUP:(g + 1) * EXP_PER_GROUP, :], 0.0)
         for g in range(N_EXP_GROUPS)]
    tot = w[0]
    for part in w[1:]:
        tot = tot + part
    tot = jnp.sum(tot, axis=0, keepdims=True)
    comb_t = jnp.concatenate([part / tot * ROUTE_SCALE for part in w]
                             + [jnp.zeros((LANES - N_EXPERTS, tm), F32)], axis=0)
    o_ref[0] = comb_t.T


def _router(h2, w_router_t, b_col, tm):
    b, s, d = h2.shape
    return pl.pallas_call(
        functools.partial(_router_kernel, tm=tm),
        grid=(b, s // tm),
        in_specs=[pl.BlockSpec((1, tm, d), lambda i, j: (i, j, 0)),
                  _const_spec(w_router_t), _const_spec(b_col)],
        out_specs=pl.BlockSpec((1, tm, LANES), lambda i, j: (i, j, 0)),
        out_shape=jax.ShapeDtypeStruct((b, s, LANES), F32),
        compiler_params=_params("arbitrary", "arbitrary"),
        name="router",
    )(h2, w_router_t, b_col)


def _moe_kernel(h_ref, c_ref, x1_ref, g2_ref, wg_ref, wu_ref, wd_ref, wgs, wus, wds, gpost,
                y_o, acc_ref, *, tm):
    e = pl.program_id(2)
    ne = pl.num_programs(2)
    hb = h_ref[0]

    @pl.when(e == 0)
    def _():
        hid = _silu(_dot(hb, wgs[...])) * _dot(hb, wus[...])
        acc_ref[...] = _dot(hid.astype(BF16), wds[...])

    lane = lax.broadcasted_iota(jnp.int32, (tm, LANES), 1)
    ce = jnp.sum(jnp.where(lane == e, c_ref[0], 0.0), axis=-1, keepdims=True)
    gte = _dot(hb, wg_ref[0].astype(BF16))
    upe = _dot(hb, wu_ref[0].astype(BF16))
    act = (_silu(gte) * upe * ce).astype(BF16)
    acc_ref[...] += _dot(act, wd_ref[0].astype(BF16))

    @pl.when(e == ne - 1)
    def _():
        y_o[0] = x1_ref[0] + g2_ref[0] * _rms(acc_ref[...], gpost[...])


def _moe(h2, comb, x1, g2, w_gate_e, w_up_e, w_down_e, shared, gpost, tm):
    b, s, d = h2.shape
    per_row_mod = g2.shape[1] != 1
    mod_spec = (pl.BlockSpec((1, tm, d), lambda i, j, e: (i, j, 0)) if per_row_mod
                else pl.BlockSpec((1, 1, d), lambda i, j, e: (i, 0, 0)))

    def tok(n):
        return pl.BlockSpec((1, tm, n), lambda i, j, e: (i, j, 0))

    return pl.pallas_call(
        functools.partial(_moe_kernel, tm=tm),
        grid=(b, s // tm, N_EXPERTS),
        in_specs=[tok(d), tok(LANES), tok(d), mod_spec,
                  pl.BlockSpec((1, d, F_EXPERT), lambda i, j, e: (e, 0, 0)),
                  pl.BlockSpec((1, d, F_EXPERT), lambda i, j, e: (e, 0, 0)),
                  pl.BlockSpec((1, F_EXPERT, d), lambda i, j, e: (e, 0, 0))]
        + [_const_spec(w) for w in shared] + [_const_spec(gpost)],
        out_specs=tok(d),
        out_shape=jax.ShapeDtypeStruct((b, s, d), F32),
        scratch_shapes=[pltpu.VMEM((tm, d), F32)],
        compiler_params=_params("arbitrary", "arbitrary", "arbitrary"),
        name="moe",
    )(h2, comb, x1, g2, w_gate_e, w_up_e, w_down_e, *shared, gpost)


def _qlat_kernel(q_ref, wuk_ref, o_ref):
    for hd in range(A_HEADS):
        qn = q_ref[:, hd * HEAD_PAD:hd * HEAD_PAD + QK_NOPE]
        o_ref[hd] = _dot_nt(qn, wuk_ref[hd]).astype(BF16)


def _qlat(q, wuk_heads):
    nb = q.shape[0]
    return pl.pallas_call(
        _qlat_kernel,
        in_specs=[_const_spec(q), _const_spec(wuk_heads)],
        out_specs=pl.BlockSpec((A_HEADS, nb, KV_LORA), lambda: (0, 0, 0)),
        out_shape=jax.ShapeDtypeStruct((A_HEADS, nb, KV_LORA), BF16),
        grid=(),
        name="q_latent",
    )(q, wuk_heads)


def _decode_kernel(pt_ref, ql_ref, qp_ref, cn_ref, kn_ref, ckv_hbm, kpet_hbm, o_ref,
                   cbuf, kbuf, sem, *, pages_per_chunk, n_chunks):
    b = pl.program_id(0)
    nb = pl.num_programs(0)
    cp = pages_per_chunk
    halves = 2 if cp % 2 == 0 else 1
    hp = cp // halves

    def copies(bb, chunk, slot):
        out = []
        for p in range(cp):
            page = pt_ref[bb, chunk * cp + p]
            out.append(pltpu.make_async_copy(ckv_hbm.at[page], cbuf.at[slot, p], sem.at[0, slot]))
            out.append(pltpu.make_async_copy(kpet_hbm.at[page], kbuf.at[slot, p], sem.at[1, slot]))
        return out

    @pl.when(b == 0)
    def _():
        for cpy in copies(0, 0, 0):
            cpy.start()

    ql = ql_ref[0]
    qp = qp_ref[0]
    c_new = cn_ref[0]
    k_new = kn_ref[0]
    s_new = (jnp.sum(ql.astype(F32) * c_new, axis=-1, keepdims=True)
             + jnp.sum(qp.astype(F32) * k_new, axis=-1, keepdims=True)) * ATTN_SCALE
    m0 = s_new
    l0 = jnp.ones((A_HEADS, 1), F32)
    acc0 = jnp.broadcast_to(c_new, (A_HEADS, KV_LORA))

    def body(chunk, carry):
        m_prev, l_prev, acc = carry
        slot = lax.rem(b * n_chunks + chunk, 2)

        @pl.when(chunk + 1 < n_chunks)
        def _():
            for cpy in copies(b, chunk + 1, 1 - slot):
                cpy.start()

        @pl.when(jnp.logical_and(chunk + 1 == n_chunks, b + 1 < nb))
        def _():
            for cpy in copies(b + 1, 0, 1 - slot):
                cpy.start()

        for cpy in copies(b, chunk, slot):
            cpy.wait()
        for hf in range(halves):
            kc = cbuf[slot, hf * hp:(hf + 1) * hp].reshape(hp * PAGE_SIZE, KV_LORA).astype(BF16)
            kt = jnp.concatenate([kbuf[slot, hf * hp + p] for p in range(hp)], axis=1).astype(BF16)
            s = (_dot_nt(ql, kc) + _dot(qp, kt)) * ATTN_SCALE
            m_new = jnp.maximum(m_prev, jnp.max(s, axis=-1, keepdims=True))
            alpha = jnp.exp(m_prev - m_new)
            p = jnp.exp(s - m_new)
            l_prev = alpha * l_prev + jnp.sum(p, axis=-1, keepdims=True)
            acc = alpha * acc + _dot(p.astype(BF16), kc)
            m_prev = m_new
        return m_prev, l_prev, acc

    _, l_fin, acc = lax.fori_loop(0, n_chunks, body, (m0, l0, acc0))
    o_ref[0] = acc / l_fin


def _decode_attention(page_table, qlat, qpe, ckv_new, kpe_new, cache_ckv, cache_kpe_t):
    nb, n_pages = page_table.shape
    cp = min(16, n_pages)
    n_chunks = n_pages // cp
    grid_spec = pltpu.PrefetchScalarGridSpec(
        num_scalar_prefetch=1,
        grid=(nb,),
        in_specs=[pl.BlockSpec((1, A_HEADS, KV_LORA), lambda i, pt: (i, 0, 0)),
                  pl.BlockSpec((1, A_HEADS, QK_ROPE), lambda i, pt: (i, 0, 0)),
                  pl.BlockSpec((1, 1, KV_LORA), lambda i, pt: (i, 0, 0)),
                  pl.BlockSpec((1, 1, QK_ROPE), lambda i, pt: (i, 0, 0)),
                  pl.BlockSpec(memory_space=pl.ANY),
                  pl.BlockSpec(memory_space=pl.ANY)],
        out_specs=pl.BlockSpec((1, A_HEADS, KV_LORA), lambda i, pt: (i, 0, 0)),
        scratch_shapes=[pltpu.VMEM((2, cp, PAGE_SIZE, KV_LORA), F32),
                        pltpu.VMEM((2, cp, QK_ROPE, PAGE_SIZE), F32),
                        pltpu.SemaphoreType.DMA((2, 2))],
    )
    return pl.pallas_call(
        functools.partial(_decode_kernel, pages_per_chunk=cp, n_chunks=n_chunks),
        grid_spec=grid_spec,
        out_shape=jax.ShapeDtypeStruct((nb, A_HEADS, KV_LORA), F32),
        compiler_params=_params("arbitrary"),
        name="decode_attention",
    )(page_table, qlat, qpe, ckv_new, kpe_new, cache_ckv, cache_kpe_t)


def _vup_kernel(ol_ref, wuv_ref, o_ref):
    for hd in range(A_HEADS):
        o_ref[:, hd * V_DIM:(hd + 1) * V_DIM] = _dot(ol_ref[hd].astype(BF16), wuv_ref[hd]).astype(BF16)


def _value_up(o_lat_heads, wuv_heads):
    nb = o_lat_heads.shape[1]
    return pl.pallas_call(
        _vup_kernel,
        in_specs=[_const_spec(o_lat_heads), _const_spec(wuv_heads)],
        out_specs=pl.BlockSpec((nb, A_HEADS * V_DIM), lambda: (0, 0)),
        out_shape=jax.ShapeDtypeStruct((nb, A_HEADS * V_DIM), BF16),
        grid=(),
        name="value_up",
    )(o_lat_heads, wuv_heads)


def _ssd_step_kernel(xbc_ref, cs_ref, z_ref, dt_ref, st_ref, cw_ref, cb_ref, dtb_ref, na_ref,
                     dsk_ref, gs_ref, y_o, st_o, y_ref):
    conv = cb_ref[...] + cw_ref[CONV_W - 1:CONV_W, :] * xbc_ref[0]
    for j in range(CONV_W - 1):
        conv = conv + cw_ref[j:j + 1, :] * cs_ref[0, j:j + 1, :]
    xbc = _silu(conv)
    dt = _softplus(dt_ref[0] + dtb_ref[...])
    dec = jnp.exp(dt * na_ref[...])
    hi, mid, lo = (part.astype(F32) for part in _split3(xbc[:, :D_INNER]))
    r8 = lax.broadcasted_iota(jnp.int32, (8, D_INNER), 0)
    x8 = jnp.where(r8 == 0, hi, jnp.where(r8 == 1, mid, jnp.where(r8 == 2, lo, 0.0))).astype(BF16)
    ones8 = (lax.broadcasted_iota(jnp.int32, (8, D_STATE), 0) < 3).astype(BF16)
    x_col = _dot_tn(x8, ones8)
    for hd in range(S_HEADS):
        g = hd // GROUP_HEADS
        hsl = slice(hd * S_HEAD_DIM, (hd + 1) * S_HEAD_DIM)
        bm = xbc[:, D_INNER + g * D_STATE:D_INNER + (g + 1) * D_STATE]
        cm = xbc[:, D_INNER + (S_GROUPS + g) * D_STATE:D_INNER + (S_GROUPS + g + 1) * D_STATE]
        new = st_ref[0, hd] * dec[:, hd:hd + 1] + x_col[hsl, :] * (bm * dt[:, hd:hd + 1])
        st_o[0, hd] = new
        cb8 = jnp.broadcast_to(cm, (8, D_STATE)).astype(BF16)
        y_ref[:, hsl] = _dot_nt(cb8, new.astype(BF16))
    y = y_ref[0:1, :] + dsk_ref[...] * xbc[:, :D_INNER]
    y = y * _silu(z_ref[0])
    for g in range(S_GROUPS):
        gsl = slice(g * GROUP_CH, (g + 1) * GROUP_CH)
        y_o[0, :, gsl] = _rms(y[:, gsl], gs_ref[:, gsl]).astype(BF16)


def _ssd_sample(xbc, conv_state, z, dt, ssm_state, conv_w, conv_b, dtb_row, na_row, dskip, g_ssd):
    nb = xbc.shape[0]
    consts = (conv_w, conv_b, dtb_row, na_row, dskip, g_ssd)
    st_spec = pl.BlockSpec((1, S_HEADS, S_HEAD_DIM, D_STATE), lambda i: (i, 0, 0, 0))
    return pl.pallas_call(
        _ssd_step_kernel,
        grid=(nb,),
        in_specs=[pl.BlockSpec((1, 1, CONV_CH), lambda i: (i, 0, 0)),
                  pl.BlockSpec((1, CONV_W - 1, CONV_CH), lambda i: (i, 0, 0)),
                  pl.BlockSpec((1, 1, D_INNER), lambda i: (i, 0, 0)),
                  pl.BlockSpec((1, 1, LANES), lambda i: (i, 0, 0)),
                  st_spec] + [_const_spec(a) for a in consts],
        out_specs=[pl.BlockSpec((1, 1, D_INNER), lambda i: (i, 0, 0)), st_spec],
        out_shape=[jax.ShapeDtypeStruct((nb, 1, D_INNER), BF16),
                   jax.ShapeDtypeStruct(ssm_state.shape, F32)],
        scratch_shapes=[pltpu.VMEM((8, D_INNER), F32)],
        compiler_params=_params("arbitrary"),
        name="ssd_sample",
    )(xbc, conv_state, z, dt, ssm_state, *consts)


def _rot_half(w):
    half = QK_ROPE // 2
    return jnp.concatenate([-w[..., half:], w[..., :half]], axis=-1)


def _pad_cols(w, start, total):
    return jnp.pad(w, ((0, 0), (start, total - start - w.shape[1])))


def _head_pad(w_nope, w_rope):
    k = w_nope.shape[0]
    pad = jnp.zeros((k, A_HEADS, HEAD_PAD - QK_NOPE - QK_ROPE), w_nope.dtype)
    return jnp.concatenate([w_nope, w_rope, pad], axis=-1).reshape(k, A_HEADS * HEAD_PAD)


def _rope_tables(pos):
    half = QK_ROPE // 2
    inv = ROPE_BASE ** (-jnp.arange(half, dtype=F32) / half)
    ang = pos.astype(F32)[:, None] * inv[None, :]
    cos, sin = jnp.cos(ang), jnp.sin(ang)
    n = pos.shape[0]
    ctab = jnp.concatenate([jnp.ones((n, QK_NOPE), F32), cos, cos,
                            jnp.zeros((n, HEAD_PAD - QK_NOPE - QK_ROPE), F32)], axis=1)
    stab = jnp.concatenate([jnp.zeros((n, QK_NOPE), F32), sin, sin,
                            jnp.zeros((n, HEAD_PAD - QK_NOPE - QK_ROPE), F32)], axis=1)
    return ctab, stab


def _pick_tile(n, target):
    t = min(n, target)
    assert n % t == 0, (n, t)
    return t


def kernel(x_prompt, x_sample, cache_ckv, cache_kpe, state_conv, state_ssm, page_table, c_prompt, c_sample, w_ada, b_ada, g_pre_mix, g_post_mix, g_pre_ffn, g_post_ffn, w_in, g_q_a, w_q_b, g_kv_a, w_uk, w_uv, w_o_attn, conv_w, conv_b, dt_bias, a_log, d_skip, g_ssd, w_o_ssd, w_out, w_router, b_router, w_gate_e, w_up_e, w_down_e, w_gate_s, w_up_s, w_down_s):
    bp, sp, d = x_prompt.shape
    nb, ds, _ = x_sample.shape
    depth = w_in.shape[0]
    assert depth == 1 and ds == 1 and d == D_MODEL
    n_pages = page_table.shape[1]
    past_len = n_pages * PAGE_SIZE
    lyr = 0

    offs = [0]
    for n in IN_SPLITS:
        offs.append(offs[-1] + n)
    win = w_in[lyr]
    w_qa, w_kva, w_kpe, w_z, w_xbc, w_dt, w_ga, w_gb = (win[:, offs[i]:offs[i + 1]] for i in range(8))
    wqb = w_q_b[lyr].reshape(Q_LORA, A_HEADS, QK_NOPE + QK_ROPE)
    wq_pad = _head_pad(wqb[..., :QK_NOPE], wqb[..., QK_NOPE:])
    wq_rot = _head_pad(jnp.zeros_like(wqb[..., :QK_NOPE]), _rot_half(wqb[..., QK_NOPE:]))
    wuk_pad = _head_pad(w_uk[lyr], jnp.zeros((KV_LORA, A_HEADS, QK_ROPE), F32))
    premix_w = [
        w_qa.astype(BF16), w_kva.astype(BF16),
        _pad_cols(w_kpe, QK_NOPE, LANES).astype(BF16),
        _pad_cols(_rot_half(w_kpe), QK_NOPE, LANES).astype(BF16),
        w_z.astype(BF16), w_xbc.astype(BF16),
        _pad_cols(w_dt, 0, LANES).astype(BF16), w_dt.T.astype(BF16),
        w_ga.astype(BF16), w_gb.astype(BF16),
        g_q_a[lyr][None, :], wq_pad.astype(BF16), wq_rot.astype(BF16),
        g_kv_a[lyr][None, :], wuk_pad.astype(BF16),
        w_uv[lyr].reshape(KV_LORA, A_HEADS * V_DIM).astype(BF16),
    ]
    merge_w = [w_o_attn[lyr].astype(BF16), w_o_ssd[lyr].astype(BF16), w_out[lyr].astype(BF16),
               g_post_mix[lyr][None, :], g_pre_ffn[lyr][None, :]]
    shared_w = [w_gate_s[lyr].astype(BF16), w_up_s[lyr].astype(BF16), w_down_s[lyr].astype(BF16)]
    w_router_t = w_router[lyr].T.astype(BF16)
    b_router_col = b_router[lyr][:, None]
    g_pre = g_pre_mix[lyr][None, :]
    g_post_ffn_row = g_post_ffn[lyr][None, :]
    cw = conv_w[lyr]
    cb = conv_b[lyr][None, :]
    neg_a = -jnp.exp(a_log[lyr].astype(F32))
    dtb_row = jnp.pad(dt_bias[lyr], (0, LANES - S_HEADS))[None, :]
    na_row = jnp.pad(neg_a, (0, LANES - S_HEADS))[None, :]
    dtb_col = dt_bias[lyr][:, None]
    na_col = neg_a[:, None]
    dskip_row = jnp.repeat(d_skip[lyr].astype(F32), S_HEAD_DIM)[None, :]
    gssd_row = g_ssd[lyr][None, :]

    n_mod_rows = bp + nb
    pad_rows = (-n_mod_rows) % 16
    c_all = jnp.concatenate([c_prompt, c_sample, jnp.zeros((pad_rows, d), F32)], axis=0)
    mod = _adaln(c_all, w_ada[lyr], b_ada[lyr][None, :])
    mods = [mod[:, i * d:(i + 1) * d] for i in range(6)]
    mods_p = [m[:bp][:, None, :] for m in mods]
    mods_s = [m[bp:bp + nb][None, :, :] for m in mods]

    ctab_p, stab_p = _rope_tables(jnp.arange(sp))
    tm = _pick_tile(sp, 256)
    (q_p, k_p, v_p, ckv_p, kpe_pad_p, z_p, xbc_p, dt_p, dtt_p, ga_p, gb_p) = _premix(
        x_prompt, mods_p[0], mods_p[1], g_pre, ctab_p, stab_p, premix_w, tm)
    o_attn_p = _flash_attention(q_p, k_p, v_p, _pick_tile(sp, 512), 2)
    y_ssd_p, ssm_t_p = _ssd_prompt(xbc_p, z_p, dt_p, dtt_p, cw, cb, dtb_row, dtb_col, na_row, na_col,
                                   dskip_row, gssd_row)
    x1_p, h2_p = _merge(x_prompt, o_attn_p, y_ssd_p, ga_p, gb_p, mods_p[2], mods_p[3], mods_p[4],
                        merge_w, tm)
    comb_p = _router(h2_p, w_router_t, b_router_col, tm)
    y_prompt = _moe(h2_p, comb_p, x1_p, mods_p[5], w_gate_e[lyr], w_up_e[lyr], w_down_e[lyr],
                    shared_w, g_post_ffn_row, _pick_tile(sp, 1024))

    xs = x_sample.reshape(1, nb, d)
    ctab_s, stab_s = _rope_tables(past_len + jnp.arange(ds))
    (q_s, _, _, ckv_s, kpe_pad_s, z_s, xbc_s, dt_s, _, ga_s, gb_s) = _premix(
        xs, mods_s[0], mods_s[1], g_pre, ctab_s, stab_s, premix_w, nb)
    kpe_s = kpe_pad_s[0, :, QK_NOPE:QK_NOPE + QK_ROPE]
    wuk_heads = jnp.transpose(w_uk[lyr], (1, 0, 2)).astype(BF16)
    wuv_heads = jnp.transpose(w_uv[lyr], (1, 0, 2)).astype(BF16)
    qlat = jnp.transpose(_qlat(q_s[0], wuk_heads), (1, 0, 2))
    qpe = q_s[0].reshape(nb, A_HEADS, HEAD_PAD)[:, :, QK_NOPE:QK_NOPE + QK_ROPE]
    o_lat = _decode_attention(page_table, qlat, qpe,
                              ckv_s.reshape(nb, 1, KV_LORA), kpe_s.reshape(nb, 1, QK_ROPE),
                              cache_ckv.reshape(cache_ckv.shape[1:]),
                              jnp.swapaxes(cache_kpe.reshape(cache_kpe.shape[1:]), 1, 2))
    o_attn_s = _value_up(jnp.transpose(o_lat, (1, 0, 2)), wuv_heads)[None]
    y_ssd_s, ssm_s = _ssd_sample(xbc_s.reshape(nb, 1, CONV_CH), state_conv[lyr],
                                 z_s.reshape(nb, 1, D_INNER), dt_s.reshape(nb, 1, LANES),
                                 state_ssm.reshape(state_ssm.shape[1:]), cw, cb, dtb_row, na_row,
                                 dskip_row, gssd_row)
    x1_s, h2_s = _merge(xs, o_attn_s, y_ssd_s.reshape(1, nb, D_INNER), ga_s, gb_s,
                        mods_s[2], mods_s[3], mods_s[4], merge_w, nb)
    comb_s = _router(h2_s, w_router_t, b_router_col, nb)
    y_s = _moe(h2_s, comb_s, x1_s, mods_s[5], w_gate_e[lyr], w_up_e[lyr], w_down_e[lyr],
               shared_w, g_post_ffn_row, nb)

    kpe_p = kpe_pad_p[:, :, QK_NOPE:QK_NOPE + QK_ROPE]
    conv_p = xbc_p[:, sp - (CONV_W - 1):, :]
    ssm_p = jnp.transpose(ssm_t_p.reshape(bp, D_STATE, S_HEADS, S_HEAD_DIM), (0, 2, 3, 1))
    conv_s = jnp.concatenate([state_conv[lyr][:, 1:, :], xbc_s.reshape(nb, 1, CONV_CH)], axis=1)
    return (y_prompt, y_s.reshape(nb, ds, d),
            ckv_p[None], kpe_p[None], conv_p[None], ssm_p[None].astype(x_prompt.dtype),
            ckv_s.reshape(1, nb, ds, KV_LORA), kpe_s.reshape(1, nb, ds, QK_ROPE),
            conv_s[None], ssm_s[None].astype(x_sample.dtype))
```

```python
import functools

import jax
import jax.numpy as jnp
from jax import lax
from jax.experimental import pallas as pl
from jax.experimental.pallas import tpu as pltpu

F32 = jnp.float32
BF16 = jnp.bfloat16

D_MODEL = 1024
PAGE_SIZE = 128
A_HEADS = 8
QK_NOPE = 64
QK_ROPE = 32
V_DIM = 64
Q_LORA = 384
KV_LORA = 256
ROPE_BASE = 10000.0
ATTN_SCALE = (QK_NOPE + QK_ROPE) ** -0.5
S_HEADS = 16
S_HEAD_DIM = 64
D_INNER = S_HEADS * S_HEAD_DIM
S_GROUPS = 2
D_STATE = 128
CONV_W = 4
CONV_CH = D_INNER + 2 * S_GROUPS * D_STATE
CHUNK = 128
N_EXPERTS = 64
TOP_K = 6
N_EXP_GROUPS = 8
TOPK_GROUPS = 4
F_EXPERT = 256
ROUTE_SCALE = 2.5
EPS = 1e-6
IN_SPLITS = (Q_LORA, KV_LORA, QK_ROPE, D_INNER, CONV_CH, S_HEADS, D_MODEL, D_MODEL)

LANES = 128
HEAD_PAD = LANES
GROUP_HEADS = S_HEADS // S_GROUPS
GROUP_CH = D_INNER // S_GROUPS
EXP_PER_GROUP = N_EXPERTS // N_EXP_GROUPS
MOE_BLOCK = 256
CHUNK_ROWS = 16
TILE_CHUNKS = 16
ROW_TILE = CHUNK_ROWS * TILE_CHUNKS
VMEM_LIMIT_BYTES = 56 * 1024 * 1024

NEG_INF = float("-inf")
LOG2_E = 1.4426950408889634


def _params(*semantics):
    return pltpu.CompilerParams(dimension_semantics=semantics, vmem_limit_bytes=VMEM_LIMIT_BYTES)


def _const_spec(arr):
    nd = arr.ndim
    return pl.BlockSpec(arr.shape, lambda *_: (0,) * nd)


def _dot(a, b):
    return jnp.dot(a, b, preferred_element_type=F32)


def _dot_nt(a, b):
    return lax.dot_general(a, b, (((1,), (1,)), ((), ())), preferred_element_type=F32)


def _dot_tn(a, b):
    return lax.dot_general(a, b, (((0,), (0,)), ((), ())), preferred_element_type=F32)


def _rms(x, g):
    return x * lax.rsqrt(jnp.mean(x * x, axis=-1, keepdims=True) + EPS) * g


def _silu(x):
    return x * (1.0 / (1.0 + jnp.exp(-x)))


def _sigmoid(x):
    return 1.0 / (1.0 + jnp.exp(-x))


def _softplus(x):
    return jnp.maximum(x, 0.0) + jnp.log(1.0 + jnp.exp(-jnp.abs(x)))


def _split3(x):
    hi = x.astype(BF16)
    r1 = x - hi.astype(F32)
    mid = r1.astype(BF16)
    lo = (r1 - mid.astype(F32)).astype(BF16)
    return hi, mid, lo


def _adaln_kernel(c_ref, w_ref, b_ref, o_ref):
    c = _silu(c_ref[...]).astype(BF16)
    o_ref[...] = _dot(c, w_ref[...].astype(BF16)) + b_ref[...]


def _adaln(c_all, w_ada, b_ada):
    rows, d = c_all.shape
    n = w_ada.shape[1]
    tn = 512
    return pl.pallas_call(
        _adaln_kernel,
        grid=(n // tn,),
        in_specs=[pl.BlockSpec((rows, d), lambda j: (0, 0)),
                  pl.BlockSpec((d, tn), lambda j: (0, j)),
                  pl.BlockSpec((1, tn), lambda j: (0, j))],
        out_specs=pl.BlockSpec((rows, tn), lambda j: (0, j)),
        out_shape=jax.ShapeDtypeStruct((rows, n), F32),
        compiler_params=_params("arbitrary"),
        name="adaln",
    )(c_all, w_ada, b_ada)


def _premix_kernel(x_ref, sh_ref, sc_ref, g_ref, ct_ref, st_ref,
                   wqa, wkva, wkpe, wkper, wz, wxbc, wdt, wdtt, wga, wgb,
                   gqa, wq, wqr, gkv, wuk, wuv,
                   q_o, k_o, v_o, ckv_o, kpe_o, z_o, xbc_o, dt_o, dtt_o, ga_o, gb_o):
    x = x_ref[0]
    h = _rms(x, g_ref[...]) * (1.0 + sc_ref[0]) + sh_ref[0]
    hb = h.astype(BF16)
    ct = ct_ref[...]
    st = st_ref[...]

    qn = _rms(_dot(hb, wqa[...]), gqa[...]).astype(BF16)
    q = _dot(qn, wq[...])
    qr = _dot(qn, wqr[...])
    for hd in range(A_HEADS):
        sl = slice(hd * HEAD_PAD, (hd + 1) * HEAD_PAD)
        q_o[0, :, sl] = (q[:, sl] * ct + qr[:, sl] * st).astype(BF16)

    ckv = _rms(_dot(hb, wkva[...]), gkv[...])
    ckv_o[0] = ckv
    cb = ckv.astype(BF16)
    kpe = _dot(hb, wkpe[...]) * ct + _dot(hb, wkper[...]) * st
    kpe_o[0] = kpe
    kn = _dot(cb, wuk[...])
    for hd in range(A_HEADS):
        sl = slice(hd * HEAD_PAD, (hd + 1) * HEAD_PAD)
        k_o[0, :, sl] = (kn[:, sl] + kpe).astype(BF16)
    v_o[0] = _dot(cb, wuv[...]).astype(BF16)

    z_o[0] = _dot(hb, wz[...])
    xbc_o[0] = _dot(hb, wxbc[...])
    dt_o[0] = _dot(hb, wdt[...])
    dtt_o[0] = _dot_nt(wdtt[...], hb)
    ga_o[0] = _dot(hb, wga[...])
    gb_o[0] = _dot(hb, wgb[...])


def _premix(x, sh, sc, g, ctab, stab, weights, tm):
    b, s, d = x.shape
    per_row_mod = sh.shape[1] != 1
    per_row_tab = ctab.shape[0] != 1
    mod_spec = (pl.BlockSpec((1, tm, d), lambda i, j: (i, j, 0)) if per_row_mod
                else pl.BlockSpec((1, 1, d), lambda i, j: (i, 0, 0)))
    tab_spec = (pl.BlockSpec((tm, LANES), lambda i, j: (j, 0)) if per_row_tab
                else pl.BlockSpec((1, LANES), lambda i, j: (0, 0)))

    def tok(n):
        return pl.BlockSpec((1, tm, n), lambda i, j: (i, j, 0))

    out_cols = [(A_HEADS * HEAD_PAD, BF16), (A_HEADS * HEAD_PAD, BF16), (A_HEADS * V_DIM, BF16),
                (KV_LORA, F32), (LANES, F32), (D_INNER, F32), (CONV_CH, F32), (LANES, F32)]
    out_shape = [jax.ShapeDtypeStruct((b, s, n), dt) for n, dt in out_cols]
    out_specs = [tok(n) for n, _ in out_cols]
    out_shape.append(jax.ShapeDtypeStruct((b, S_HEADS, s), F32))
    out_specs.append(pl.BlockSpec((1, S_HEADS, tm), lambda i, j: (i, 0, j)))
    out_shape += [jax.ShapeDtypeStruct((b, s, D_MODEL), F32)] * 2
    out_specs += [tok(D_MODEL)] * 2
    order = [out_shape[i] for i in (0, 1, 2, 3, 4, 5, 6, 7, 8, 9, 10)]
    return pl.pallas_call(
        _premix_kernel,
        grid=(b, s // tm),
        in_specs=[tok(d), mod_spec, mod_spec, _const_spec(g), tab_spec, tab_spec]
        + [_const_spec(w) for w in weights],
        out_specs=out_specs,
        out_shape=order,
        compiler_params=_params("arbitrary", "arbitrary"),
        name="premix",
    )(x, sh, sc, g, ctab, stab, *weights)


def _flash_kernel(q_ref, k_ref, v_ref, o_ref, m_ref, l_ref, acc_ref, *, tq, heads):
    qi = pl.program_id(2)
    m_ref[...] = jnp.full(m_ref.shape, NEG_INF, F32)
    l_ref[...] = jnp.zeros(l_ref.shape, F32)
    acc_ref[...] = jnp.zeros(acc_ref.shape, F32)
    row = lax.broadcasted_iota(jnp.int32, (tq, tq), 0)
    col = lax.broadcasted_iota(jnp.int32, (tq, tq), 1)
    c2 = ATTN_SCALE * LOG2_E

    def tile(kj, on_diagonal):
        koff = pl.multiple_of(kj * tq, tq)
        for hh in range(heads):
            sl = slice(hh * HEAD_PAD, (hh + 1) * HEAD_PAD)
            vsl = slice((hh // 2) * 2 * V_DIM, (hh // 2 + 1) * 2 * V_DIM)
            s = _dot_nt(q_ref[0, :, sl], k_ref[0, pl.ds(koff, tq), sl])
            if on_diagonal:
                s = jnp.where(col <= row, s, NEG_INF)
            m_prev = m_ref[hh]
            m_new = jnp.maximum(m_prev, jnp.max(s, axis=-1, keepdims=True))
            alpha = jnp.exp2((m_prev - m_new) * c2)
            p_parts = [jnp.exp2((s[:, j * LANES:(j + 1) * LANES] - m_new) * c2)
                       for j in range(tq // LANES)]
            l_ref[hh] = alpha * l_ref[hh] + sum(p_parts)
            p = jnp.concatenate(p_parts, axis=1).astype(BF16)
            acc_ref[hh] = alpha * acc_ref[hh] + _dot(p, v_ref[0, pl.ds(koff, tq), vsl])
            m_ref[hh] = m_new

    def body(kj, carry):
        tile(kj, False)
        return carry

    lax.fori_loop(0, qi, body, 0)
    tile(qi, True)
    lane = lax.broadcasted_iota(jnp.int32, (tq, 2 * V_DIM), 1)
    for pair in range(heads // 2):
        o0 = acc_ref[2 * pair] / jnp.sum(l_ref[2 * pair], axis=-1, keepdims=True)
        o1 = acc_ref[2 * pair + 1] / jnp.sum(l_ref[2 * pair + 1], axis=-1, keepdims=True)
        o_ref[0, :, pair * 2 * V_DIM:(pair + 1) * 2 * V_DIM] = jnp.where(lane < V_DIM, o0, o1).astype(BF16)


def _flash_attention(q, k, v, tq, heads):
    b, s, _ = q.shape
    return pl.pallas_call(
        functools.partial(_flash_kernel, tq=tq, heads=heads),
        grid=(b, A_HEADS // heads, s // tq),
        in_specs=[pl.BlockSpec((1, tq, heads * HEAD_PAD), lambda bi, hg, qi: (bi, qi, hg)),
                  pl.BlockSpec((1, s, heads * HEAD_PAD), lambda bi, hg, qi: (bi, 0, hg)),
                  pl.BlockSpec((1, s, heads * V_DIM), lambda bi, hg, qi: (bi, 0, hg))],
        out_specs=pl.BlockSpec((1, tq, heads * V_DIM), lambda bi, hg, qi: (bi, qi, hg)),
        out_shape=jax.ShapeDtypeStruct((b, s, A_HEADS * V_DIM), BF16),
        scratch_shapes=[pltpu.VMEM((heads, tq, LANES), F32), pltpu.VMEM((heads, tq, LANES), F32),
                        pltpu.VMEM((heads, tq, 2 * V_DIM), F32)],
        compiler_params=_params("arbitrary", "arbitrary", "arbitrary"),
        name="flash_attention",
    )(q, k, v)


def _ssd_chunk_kernel(xbc_ref, z_ref, dt_ref, dtt_ref, cw_ref, cb_ref, dtb_ref, dtbc_ref,
                      na_ref, nac_ref, dsk_ref, gs_ref, y_o, st_o, xc_ref, state_ref, y_ref):
    c = pl.program_id(1)
    nc = pl.num_programs(1)

    @pl.when(c == 0)
    def _():
        xc_ref[0:8, :] = jnp.zeros((8, CONV_CH), F32)
        state_ref[...] = jnp.zeros(state_ref.shape, F32)

    xc_ref[8:8 + CHUNK, :] = xbc_ref[0]
    conv = cb_ref[...] + cw_ref[CONV_W - 1:CONV_W, :] * xc_ref[8:8 + CHUNK, :]
    for j in range(1, CONV_W):
        conv = conv + cw_ref[CONV_W - 1 - j:CONV_W - j, :] * xc_ref[8 - j:8 - j + CHUNK, :]
    xc_ref[0:8, :] = xc_ref[CHUNK:CHUNK + 8, :]
    xbc = _silu(conv)

    dt_c = _softplus(dt_ref[0] + dtb_ref[...])
    a_c = dt_c * na_ref[...]
    dt_r = _softplus(dtt_ref[0] + dtbc_ref[...])
    a_r = dt_r * nac_ref[...]
    li = lax.broadcasted_iota(jnp.int32, (CHUNK, CHUNK), 0)
    si = lax.broadcasted_iota(jnp.int32, (CHUNK, CHUNK), 1)
    lower = (si <= li)
    tri = lower.astype(BF16)
    acum_c = sum(_dot(tri, part) for part in _split3(a_c))
    acum_r = sum(_dot_nt(part, tri) for part in _split3(a_r))
    a_last = acum_c[CHUNK - 1:CHUNK, :]
    dec_end = jnp.exp(a_last - acum_c)
    exp_ac = jnp.exp(acum_c)
    dec_all = jnp.exp(a_last)

    for g in range(S_GROUPS):
        bm = xbc[:, D_INNER + g * D_STATE:D_INNER + (g + 1) * D_STATE]
        cm = xbc[:, D_INNER + (S_GROUPS + g) * D_STATE:D_INNER + (S_GROUPS + g + 1) * D_STATE]
        bmb = bm.astype(BF16)
        cmb = cm.astype(BF16)
        gmat = _dot_nt(cmb, bmb)
        gsl = slice(g * GROUP_CH, (g + 1) * GROUP_CH)
        y_off = _dot(cmb, state_ref[:, gsl].astype(BF16))
        xde_parts = []
        for hh in range(GROUP_HEADS):
            hd = g * GROUP_HEADS + hh
            hsl = slice(hd * S_HEAD_DIM, (hd + 1) * S_HEAD_DIM)
            xs = xbc[:, hsl]
            xdt = xs * dt_c[:, hd:hd + 1]
            seg = acum_c[:, hd:hd + 1] - acum_r[hd:hd + 1, :]
            lmat = jnp.exp(jnp.where(lower, seg, NEG_INF))
            y_diag = _dot((gmat * lmat).astype(BF16), xdt.astype(BF16))
            yo = y_off[:, hh * S_HEAD_DIM:(hh + 1) * S_HEAD_DIM] * exp_ac[:, hd:hd + 1]
            y_ref[:, hsl] = y_diag + yo
            xde_parts.append((xdt * dec_end[:, hd:hd + 1]).astype(BF16))
        xde = jnp.concatenate(xde_parts, axis=1)
        new_states = _dot_tn(bmb, xde)
        for hh in range(GROUP_HEADS):
            hd = g * GROUP_HEADS + hh
            hsl = slice(hd * S_HEAD_DIM, (hd + 1) * S_HEAD_DIM)
            lsl = slice(hh * S_HEAD_DIM, (hh + 1) * S_HEAD_DIM)
            state_ref[:, hsl] = state_ref[:, hsl] * dec_all[:, hd:hd + 1] + new_states[:, lsl]

    y = y_ref[...] + dsk_ref[...] * xbc[:, :D_INNER]
    y = y * _silu(z_ref[0])
    for g in range(S_GROUPS):
        gsl = slice(g * GROUP_CH, (g + 1) * GROUP_CH)
        y_o[0, :, gsl] = _rms(y[:, gsl], gs_ref[:, gsl]).astype(BF16)

    @pl.when(c == nc - 1)
    def _():
        st_o[0] = state_ref[...]


def _ssd_prompt(xbc, z, dt, dtt, conv_w, conv_b, dtb_row, dtb_col, na_row, na_col, dskip, g_ssd):
    b, s, _ = xbc.shape
    nc = s // CHUNK
    consts = (conv_w, conv_b, dtb_row, dtb_col, na_row, na_col, dskip, g_ssd)
    return pl.pallas_call(
        _ssd_chunk_kernel,
        grid=(b, nc),
        in_specs=[pl.BlockSpec((1, CHUNK, CONV_CH), lambda i, j: (i, j, 0)),
                  pl.BlockSpec((1, CHUNK, D_INNER), lambda i, j: (i, j, 0)),
                  pl.BlockSpec((1, CHUNK, LANES), lambda i, j: (i, j, 0)),
                  pl.BlockSpec((1, S_HEADS, CHUNK), lambda i, j: (i, 0, j))]
        + [_const_spec(a) for a in consts],
        out_specs=[pl.BlockSpec((1, CHUNK, D_INNER), lambda i, j: (i, j, 0)),
                   pl.BlockSpec((1, D_STATE, D_INNER), lambda i, j: (i, 0, 0))],
        out_shape=[jax.ShapeDtypeStruct((b, s, D_INNER), BF16),
                   jax.ShapeDtypeStruct((b, D_STATE, D_INNER), F32)],
        scratch_shapes=[pltpu.VMEM((CHUNK + 8, CONV_CH), F32),
                        pltpu.VMEM((D_STATE, D_INNER), F32),
                        pltpu.VMEM((CHUNK, D_INNER), F32)],
        compiler_params=_params("arbitrary", "arbitrary"),
        name="ssd_prompt",
    )(xbc, z, dt, dtt, *consts)


def _merge_kernel(x_ref, oa_ref, ys_ref, ga_ref, gb_ref, g1_ref, sh2_ref, sc2_ref,
                  woa, wos, wout, gpost, gpre, x1_o, h2_o):
    o_attn = _dot(oa_ref[0], woa[...])
    o_ssd = _dot(ys_ref[0], wos[...])
    merged = _sigmoid(ga_ref[0]) * o_attn + _sigmoid(gb_ref[0]) * o_ssd
    m = _dot(merged.astype(BF16), wout[...])
    x1 = x_ref[0] + g1_ref[0] * _rms(m, gpost[...])
    x1_o[0] = x1
    h2_o[0] = (_rms(x1, gpre[...]) * (1.0 + sc2_ref[0]) + sh2_ref[0]).astype(BF16)


def _merge(x, o_attn, y_ssd, ga, gb, g1, sh2, sc2, weights, tm):
    b, s, d = x.shape
    per_row_mod = g1.shape[1] != 1
    mod_spec = (pl.BlockSpec((1, tm, d), lambda i, j: (i, j, 0)) if per_row_mod
                else pl.BlockSpec((1, 1, d), lambda i, j: (i, 0, 0)))

    def tok(n):
        return pl.BlockSpec((1, tm, n), lambda i, j: (i, j, 0))

    return pl.pallas_call(
        _merge_kernel,
        grid=(b, s // tm),
        in_specs=[tok(d), tok(A_HEADS * V_DIM), tok(D_INNER), tok(d), tok(d),
                  mod_spec, mod_spec, mod_spec] + [_const_spec(w) for w in weights],
        out_specs=[tok(d), tok(d)],
        out_shape=[jax.ShapeDtypeStruct((b, s, d), F32), jax.ShapeDtypeStruct((b, s, d), BF16)],
        compiler_params=_params("arbitrary", "arbitrary"),
        name="merge",
    )(x, o_attn, y_ssd, ga, gb, g1, sh2, sc2, *weights)


def _first_max(vals, idx, big):
    m = vals[0]
    for v in vals[1:]:
        m = jnp.maximum(m, v)
    m = jnp.max(m, axis=0, keepdims=True)
    cand = [jnp.where(v == m, i, big) for v, i in zip(vals, idx)]
    a = cand[0]
    for cnd in cand[1:]:
        a = jnp.minimum(a, cnd)
    a = jnp.min(a, axis=0, keepdims=True)
    return m, a


def _router_kernel(h_ref, wt_ref, b_ref, o_ref, slot_o, w_o, cnt_o, *, tm):
    logits = _dot_nt(wt_ref[...], h_ref[0])
    scores = _sigmoid(logits)
    biased = scores + b_ref[...]
    sub = lax.broadcasted_iota(jnp.int32, (EXP_PER_GROUP, tm), 0)
    slabs = [biased[g * EXP_PER_GROUP:(g + 1) * EXP_PER_GROUP, :] for g in range(N_EXP_GROUPS)]
    big = jnp.int32(N_EXPERTS)

    gscore = []
    for g in range(N_EXP_GROUPS):
        m1, a1 = _first_max([slabs[g]], [sub], big)
        rest = jnp.where(sub == a1, NEG_INF, slabs[g])
        m2 = jnp.max(rest, axis=0, keepdims=True)
        gscore.append(m1 + m2)
    gs = jnp.full((N_EXP_GROUPS, tm), NEG_INF, F32)
    for g in range(N_EXP_GROUPS):
        gs = jnp.where(sub == g, gscore[g], gs)
    gsel = jnp.zeros((N_EXP_GROUPS, tm), F32)
    for _ in range(TOPK_GROUPS):
        _, a = _first_max([gs], [sub], big)
        hit = sub == a
        gsel = jnp.where(hit, 1.0, gsel)
        gs = jnp.where(hit, NEG_INF, gs)

    masked = [jnp.where(gsel[g:g + 1, :] > 0.5, slabs[g], NEG_INF) for g in range(N_EXP_GROUPS)]
    flat = [sub + g * EXP_PER_GROUP for g in range(N_EXP_GROUPS)]
    chosen = [jnp.zeros((EXP_PER_GROUP, tm), jnp.bool_) for _ in range(N_EXP_GROUPS)]
    picks = []
    for _ in range(TOP_K):
        _, a = _first_max(masked, flat, big)
        picks.append(a)
        for g in range(N_EXP_GROUPS):
            hit = flat[g] == a
            chosen[g] = jnp.logical_or(chosen[g], hit)
            masked[g] = jnp.where(hit, NEG_INF, masked[g])
    score_slabs = [scores[g * EXP_PER_GROUP:(g + 1) * EXP_PER_GROUP, :] for g in range(N_EXP_GROUPS)]
    w = [jnp.where(chosen[g], score_slabs[g], 0.0) for g in range(N_EXP_GROUPS)]
    tot = w[0]
    for part in w[1:]:
        tot = tot + part
    tot = jnp.sum(tot, axis=0, keepdims=True)

    sel_t = jnp.concatenate([jnp.where(c, 1.0, 0.0) for c in chosen], axis=0)
    t_row = lax.broadcasted_iota(jnp.int32, (tm, tm), 0)
    t_col = lax.broadcasted_iota(jnp.int32, (tm, tm), 1)
    rank_t = _dot(sel_t.astype(BF16), (t_row < t_col).astype(BF16))
    cnt = jnp.sum(sel_t, axis=-1, keepdims=True)
    pc = jnp.floor((cnt + (CHUNK_ROWS - 1.0)) * (1.0 / CHUNK_ROWS)) * CHUNK_ROWS
    total = jnp.sum(pc, axis=0, keepdims=True)
    fill = jnp.ceil(total * (1.0 / ROW_TILE)) * ROW_TILE - total
    e_row = lax.broadcasted_iota(jnp.int32, (N_EXPERTS, 1), 0)
    pc = pc + jnp.where(e_row == N_EXPERTS - 1, fill, 0.0)
    e_r = lax.broadcasted_iota(jnp.int32, (N_EXPERTS, N_EXPERTS), 0)
    e_c = lax.broadcasted_iota(jnp.int32, (N_EXPERTS, N_EXPERTS), 1)
    pc_rep = jnp.broadcast_to(pc, (N_EXPERTS, LANES))
    off = _dot((e_c < e_r).astype(BF16), pc_rep.astype(BF16))
    slot_t = off[:, :1] + rank_t
    slot8 = jnp.full((8, tm), -1.0, F32)
    w8 = jnp.zeros((8, tm), F32)
    for k, a in enumerate(picks):
        s_acc = jnp.zeros((EXP_PER_GROUP, tm), F32)
        w_acc = jnp.zeros((EXP_PER_GROUP, tm), F32)
        for g in range(N_EXP_GROUPS):
            hit = flat[g] == a
            s_acc = jnp.where(hit, slot_t[g * EXP_PER_GROUP:(g + 1) * EXP_PER_GROUP, :], s_acc)
            w_acc = jnp.where(hit, score_slabs[g], w_acc)
        slot8 = jnp.where(sub == k, jnp.sum(s_acc, axis=0, keepdims=True), slot8)
        w8 = jnp.where(sub == k, jnp.sum(w_acc, axis=0, keepdims=True) / tot * ROUTE_SCALE, w8)

    info_t = jnp.concatenate([part / tot * ROUTE_SCALE for part in w] + [slot8, w8]
                             + [jnp.zeros((LANES - N_EXPERTS - 16, tm), F32)], axis=0)
    o_ref[0] = info_t.T
    slot_o[0] = slot8
    w_o[0] = w8
    cnt_o[0] = pc_rep


def _router(h2, w_router_t, b_col, tm):
    b, s, d = h2.shape
    nj = s // tm
    return pl.pallas_call(
        functools.partial(_router_kernel, tm=tm),
        grid=(b, nj),
        in_specs=[pl.BlockSpec((1, tm, d), lambda i, j: (i, j, 0)),
                  _const_spec(w_router_t), _const_spec(b_col)],
        out_specs=[pl.BlockSpec((1, tm, LANES), lambda i, j: (i, j, 0)),
                   pl.BlockSpec((1, 8, tm), lambda i, j: (i, 0, j)),
                   pl.BlockSpec((1, 8, tm), lambda i, j: (i, 0, j)),
                   pl.BlockSpec((1, N_EXPERTS, LANES), lambda i, j: (i * nj + j, 0, 0))],
        out_shape=[jax.ShapeDtypeStruct((b, s, LANES), F32),
                   jax.ShapeDtypeStruct((b, 8, s), F32),
                   jax.ShapeDtypeStruct((b, 8, s), F32),
                   jax.ShapeDtypeStruct((b * nj, N_EXPERTS, LANES), F32)],
        compiler_params=_params("arbitrary", "arbitrary"),
        name="router",
    )(h2, w_router_t, b_col)


def _dispatch_kernel(h_ref, slot_ref, w_ref, xs_ref, os_ref, *, tm, cap):
    os_ref[...] = jnp.zeros(os_ref.shape, BF16)
    h = h_ref[0]
    slots = slot_ref[0]
    ws = w_ref[0]
    lane = lax.broadcasted_iota(jnp.int32, (ROW_TILE, LANES), 1)
    for rt in range(cap // ROW_TILE):
        rows = (rt * ROW_TILE + lax.broadcasted_iota(jnp.int32, (ROW_TILE, tm), 0)).astype(F32)
        p = jnp.zeros((ROW_TILE, tm), F32)
        pw = jnp.zeros((ROW_TILE, tm), F32)
        for k in range(TOP_K):
            hit = rows == slots[k:k + 1, :]
            p = jnp.where(hit, 1.0, p)
            pw = jnp.where(hit, ws[k:k + 1, :], pw)
        x = _dot(p.astype(BF16), h)
        hi, mid, lo = (part.astype(F32) for part in _split3(jnp.sum(pw, axis=-1, keepdims=True)))
        extra = jnp.where(lane == 0, hi, jnp.where(lane == 1, mid, jnp.where(lane == 2, lo, 0.0)))
        rsl = slice(rt * ROW_TILE, (rt + 1) * ROW_TILE)
        xs_ref[0, rsl, :D_MODEL] = x.astype(BF16)
        xs_ref[0, rsl, D_MODEL:] = extra.astype(BF16)


def _dispatch(h2, slotk, wk, tm, cap):
    b, s, d = h2.shape
    nj = s // tm
    return pl.pallas_call(
        functools.partial(_dispatch_kernel, tm=tm, cap=cap),
        grid=(b, nj),
        in_specs=[pl.BlockSpec((1, tm, d), lambda i, j: (i, j, 0)),
                  pl.BlockSpec((1, 8, tm), lambda i, j: (i, 0, j)),
                  pl.BlockSpec((1, 8, tm), lambda i, j: (i, 0, j))],
        out_specs=[pl.BlockSpec((1, cap, d + LANES), lambda i, j: (i * nj + j, 0, 0)),
                   pl.BlockSpec((1, cap, d), lambda i, j: (i * nj + j, 0, 0))],
        out_shape=[jax.ShapeDtypeStruct((b * nj, cap, d + LANES), BF16),
                   jax.ShapeDtypeStruct((b * nj, cap, d), BF16)],
        compiler_params=_params("arbitrary", "arbitrary"),
        name="moe_dispatch",
    )(h2, slotk, wk)


def _expert_kernel(te_ref, src_ref, dst_ref, nt_ref, xs_hbm, wg_ref, wu_ref, wd_ref, os_zero_hbm,
                   os_hbm, xbuf, obuf, sem):
    del os_zero_hbm
    i = pl.program_id(0)
    nt = nt_ref[0]

    def in_copies(tile, slot):
        return [pltpu.make_async_copy(xs_hbm.at[src_ref[tile * TILE_CHUNKS + q]], xbuf.at[slot, q],
                                      sem.at[0, slot]) for q in range(TILE_CHUNKS)]

    def out_copies(tile, slot):
        return [pltpu.make_async_copy(obuf.at[slot, q], os_hbm.at[dst_ref[tile * TILE_CHUNKS + q]],
                                      sem.at[1, slot]) for q in range(TILE_CHUNKS)]

    @pl.when(i < nt)
    def _():
        slot = lax.rem(i, 2)

        @pl.when(i == 0)
        def _():
            for cpy in in_copies(0, 0):
                cpy.start()

        @pl.when(i + 1 < nt)
        def _():
            for cpy in in_copies(i + 1, 1 - slot):
                cpy.start()

        for cpy in in_copies(i, slot):
            cpy.wait()
        x = xbuf[slot].reshape(ROW_TILE, D_MODEL + LANES)
        xb = x[:, :D_MODEL]
        wrow = jnp.sum(x[:, D_MODEL:].astype(F32), axis=-1, keepdims=True)
        gte = _dot(xb, wg_ref[0].astype(BF16))
        upe = _dot(xb, wu_ref[0].astype(BF16))
        act = (_silu(gte) * upe * wrow).astype(BF16)
        out = _dot(act, wd_ref[0].astype(BF16)).astype(BF16)

        @pl.when(i >= 2)
        def _():
            for cpy in out_copies(i - 2, slot):
                cpy.wait()

        obuf[slot] = out.reshape(TILE_CHUNKS, CHUNK_ROWS, D_MODEL)
        for cpy in out_copies(i, slot):
            cpy.start()

        @pl.when(i == nt - 1)
        def _():
            @pl.when(i >= 1)
            def _():
                for cpy in out_copies(i - 1, 1 - slot):
                    cpy.wait()

            for cpy in out_copies(i, slot):
                cpy.wait()


def _experts(tile_expert, src, dst, n_tiles, xs_chunks, os_zero, w_gate_e, w_up_e, w_down_e, max_tiles):
    n_chunks, _, width = xs_chunks.shape
    d = width - LANES
    grid_spec = pltpu.PrefetchScalarGridSpec(
        num_scalar_prefetch=4,
        grid=(max_tiles,),
        in_specs=[pl.BlockSpec(memory_space=pl.ANY),
                  pl.BlockSpec((1, d, F_EXPERT), lambda i, te, s_, d_, n_: (te[i], 0, 0)),
                  pl.BlockSpec((1, d, F_EXPERT), lambda i, te, s_, d_, n_: (te[i], 0, 0)),
                  pl.BlockSpec((1, F_EXPERT, d), lambda i, te, s_, d_, n_: (te[i], 0, 0)),
                  pl.BlockSpec(memory_space=pl.ANY)],
        out_specs=pl.BlockSpec(memory_space=pl.ANY),
        scratch_shapes=[pltpu.VMEM((2, TILE_CHUNKS, CHUNK_ROWS, width), BF16),
                        pltpu.VMEM((2, TILE_CHUNKS, CHUNK_ROWS, d), BF16),
                        pltpu.SemaphoreType.DMA((2, 2))],
    )
    return pl.pallas_call(
        _expert_kernel,
        grid_spec=grid_spec,
        out_shape=jax.ShapeDtypeStruct((n_chunks, CHUNK_ROWS, d), BF16),
        input_output_aliases={8: 0},
        compiler_params=_params("arbitrary"),
        name="moe_experts",
    )(tile_expert, src, dst, n_tiles, xs_chunks, w_gate_e, w_up_e, w_down_e, os_zero)


def _combine_kernel(nk_ref, cum_ref, info_ref, h_ref, x1_ref, g2_ref, os_hbm, wgs, wus, wds, gpost,
                    y_o, obuf, sem, *, tm):
    nj = pl.num_programs(1)
    blk = pl.program_id(0) * nj + pl.program_id(1)
    nblk = pl.num_programs(0) * nj
    nk = nk_ref[blk]

    def fetch(bk, kt, slot):
        rows = pl.ds(pl.multiple_of(kt * ROW_TILE, ROW_TILE), ROW_TILE)
        return pltpu.make_async_copy(os_hbm.at[bk, rows], obuf.at[slot], sem.at[slot])

    @pl.when(blk == 0)
    def _():
        fetch(0, 0, 0).start()

    hb = h_ref[0]
    hid = _silu(_dot(hb, wgs[...])) * _dot(hb, wus[...])
    acc0 = _dot(hid.astype(BF16), wds[...])
    info = info_ref[0]
    slot_cols = [jnp.broadcast_to(info[:, N_EXPERTS + k:N_EXPERTS + k + 1], (tm, ROW_TILE))
                 for k in range(TOP_K)]
    lane_row = lax.broadcasted_iota(jnp.int32, (tm, ROW_TILE), 1).astype(F32)

    def body(kt, acc):
        slot = lax.rem(cum_ref[blk] + kt, 2)

        @pl.when(kt + 1 < nk)
        def _():
            fetch(blk, kt + 1, 1 - slot).start()

        @pl.when(jnp.logical_and(kt + 1 == nk, blk + 1 < nblk))
        def _():
            fetch(blk + 1, 0, 1 - slot).start()

        fetch(blk, kt, slot).wait()
        rows = lane_row + (kt * ROW_TILE).astype(F32)
        pt = jnp.zeros((tm, ROW_TILE), F32)
        for col in slot_cols:
            pt = jnp.where(rows == col, 1.0, pt)
        return acc + _dot(pt.astype(BF16), obuf[slot])

    acc = lax.fori_loop(0, nk, body, acc0)
    y_o[0] = x1_ref[0] + g2_ref[0] * _rms(acc, gpost[...])


def _combine(nk, cum_nk, info, h2, x1, g2, os_blocks, shared, gpost, tm):
    b, s, d = h2.shape
    nj = s // tm

    def tok(n):
        return pl.BlockSpec((1, tm, n), lambda i, j, *_: (i, j, 0))

    grid_spec = pltpu.PrefetchScalarGridSpec(
        num_scalar_prefetch=2,
        grid=(b, nj),
        in_specs=[tok(LANES), tok(d), tok(d), pl.BlockSpec((1, 1, d), lambda i, j, *_: (i, 0, 0)),
                  pl.BlockSpec(memory_space=pl.ANY)]
        + [_const_spec(w) for w in shared] + [_const_spec(gpost)],
        out_specs=tok(d),
        scratch_shapes=[pltpu.VMEM((2, ROW_TILE, d), BF16), pltpu.SemaphoreType.DMA((2,))],
    )
    return pl.pallas_call(
        functools.partial(_combine_kernel, tm=tm),
        grid_spec=grid_spec,
        out_shape=jax.ShapeDtypeStruct((b, s, d), F32),
        compiler_params=_params("arbitrary", "arbitrary"),
        name="moe_combine",
    )(nk, cum_nk, info, h2, x1, g2, os_blocks, *shared, gpost)


def _moe_plan(pc_rep, cap_chunks, max_tiles):
    pc = pc_rep[:, :, 0].astype(jnp.int32)
    nblk = pc.shape[0]
    nch = pc // CHUNK_ROWS
    off = jnp.cumsum(nch, axis=1) - nch
    nk = jnp.sum(pc, axis=1) // ROW_TILE
    cum_nk = jnp.cumsum(nk) - nk
    cum_blk = jnp.cumsum(nch, axis=0)
    tot = cum_blk[-1]
    tiles = (tot + TILE_CHUNKS - 1) // TILE_CHUNKS
    tile_end = jnp.cumsum(tiles)
    n_tiles = tile_end[-1]
    tile_ids = jnp.arange(max_tiles, dtype=jnp.int32)
    te = jnp.sum((tile_end[None, :] <= tile_ids[:, None]).astype(jnp.int32), axis=1)
    te = jnp.minimum(te, N_EXPERTS - 1)
    te = jnp.where(tile_ids < n_tiles, te, te[jnp.maximum(n_tiles - 1, 0)])
    pos = ((tile_ids - (tile_end - tiles)[te])[:, None] * TILE_CHUNKS
           + jnp.arange(TILE_CHUNKS, dtype=jnp.int32)[None, :])
    valid = jnp.logical_and(tile_ids[:, None] < n_tiles, pos < tot[te][:, None])
    cum_e = cum_blk.T[te]
    blk = jnp.sum((cum_e[:, None, :] <= pos[:, :, None]).astype(jnp.int32), axis=-1)
    blk = jnp.minimum(blk, nblk - 1)
    before = jnp.take_along_axis(cum_e - nch.T[te], blk, axis=1)
    local = off[blk, te[:, None]] + (pos - before)
    chunk = blk * cap_chunks + local
    spare = cap_chunks - TILE_CHUNKS + jnp.arange(TILE_CHUNKS, dtype=jnp.int32)[None, :]
    src = jnp.where(valid, chunk, cap_chunks - 1)
    dst = jnp.where(valid, chunk, spare)
    return (te.astype(jnp.int32), src.reshape(-1).astype(jnp.int32), dst.reshape(-1).astype(jnp.int32),
            n_tiles.reshape(1).astype(jnp.int32), nk.astype(jnp.int32), cum_nk.astype(jnp.int32))


def _sparse_moe(h2, info, slotk, wk, pc_rep, x1, g2, w_gate_e, w_up_e, w_down_e, shared, gpost, tm):
    b, s, d = h2.shape
    nblk = b * (s // tm)
    cap = TOP_K * tm + N_EXPERTS * CHUNK_ROWS + ROW_TILE
    cap = -(-cap // ROW_TILE) * ROW_TILE
    cap_chunks = cap // CHUNK_ROWS
    max_tiles = nblk * (cap_chunks - TILE_CHUNKS) // TILE_CHUNKS + N_EXPERTS
    te, src, dst, n_tiles, nk, cum_nk = _moe_plan(pc_rep, cap_chunks, max_tiles)
    xs, os_zero = _dispatch(h2, slotk, wk, tm, cap)
    os_chunks = _experts(te, src, dst, n_tiles, xs.reshape(nblk * cap_chunks, CHUNK_ROWS, d + LANES),
                         os_zero.reshape(nblk * cap_chunks, CHUNK_ROWS, d),
                         w_gate_e, w_up_e, w_down_e, max_tiles)
    return _combine(nk, cum_nk, info, h2, x1, g2, os_chunks.reshape(nblk, cap, d), shared, gpost, tm)


def _moe_kernel(h_ref, c_ref, x1_ref, g2_ref, wg_ref, wu_ref, wd_ref, wgs, wus, wds, gpost,
                y_o, acc_ref, *, tm):
    e = pl.program_id(2)
    ne = pl.num_programs(2)
    hb = h_ref[0]

    @pl.when(e == 0)
    def _():
        hid = _silu(_dot(hb, wgs[...])) * _dot(hb, wus[...])
        acc_ref[...] = _dot(hid.astype(BF16), wds[...])

    lane = lax.broadcasted_iota(jnp.int32, (tm, LANES), 1)
    ce = jnp.sum(jnp.where(lane == e, c_ref[0], 0.0), axis=-1, keepdims=True)
    gte = _dot(hb, wg_ref[0].astype(BF16))
    upe = _dot(hb, wu_ref[0].astype(BF16))
    act = (_silu(gte) * upe * ce).astype(BF16)
    acc_ref[...] += _dot(act, wd_ref[0].astype(BF16))

    @pl.when(e == ne - 1)
    def _():
        y_o[0] = x1_ref[0] + g2_ref[0] * _rms(acc_ref[...], gpost[...])


def _moe(h2, comb, x1, g2, w_gate_e, w_up_e, w_down_e, shared, gpost, tm):
    b, s, d = h2.shape
    per_row_mod = g2.shape[1] != 1
    mod_spec = (pl.BlockSpec((1, tm, d), lambda i, j, e: (i, j, 0)) if per_row_mod
                else pl.BlockSpec((1, 1, d), lambda i, j, e: (i, 0, 0)))

    def tok(n):
        return pl.BlockSpec((1, tm, n), lambda i, j, e: (i, j, 0))

    return pl.pallas_call(
        functools.partial(_moe_kernel, tm=tm),
        grid=(b, s // tm, N_EXPERTS),
        in_specs=[tok(d), tok(LANES), tok(d), mod_spec,
                  pl.BlockSpec((1, d, F_EXPERT), lambda i, j, e: (e, 0, 0)),
                  pl.BlockSpec((1, d, F_EXPERT), lambda i, j, e: (e, 0, 0)),
                  pl.BlockSpec((1, F_EXPERT, d), lambda i, j, e: (e, 0, 0))]
        + [_const_spec(w) for w in shared] + [_const_spec(gpost)],
        out_specs=tok(d),
        out_shape=jax.ShapeDtypeStruct((b, s, d), F32),
        scratch_shapes=[pltpu.VMEM((tm, d), F32)],
        compiler_params=_params("arbitrary", "arbitrary", "arbitrary"),
        name="moe",
    )(h2, comb, x1, g2, w_gate_e, w_up_e, w_down_e, *shared, gpost)


def _qlat_kernel(q_ref, wuk_ref, o_ref):
    for hd in range(A_HEADS):
        qn = q_ref[:, hd * HEAD_PAD:hd * HEAD_PAD + QK_NOPE]
        o_ref[hd] = _dot_nt(qn, wuk_ref[hd]).astype(BF16)


def _qlat(q, wuk_heads):
    nb = q.shape[0]
    return pl.pallas_call(
        _qlat_kernel,
        in_specs=[_const_spec(q), _const_spec(wuk_heads)],
        out_specs=pl.BlockSpec((A_HEADS, nb, KV_LORA), lambda: (0, 0, 0)),
        out_shape=jax.ShapeDtypeStruct((A_HEADS, nb, KV_LORA), BF16),
        grid=(),
        name="q_latent",
    )(q, wuk_heads)


def _decode_kernel(pt_ref, ql_ref, qp_ref, cn_ref, kn_ref, ckv_hbm, kpet_hbm, o_ref,
                   cbuf, kbuf, sem, *, pages_per_chunk, n_chunks):
    b = pl.program_id(0)
    nb = pl.num_programs(0)
    cp = pages_per_chunk
    halves = 2 if cp % 2 == 0 else 1
    hp = cp // halves

    def copies(bb, chunk, slot):
        out = []
        for p in range(cp):
            page = pt_ref[bb, chunk * cp + p]
            out.append(pltpu.make_async_copy(ckv_hbm.at[page], cbuf.at[slot, p], sem.at[0, slot]))
            out.append(pltpu.make_async_copy(kpet_hbm.at[page], kbuf.at[slot, p], sem.at[1, slot]))
        return out

    @pl.when(b == 0)
    def _():
        for cpy in copies(0, 0, 0):
            cpy.start()

    ql = ql_ref[0]
    qp = qp_ref[0]
    c_new = cn_ref[0]
    k_new = kn_ref[0]
    s_new = (jnp.sum(ql.astype(F32) * c_new, axis=-1, keepdims=True)
             + jnp.sum(qp.astype(F32) * k_new, axis=-1, keepdims=True)) * ATTN_SCALE
    m0 = s_new
    l0 = jnp.ones((A_HEADS, 1), F32)
    acc0 = jnp.broadcast_to(c_new, (A_HEADS, KV_LORA))

    def body(chunk, carry):
        m_prev, l_prev, acc = carry
        slot = lax.rem(b * n_chunks + chunk, 2)

        @pl.when(chunk + 1 < n_chunks)
        def _():
            for cpy in copies(b, chunk + 1, 1 - slot):
                cpy.start()

        @pl.when(jnp.logical_and(chunk + 1 == n_chunks, b + 1 < nb))
        def _():
            for cpy in copies(b + 1, 0, 1 - slot):
                cpy.start()

        for cpy in copies(b, chunk, slot):
            cpy.wait()
        for hf in range(halves):
            kc = cbuf[slot, hf * hp:(hf + 1) * hp].reshape(hp * PAGE_SIZE, KV_LORA).astype(BF16)
            kt = jnp.concatenate([kbuf[slot, hf * hp + p] for p in range(hp)], axis=1).astype(BF16)
            s = (_dot_nt(ql, kc) + _dot(qp, kt)) * ATTN_SCALE
            m_new = jnp.maximum(m_prev, jnp.max(s, axis=-1, keepdims=True))
            alpha = jnp.exp(m_prev - m_new)
            p = jnp.exp(s - m_new)
            l_prev = alpha * l_prev + jnp.sum(p, axis=-1, keepdims=True)
            acc = alpha * acc + _dot(p.astype(BF16), kc)
            m_prev = m_new
        return m_prev, l_prev, acc

    _, l_fin, acc = lax.fori_loop(0, n_chunks, body, (m0, l0, acc0))
    o_ref[0] = acc / l_fin


def _decode_attention(page_table, qlat, qpe, ckv_new, kpe_new, cache_ckv, cache_kpe_t):
    nb, n_pages = page_table.shape
    cp = min(16, n_pages)
    n_chunks = n_pages // cp
    grid_spec = pltpu.PrefetchScalarGridSpec(
        num_scalar_prefetch=1,
        grid=(nb,),
        in_specs=[pl.BlockSpec((1, A_HEADS, KV_LORA), lambda i, pt: (i, 0, 0)),
                  pl.BlockSpec((1, A_HEADS, QK_ROPE), lambda i, pt: (i, 0, 0)),
                  pl.BlockSpec((1, 1, KV_LORA), lambda i, pt: (i, 0, 0)),
                  pl.BlockSpec((1, 1, QK_ROPE), lambda i, pt: (i, 0, 0)),
                  pl.BlockSpec(memory_space=pl.ANY),
                  pl.BlockSpec(memory_space=pl.ANY)],
        out_specs=pl.BlockSpec((1, A_HEADS, KV_LORA), lambda i, pt: (i, 0, 0)),
        scratch_shapes=[pltpu.VMEM((2, cp, PAGE_SIZE, KV_LORA), F32),
                        pltpu.VMEM((2, cp, QK_ROPE, PAGE_SIZE), F32),
                        pltpu.SemaphoreType.DMA((2, 2))],
    )
    return pl.pallas_call(
        functools.partial(_decode_kernel, pages_per_chunk=cp, n_chunks=n_chunks),
        grid_spec=grid_spec,
        out_shape=jax.ShapeDtypeStruct((nb, A_HEADS, KV_LORA), F32),
        compiler_params=_params("arbitrary"),
        name="decode_attention",
    )(page_table, qlat, qpe, ckv_new, kpe_new, cache_ckv, cache_kpe_t)


def _vup_kernel(ol_ref, wuv_ref, o_ref):
    for hd in range(A_HEADS):
        o_ref[:, hd * V_DIM:(hd + 1) * V_DIM] = _dot(ol_ref[hd].astype(BF16), wuv_ref[hd]).astype(BF16)


def _value_up(o_lat_heads, wuv_heads):
    nb = o_lat_heads.shape[1]
    return pl.pallas_call(
        _vup_kernel,
        in_specs=[_const_spec(o_lat_heads), _const_spec(wuv_heads)],
        out_specs=pl.BlockSpec((nb, A_HEADS * V_DIM), lambda: (0, 0)),
        out_shape=jax.ShapeDtypeStruct((nb, A_HEADS * V_DIM), BF16),
        grid=(),
        name="value_up",
    )(o_lat_heads, wuv_heads)


def _ssd_step_kernel(xbc_ref, cs_ref, z_ref, dt_ref, st_ref, cw_ref, cb_ref, dtb_ref, na_ref,
                     dsk_ref, gs_ref, y_o, st_o, y_ref):
    conv = cb_ref[...] + cw_ref[CONV_W - 1:CONV_W, :] * xbc_ref[0]
    for j in range(CONV_W - 1):
        conv = conv + cw_ref[j:j + 1, :] * cs_ref[0, j:j + 1, :]
    xbc = _silu(conv)
    dt = _softplus(dt_ref[0] + dtb_ref[...])
    dec = jnp.exp(dt * na_ref[...])
    hi, mid, lo = (part.astype(F32) for part in _split3(xbc[:, :D_INNER]))
    r8 = lax.broadcasted_iota(jnp.int32, (8, D_INNER), 0)
    x8 = jnp.where(r8 == 0, hi, jnp.where(r8 == 1, mid, jnp.where(r8 == 2, lo, 0.0))).astype(BF16)
    ones8 = (lax.broadcasted_iota(jnp.int32, (8, D_STATE), 0) < 3).astype(BF16)
    x_col = _dot_tn(x8, ones8)
    for hd in range(S_HEADS):
        g = hd // GROUP_HEADS
        hsl = slice(hd * S_HEAD_DIM, (hd + 1) * S_HEAD_DIM)
        bm = xbc[:, D_INNER + g * D_STATE:D_INNER + (g + 1) * D_STATE]
        cm = xbc[:, D_INNER + (S_GROUPS + g) * D_STATE:D_INNER + (S_GROUPS + g + 1) * D_STATE]
        new = st_ref[0, hd] * dec[:, hd:hd + 1] + x_col[hsl, :] * (bm * dt[:, hd:hd + 1])
        st_o[0, hd] = new
        cb8 = jnp.broadcast_to(cm, (8, D_STATE)).astype(BF16)
        y_ref[:, hsl] = _dot_nt(cb8, new.astype(BF16))
    y = y_ref[0:1, :] + dsk_ref[...] * xbc[:, :D_INNER]
    y = y * _silu(z_ref[0])
    for g in range(S_GROUPS):
        gsl = slice(g * GROUP_CH, (g + 1) * GROUP_CH)
        y_o[0, :, gsl] = _rms(y[:, gsl], gs_ref[:, gsl]).astype(BF16)


def _ssd_sample(xbc, conv_state, z, dt, ssm_state, conv_w, conv_b, dtb_row, na_row, dskip, g_ssd):
    nb = xbc.shape[0]
    consts = (conv_w, conv_b, dtb_row, na_row, dskip, g_ssd)
    st_spec = pl.BlockSpec((1, S_HEADS, S_HEAD_DIM, D_STATE), lambda i: (i, 0, 0, 0))
    return pl.pallas_call(
        _ssd_step_kernel,
        grid=(nb,),
        in_specs=[pl.BlockSpec((1, 1, CONV_CH), lambda i: (i, 0, 0)),
                  pl.BlockSpec((1, CONV_W - 1, CONV_CH), lambda i: (i, 0, 0)),
                  pl.BlockSpec((1, 1, D_INNER), lambda i: (i, 0, 0)),
                  pl.BlockSpec((1, 1, LANES), lambda i: (i, 0, 0)),
                  st_spec] + [_const_spec(a) for a in consts],
        out_specs=[pl.BlockSpec((1, 1, D_INNER), lambda i: (i, 0, 0)), st_spec],
        out_shape=[jax.ShapeDtypeStruct((nb, 1, D_INNER), BF16),
                   jax.ShapeDtypeStruct(ssm_state.shape, F32)],
        scratch_shapes=[pltpu.VMEM((8, D_INNER), F32)],
        compiler_params=_params("arbitrary"),
        name="ssd_sample",
    )(xbc, conv_state, z, dt, ssm_state, *consts)


def _rot_half(w):
    half = QK_ROPE // 2
    return jnp.concatenate([-w[..., half:], w[..., :half]], axis=-1)


def _pad_cols(w, start, total):
    return jnp.pad(w, ((0, 0), (start, total - start - w.shape[1])))


def _head_pad(w_nope, w_rope):
    k = w_nope.shape[0]
    pad = jnp.zeros((k, A_HEADS, HEAD_PAD - QK_NOPE - QK_ROPE), w_nope.dtype)
    return jnp.concatenate([w_nope, w_rope, pad], axis=-1).reshape(k, A_HEADS * HEAD_PAD)


def _rope_tables(pos):
    half = QK_ROPE // 2
    inv = ROPE_BASE ** (-jnp.arange(half, dtype=F32) / half)
    ang = pos.astype(F32)[:, None] * inv[None, :]
    cos, sin = jnp.cos(ang), jnp.sin(ang)
    n = pos.shape[0]
    ctab = jnp.concatenate([jnp.ones((n, QK_NOPE), F32), cos, cos,
                            jnp.zeros((n, HEAD_PAD - QK_NOPE - QK_ROPE), F32)], axis=1)
    stab = jnp.concatenate([jnp.zeros((n, QK_NOPE), F32), sin, sin,
                            jnp.zeros((n, HEAD_PAD - QK_NOPE - QK_ROPE), F32)], axis=1)
    return ctab, stab


def _pick_tile(n, target):
    t = min(n, target)
    assert n % t == 0, (n, t)
    return t


def kernel(x_prompt, x_sample, cache_ckv, cache_kpe, state_conv, state_ssm, page_table, c_prompt, c_sample, w_ada, b_ada, g_pre_mix, g_post_mix, g_pre_ffn, g_post_ffn, w_in, g_q_a, w_q_b, g_kv_a, w_uk, w_uv, w_o_attn, conv_w, conv_b, dt_bias, a_log, d_skip, g_ssd, w_o_ssd, w_out, w_router, b_router, w_gate_e, w_up_e, w_down_e, w_gate_s, w_up_s, w_down_s):
    bp, sp, d = x_prompt.shape
    nb, ds, _ = x_sample.shape
    depth = w_in.shape[0]
    assert depth == 1 and ds == 1 and d == D_MODEL
    n_pages = page_table.shape[1]
    past_len = n_pages * PAGE_SIZE
    lyr = 0

    offs = [0]
    for n in IN_SPLITS:
        offs.append(offs[-1] + n)
    win = w_in[lyr]
    w_qa, w_kva, w_kpe, w_z, w_xbc, w_dt, w_ga, w_gb = (win[:, offs[i]:offs[i + 1]] for i in range(8))
    wqb = w_q_b[lyr].reshape(Q_LORA, A_HEADS, QK_NOPE + QK_ROPE)
    wq_pad = _head_pad(wqb[..., :QK_NOPE], wqb[..., QK_NOPE:])
    wq_rot = _head_pad(jnp.zeros_like(wqb[..., :QK_NOPE]), _rot_half(wqb[..., QK_NOPE:]))
    wuk_pad = _head_pad(w_uk[lyr], jnp.zeros((KV_LORA, A_HEADS, QK_ROPE), F32))
    premix_w = [
        w_qa.astype(BF16), w_kva.astype(BF16),
        _pad_cols(w_kpe, QK_NOPE, LANES).astype(BF16),
        _pad_cols(_rot_half(w_kpe), QK_NOPE, LANES).astype(BF16),
        w_z.astype(BF16), w_xbc.astype(BF16),
        _pad_cols(w_dt, 0, LANES).astype(BF16), w_dt.T.astype(BF16),
        w_ga.astype(BF16), w_gb.astype(BF16),
        g_q_a[lyr][None, :], wq_pad.astype(BF16), wq_rot.astype(BF16),
        g_kv_a[lyr][None, :], wuk_pad.astype(BF16),
        w_uv[lyr].reshape(KV_LORA, A_HEADS * V_DIM).astype(BF16),
    ]
    merge_w = [w_o_attn[lyr].astype(BF16), w_o_ssd[lyr].astype(BF16), w_out[lyr].astype(BF16),
               g_post_mix[lyr][None, :], g_pre_ffn[lyr][None, :]]
    shared_w = [w_gate_s[lyr].astype(BF16), w_up_s[lyr].astype(BF16), w_down_s[lyr].astype(BF16)]
    w_router_t = w_router[lyr].T.astype(BF16)
    b_router_col = b_router[lyr][:, None]
    g_pre = g_pre_mix[lyr][None, :]
    g_post_ffn_row = g_post_ffn[lyr][None, :]
    cw = conv_w[lyr]
    cb = conv_b[lyr][None, :]
    neg_a = -jnp.exp(a_log[lyr].astype(F32))
    dtb_row = jnp.pad(dt_bias[lyr], (0, LANES - S_HEADS))[None, :]
    na_row = jnp.pad(neg_a, (0, LANES - S_HEADS))[None, :]
    dtb_col = dt_bias[lyr][:, None]
    na_col = neg_a[:, None]
    dskip_row = jnp.repeat(d_skip[lyr].astype(F32), S_HEAD_DIM)[None, :]
    gssd_row = g_ssd[lyr][None, :]

    n_mod_rows = bp + nb
    pad_rows = (-n_mod_rows) % 16
    c_all = jnp.concatenate([c_prompt, c_sample, jnp.zeros((pad_rows, d), F32)], axis=0)
    mod = _adaln(c_all, w_ada[lyr], b_ada[lyr][None, :])
    mods = [mod[:, i * d:(i + 1) * d] for i in range(6)]
    mods_p = [m[:bp][:, None, :] for m in mods]
    mods_s = [m[bp:bp + nb][None, :, :] for m in mods]

    ctab_p, stab_p = _rope_tables(jnp.arange(sp))
    tm = _pick_tile(sp, 256)
    (q_p, k_p, v_p, ckv_p, kpe_pad_p, z_p, xbc_p, dt_p, dtt_p, ga_p, gb_p) = _premix(
        x_prompt, mods_p[0], mods_p[1], g_pre, ctab_p, stab_p, premix_w, tm)
    o_attn_p = _flash_attention(q_p, k_p, v_p, _pick_tile(sp, 512), 2)
    y_ssd_p, ssm_t_p = _ssd_prompt(xbc_p, z_p, dt_p, dtt_p, cw, cb, dtb_row, dtb_col, na_row, na_col,
                                   dskip_row, gssd_row)
    x1_p, h2_p = _merge(x_prompt, o_attn_p, y_ssd_p, ga_p, gb_p, mods_p[2], mods_p[3], mods_p[4],
                        merge_w, tm)
    tb = _pick_tile(sp, MOE_BLOCK)
    info_p, slotk_p, wk_p, pc_p = _router(h2_p, w_router_t, b_router_col, tb)
    y_prompt = _sparse_moe(h2_p, info_p, slotk_p, wk_p, pc_p, x1_p, mods_p[5],
                           w_gate_e[lyr], w_up_e[lyr], w_down_e[lyr], shared_w, g_post_ffn_row, tb)

    xs = x_sample.reshape(1, nb, d)
    ctab_s, stab_s = _rope_tables(past_len + jnp.arange(ds))
    (q_s, _, _, ckv_s, kpe_pad_s, z_s, xbc_s, dt_s, _, ga_s, gb_s) = _premix(
        xs, mods_s[0], mods_s[1], g_pre, ctab_s, stab_s, premix_w, nb)
    kpe_s = kpe_pad_s[0, :, QK_NOPE:QK_NOPE + QK_ROPE]
    wuk_heads = jnp.transpose(w_uk[lyr], (1, 0, 2)).astype(BF16)
    wuv_heads = jnp.transpose(w_uv[lyr], (1, 0, 2)).astype(BF16)
    qlat = jnp.transpose(_qlat(q_s[0], wuk_heads), (1, 0, 2))
    qpe = q_s[0].reshape(nb, A_HEADS, HEAD_PAD)[:, :, QK_NOPE:QK_NOPE + QK_ROPE]
    o_lat = _decode_attention(page_table, qlat, qpe,
                              ckv_s.reshape(nb, 1, KV_LORA), kpe_s.reshape(nb, 1, QK_ROPE),
                              cache_ckv.reshape(cache_ckv.shape[1:]),
                              jnp.swapaxes(cache_kpe.reshape(cache_kpe.shape[1:]), 1, 2))
    o_attn_s = _value_up(jnp.transpose(o_lat, (1, 0, 2)), wuv_heads)[None]
    y_ssd_s, ssm_s = _ssd_sample(xbc_s.reshape(nb, 1, CONV_CH), state_conv[lyr],
                                 z_s.reshape(nb, 1, D_INNER), dt_s.reshape(nb, 1, LANES),
                                 state_ssm.reshape(state_ssm.shape[1:]), cw, cb, dtb_row, na_row,
                                 dskip_row, gssd_row)
    x1_s, h2_s = _merge(xs, o_attn_s, y_ssd_s.reshape(1, nb, D_INNER), ga_s, gb_s,
                        mods_s[2], mods_s[3], mods_s[4], merge_w, nb)
    comb_s = _router(h2_s, w_router_t, b_router_col, nb)[0]
    y_s = _moe(h2_s, comb_s, x1_s, mods_s[5], w_gate_e[lyr], w_up_e[lyr], w_down_e[lyr],
               shared_w, g_post_ffn_row, nb)

    kpe_p = kpe_pad_p[:, :, QK_NOPE:QK_NOPE + QK_ROPE]
    conv_p = xbc_p[:, sp - (CONV_W - 1):, :]
    ssm_p = jnp.transpose(ssm_t_p.reshape(bp, D_STATE, S_HEADS, S_HEAD_DIM), (0, 2, 3, 1))
    conv_s = jnp.concatenate([state_conv[lyr][:, 1:, :], xbc_s.reshape(nb, 1, CONV_CH)], axis=1)
    return (y_prompt, y_s.reshape(nb, ds, d),
            ckv_p[None], kpe_p[None], conv_p[None], ssm_p[None].astype(x_prompt.dtype),
            ckv_s.reshape(1, nb, ds, KV_LORA), kpe_s.reshape(1, nb, ds, QK_ROPE),
            conv_s[None], ssm_s[None].astype(x_sample.dtype))
```

```python
import functools

import jax
import jax.numpy as jnp
from jax import lax
from jax.experimental import pallas as pl
from jax.experimental.pallas import tpu as pltpu

F32 = jnp.float32
BF16 = jnp.bfloat16

D_MODEL = 1024
PAGE_SIZE = 128
A_HEADS = 8
QK_NOPE = 64
QK_ROPE = 32
V_DIM = 64
Q_LORA = 384
KV_LORA = 256
ROPE_BASE = 10000.0
ATTN_SCALE = (QK_NOPE + QK_ROPE) ** -0.5
S_HEADS = 16
S_HEAD_DIM = 64
D_INNER = S_HEADS * S_HEAD_DIM
S_GROUPS = 2
D_STATE = 128
CONV_W = 4
CONV_CH = D_INNER + 2 * S_GROUPS * D_STATE
CHUNK = 128
N_EXPERTS = 64
TOP_K = 6
N_EXP_GROUPS = 8
TOPK_GROUPS = 4
F_EXPERT = 256
ROUTE_SCALE = 2.5
EPS = 1e-6
IN_SPLITS = (Q_LORA, KV_LORA, QK_ROPE, D_INNER, CONV_CH, S_HEADS, D_MODEL, D_MODEL)

LANES = 128
HEAD_PAD = LANES
GROUP_HEADS = S_HEADS // S_GROUPS
GROUP_CH = D_INNER // S_GROUPS
EXP_PER_GROUP = N_EXPERTS // N_EXP_GROUPS
MOE_BLOCK = 256
CHUNK_ROWS = 16
TILE_CHUNKS = 16
ROW_TILE = CHUNK_ROWS * TILE_CHUNKS
DECODE_RING = 3
VMEM_LIMIT_BYTES = 56 * 1024 * 1024

NEG_INF = float("-inf")
LOG2_E = 1.4426950408889634


def _params(*semantics):
    return pltpu.CompilerParams(dimension_semantics=semantics, vmem_limit_bytes=VMEM_LIMIT_BYTES)


def _const_spec(arr):
    nd = arr.ndim
    return pl.BlockSpec(arr.shape, lambda *_: (0,) * nd)


def _dot(a, b):
    return jnp.dot(a, b, preferred_element_type=F32)


def _dot_nt(a, b):
    return lax.dot_general(a, b, (((1,), (1,)), ((), ())), preferred_element_type=F32)


def _dot_tn(a, b):
    return lax.dot_general(a, b, (((0,), (0,)), ((), ())), preferred_element_type=F32)


def _rms(x, g):
    return x * lax.rsqrt(jnp.mean(x * x, axis=-1, keepdims=True) + EPS) * g


def _silu(x):
    return x * (1.0 / (1.0 + jnp.exp(-x)))


def _sigmoid(x):
    return 1.0 / (1.0 + jnp.exp(-x))


def _softplus(x):
    return jnp.maximum(x, 0.0) + jnp.log(1.0 + jnp.exp(-jnp.abs(x)))


def _split3(x):
    hi = x.astype(BF16)
    r1 = x - hi.astype(F32)
    mid = r1.astype(BF16)
    lo = (r1 - mid.astype(F32)).astype(BF16)
    return hi, mid, lo


def _adaln_kernel(c_ref, w_ref, b_ref, o_ref):
    c = _silu(c_ref[...]).astype(BF16)
    o_ref[...] = _dot(c, w_ref[...].astype(BF16)) + b_ref[...]


def _adaln(c_all, w_ada, b_ada):
    rows, d = c_all.shape
    n = w_ada.shape[1]
    tn = 512
    return pl.pallas_call(
        _adaln_kernel,
        grid=(n // tn,),
        in_specs=[pl.BlockSpec((rows, d), lambda j: (0, 0)),
                  pl.BlockSpec((d, tn), lambda j: (0, j)),
                  pl.BlockSpec((1, tn), lambda j: (0, j))],
        out_specs=pl.BlockSpec((rows, tn), lambda j: (0, j)),
        out_shape=jax.ShapeDtypeStruct((rows, n), F32),
        compiler_params=_params("arbitrary"),
        name="adaln",
    )(c_all, w_ada, b_ada)


def _premix_kernel(x_ref, sh_ref, sc_ref, g_ref, ct_ref, st_ref,
                   wqa, wkva, wkpe, wkper, wz, wxbc, wdt, wdtt, wga, wgb,
                   gqa, wq, wqr, gkv, wuk, wuv,
                   q_o, k_o, v_o, ckv_o, kpe_o, z_o, xbc_o, dt_o, dtt_o, ga_o, gb_o):
    x = x_ref[0]
    h = _rms(x, g_ref[...]) * (1.0 + sc_ref[0]) + sh_ref[0]
    hb = h.astype(BF16)
    ct = ct_ref[...]
    st = st_ref[...]

    qn = _rms(_dot(hb, wqa[...]), gqa[...]).astype(BF16)
    q = _dot(qn, wq[...])
    qr = _dot(qn, wqr[...])
    for hd in range(A_HEADS):
        sl = slice(hd * HEAD_PAD, (hd + 1) * HEAD_PAD)
        q_o[0, :, sl] = (q[:, sl] * ct + qr[:, sl] * st).astype(BF16)

    ckv = _rms(_dot(hb, wkva[...]), gkv[...])
    ckv_o[0] = ckv
    cb = ckv.astype(BF16)
    kpe = _dot(hb, wkpe[...]) * ct + _dot(hb, wkper[...]) * st
    kpe_o[0] = kpe
    kn = _dot(cb, wuk[...])
    for hd in range(A_HEADS):
        sl = slice(hd * HEAD_PAD, (hd + 1) * HEAD_PAD)
        k_o[0, :, sl] = (kn[:, sl] + kpe).astype(BF16)
    v_o[0] = _dot(cb, wuv[...]).astype(BF16)

    z_o[0] = _dot(hb, wz[...])
    xbc_o[0] = _dot(hb, wxbc[...])
    dt_o[0] = _dot(hb, wdt[...])
    dtt_o[0] = _dot_nt(wdtt[...], hb)
    ga_o[0] = _dot(hb, wga[...])
    gb_o[0] = _dot(hb, wgb[...])


def _premix(x, sh, sc, g, ctab, stab, weights, tm):
    b, s, d = x.shape
    per_row_mod = sh.shape[1] != 1
    per_row_tab = ctab.shape[0] != 1
    mod_spec = (pl.BlockSpec((1, tm, d), lambda i, j: (i, j, 0)) if per_row_mod
                else pl.BlockSpec((1, 1, d), lambda i, j: (i, 0, 0)))
    tab_spec = (pl.BlockSpec((tm, LANES), lambda i, j: (j, 0)) if per_row_tab
                else pl.BlockSpec((1, LANES), lambda i, j: (0, 0)))

    def tok(n):
        return pl.BlockSpec((1, tm, n), lambda i, j: (i, j, 0))

    out_cols = [(A_HEADS * HEAD_PAD, BF16), (A_HEADS * HEAD_PAD, BF16), (A_HEADS * V_DIM, BF16),
                (KV_LORA, F32), (LANES, F32), (D_INNER, F32), (CONV_CH, F32), (LANES, F32)]
    out_shape = [jax.ShapeDtypeStruct((b, s, n), dt) for n, dt in out_cols]
    out_specs = [tok(n) for n, _ in out_cols]
    out_shape.append(jax.ShapeDtypeStruct((b, S_HEADS, s), F32))
    out_specs.append(pl.BlockSpec((1, S_HEADS, tm), lambda i, j: (i, 0, j)))
    out_shape += [jax.ShapeDtypeStruct((b, s, D_MODEL), F32)] * 2
    out_specs += [tok(D_MODEL)] * 2
    order = [out_shape[i] for i in (0, 1, 2, 3, 4, 5, 6, 7, 8, 9, 10)]
    return pl.pallas_call(
        _premix_kernel,
        grid=(b, s // tm),
        in_specs=[tok(d), mod_spec, mod_spec, _const_spec(g), tab_spec, tab_spec]
        + [_const_spec(w) for w in weights],
        out_specs=out_specs,
        out_shape=order,
        compiler_params=_params("arbitrary", "arbitrary"),
        name="premix",
    )(x, sh, sc, g, ctab, stab, *weights)


def _flash_kernel(q_ref, k_ref, v_ref, o_ref, m_ref, l_ref, acc_ref, *, tq, heads):
    qi = pl.program_id(2)
    m_ref[...] = jnp.full(m_ref.shape, NEG_INF, F32)
    l_ref[...] = jnp.zeros(l_ref.shape, F32)
    acc_ref[...] = jnp.zeros(acc_ref.shape, F32)
    row = lax.broadcasted_iota(jnp.int32, (tq, tq), 0)
    col = lax.broadcasted_iota(jnp.int32, (tq, tq), 1)
    c2 = ATTN_SCALE * LOG2_E

    def tile(kj, on_diagonal):
        koff = pl.multiple_of(kj * tq, tq)
        for hh in range(heads):
            sl = slice(hh * HEAD_PAD, (hh + 1) * HEAD_PAD)
            vsl = slice((hh // 2) * 2 * V_DIM, (hh // 2 + 1) * 2 * V_DIM)
            s = _dot_nt(q_ref[0, :, sl], k_ref[0, pl.ds(koff, tq), sl])
            if on_diagonal:
                s = jnp.where(col <= row, s, NEG_INF)
            m_prev = m_ref[hh]
            m_new = jnp.maximum(m_prev, jnp.max(s, axis=-1, keepdims=True))
            alpha = jnp.exp2((m_prev - m_new) * c2)
            p_parts = [jnp.exp2((s[:, j * LANES:(j + 1) * LANES] - m_new) * c2)
                       for j in range(tq // LANES)]
            l_ref[hh] = alpha * l_ref[hh] + sum(p_parts)
            p = jnp.concatenate(p_parts, axis=1).astype(BF16)
            acc_ref[hh] = alpha * acc_ref[hh] + _dot(p, v_ref[0, pl.ds(koff, tq), vsl])
            m_ref[hh] = m_new

    def body(kj, carry):
        tile(kj, False)
        return carry

    lax.fori_loop(0, qi, body, 0)
    tile(qi, True)
    lane = lax.broadcasted_iota(jnp.int32, (tq, 2 * V_DIM), 1)
    for pair in range(heads // 2):
        o0 = acc_ref[2 * pair] / jnp.sum(l_ref[2 * pair], axis=-1, keepdims=True)
        o1 = acc_ref[2 * pair + 1] / jnp.sum(l_ref[2 * pair + 1], axis=-1, keepdims=True)
        o_ref[0, :, pair * 2 * V_DIM:(pair + 1) * 2 * V_DIM] = jnp.where(lane < V_DIM, o0, o1).astype(BF16)


def _flash_attention(q, k, v, tq, heads):
    b, s, _ = q.shape
    return pl.pallas_call(
        functools.partial(_flash_kernel, tq=tq, heads=heads),
        grid=(b, A_HEADS // heads, s // tq),
        in_specs=[pl.BlockSpec((1, tq, heads * HEAD_PAD), lambda bi, hg, qi: (bi, qi, hg)),
                  pl.BlockSpec((1, s, heads * HEAD_PAD), lambda bi, hg, qi: (bi, 0, hg)),
                  pl.BlockSpec((1, s, heads * V_DIM), lambda bi, hg, qi: (bi, 0, hg))],
        out_specs=pl.BlockSpec((1, tq, heads * V_DIM), lambda bi, hg, qi: (bi, qi, hg)),
        out_shape=jax.ShapeDtypeStruct((b, s, A_HEADS * V_DIM), BF16),
        scratch_shapes=[pltpu.VMEM((heads, tq, LANES), F32), pltpu.VMEM((heads, tq, LANES), F32),
                        pltpu.VMEM((heads, tq, 2 * V_DIM), F32)],
        compiler_params=_params("arbitrary", "arbitrary", "arbitrary"),
        name="flash_attention",
    )(q, k, v)


def _ssd_chunk_kernel(xbc_ref, z_ref, dt_ref, dtt_ref, cw_ref, cb_ref, dtb_ref, dtbc_ref,
                      na_ref, nac_ref, dsk_ref, gs_ref, y_o, st_o, xc_ref, state_ref, y_ref):
    c = pl.program_id(1)
    nc = pl.num_programs(1)

    @pl.when(c == 0)
    def _():
        xc_ref[0:8, :] = jnp.zeros((8, CONV_CH), F32)
        state_ref[...] = jnp.zeros(state_ref.shape, F32)

    xc_ref[8:8 + CHUNK, :] = xbc_ref[0]
    conv = cb_ref[...] + cw_ref[CONV_W - 1:CONV_W, :] * xc_ref[8:8 + CHUNK, :]
    for j in range(1, CONV_W):
        conv = conv + cw_ref[CONV_W - 1 - j:CONV_W - j, :] * xc_ref[8 - j:8 - j + CHUNK, :]
    xc_ref[0:8, :] = xc_ref[CHUNK:CHUNK + 8, :]
    xbc = _silu(conv)

    dt_c = _softplus(dt_ref[0] + dtb_ref[...])
    a_c = dt_c * na_ref[...]
    dt_r = _softplus(dtt_ref[0] + dtbc_ref[...])
    a_r = dt_r * nac_ref[...]
    li = lax.broadcasted_iota(jnp.int32, (CHUNK, CHUNK), 0)
    si = lax.broadcasted_iota(jnp.int32, (CHUNK, CHUNK), 1)
    lower = (si <= li)
    tri = lower.astype(BF16)
    acum_c = sum(_dot(tri, part) for part in _split3(a_c))
    acum_r = sum(_dot_nt(part, tri) for part in _split3(a_r))
    a_last = acum_c[CHUNK - 1:CHUNK, :]
    dec_end = jnp.exp(a_last - acum_c)
    exp_ac = jnp.exp(acum_c)
    dec_all = jnp.exp(a_last)

    for g in range(S_GROUPS):
        bm = xbc[:, D_INNER + g * D_STATE:D_INNER + (g + 1) * D_STATE]
        cm = xbc[:, D_INNER + (S_GROUPS + g) * D_STATE:D_INNER + (S_GROUPS + g + 1) * D_STATE]
        bmb = bm.astype(BF16)
        cmb = cm.astype(BF16)
        gmat = _dot_nt(cmb, bmb)
        gsl = slice(g * GROUP_CH, (g + 1) * GROUP_CH)
        y_off = _dot(cmb, state_ref[:, gsl].astype(BF16))
        xde_parts = []
        for hh in range(GROUP_HEADS):
            hd = g * GROUP_HEADS + hh
            hsl = slice(hd * S_HEAD_DIM, (hd + 1) * S_HEAD_DIM)
            xs = xbc[:, hsl]
            xdt = xs * dt_c[:, hd:hd + 1]
            seg = acum_c[:, hd:hd + 1] - acum_r[hd:hd + 1, :]
            lmat = jnp.exp(jnp.where(lower, seg, NEG_INF))
            y_diag = _dot((gmat * lmat).astype(BF16), xdt.astype(BF16))
            yo = y_off[:, hh * S_HEAD_DIM:(hh + 1) * S_HEAD_DIM] * exp_ac[:, hd:hd + 1]
            y_ref[:, hsl] = y_diag + yo
            xde_parts.append((xdt * dec_end[:, hd:hd + 1]).astype(BF16))
        xde = jnp.concatenate(xde_parts, axis=1)
        new_states = _dot_tn(bmb, xde)
        for hh in range(GROUP_HEADS):
            hd = g * GROUP_HEADS + hh
            hsl = slice(hd * S_HEAD_DIM, (hd + 1) * S_HEAD_DIM)
            lsl = slice(hh * S_HEAD_DIM, (hh + 1) * S_HEAD_DIM)
            state_ref[:, hsl] = state_ref[:, hsl] * dec_all[:, hd:hd + 1] + new_states[:, lsl]

    y = y_ref[...] + dsk_ref[...] * xbc[:, :D_INNER]
    y = y * _silu(z_ref[0])
    for g in range(S_GROUPS):
        gsl = slice(g * GROUP_CH, (g + 1) * GROUP_CH)
        y_o[0, :, gsl] = _rms(y[:, gsl], gs_ref[:, gsl]).astype(BF16)

    @pl.when(c == nc - 1)
    def _():
        st_o[0] = state_ref[...]


def _ssd_prompt(xbc, z, dt, dtt, conv_w, conv_b, dtb_row, dtb_col, na_row, na_col, dskip, g_ssd):
    b, s, _ = xbc.shape
    nc = s // CHUNK
    consts = (conv_w, conv_b, dtb_row, dtb_col, na_row, na_col, dskip, g_ssd)
    return pl.pallas_call(
        _ssd_chunk_kernel,
        grid=(b, nc),
        in_specs=[pl.BlockSpec((1, CHUNK, CONV_CH), lambda i, j: (i, j, 0)),
                  pl.BlockSpec((1, CHUNK, D_INNER), lambda i, j: (i, j, 0)),
                  pl.BlockSpec((1, CHUNK, LANES), lambda i, j: (i, j, 0)),
                  pl.BlockSpec((1, S_HEADS, CHUNK), lambda i, j: (i, 0, j))]
        + [_const_spec(a) for a in consts],
        out_specs=[pl.BlockSpec((1, CHUNK, D_INNER), lambda i, j: (i, j, 0)),
                   pl.BlockSpec((1, D_STATE, D_INNER), lambda i, j: (i, 0, 0))],
        out_shape=[jax.ShapeDtypeStruct((b, s, D_INNER), BF16),
                   jax.ShapeDtypeStruct((b, D_STATE, D_INNER), F32)],
        scratch_shapes=[pltpu.VMEM((CHUNK + 8, CONV_CH), F32),
                        pltpu.VMEM((D_STATE, D_INNER), F32),
                        pltpu.VMEM((CHUNK, D_INNER), F32)],
        compiler_params=_params("arbitrary", "arbitrary"),
        name="ssd_prompt",
    )(xbc, z, dt, dtt, *consts)


def _merge_kernel(x_ref, oa_ref, ys_ref, ga_ref, gb_ref, g1_ref, sh2_ref, sc2_ref,
                  woa, wos, wout, gpost, gpre, x1_o, h2_o):
    o_attn = _dot(oa_ref[0], woa[...])
    o_ssd = _dot(ys_ref[0], wos[...])
    merged = _sigmoid(ga_ref[0]) * o_attn + _sigmoid(gb_ref[0]) * o_ssd
    m = _dot(merged.astype(BF16), wout[...])
    x1 = x_ref[0] + g1_ref[0] * _rms(m, gpost[...])
    x1_o[0] = x1
    h2_o[0] = (_rms(x1, gpre[...]) * (1.0 + sc2_ref[0]) + sh2_ref[0]).astype(BF16)


def _merge(x, o_attn, y_ssd, ga, gb, g1, sh2, sc2, weights, tm):
    b, s, d = x.shape
    per_row_mod = g1.shape[1] != 1
    mod_spec = (pl.BlockSpec((1, tm, d), lambda i, j: (i, j, 0)) if per_row_mod
                else pl.BlockSpec((1, 1, d), lambda i, j: (i, 0, 0)))

    def tok(n):
        return pl.BlockSpec((1, tm, n), lambda i, j: (i, j, 0))

    return pl.pallas_call(
        _merge_kernel,
        grid=(b, s // tm),
        in_specs=[tok(d), tok(A_HEADS * V_DIM), tok(D_INNER), tok(d), tok(d),
                  mod_spec, mod_spec, mod_spec] + [_const_spec(w) for w in weights],
        out_specs=[tok(d), tok(d)],
        out_shape=[jax.ShapeDtypeStruct((b, s, d), F32), jax.ShapeDtypeStruct((b, s, d), BF16)],
        compiler_params=_params("arbitrary", "arbitrary"),
        name="merge",
    )(x, o_attn, y_ssd, ga, gb, g1, sh2, sc2, *weights)


def _first_max(vals, idx, big):
    m = vals[0]
    for v in vals[1:]:
        m = jnp.maximum(m, v)
    m = jnp.max(m, axis=0, keepdims=True)
    cand = [jnp.where(v == m, i, big) for v, i in zip(vals, idx)]
    a = cand[0]
    for cnd in cand[1:]:
        a = jnp.minimum(a, cnd)
    a = jnp.min(a, axis=0, keepdims=True)
    return m, a


def _router_kernel(h_ref, wt_ref, b_ref, o_ref, slot_o, w_o, cnt_o, *, tm):
    logits = _dot_nt(wt_ref[...], h_ref[0])
    scores = _sigmoid(logits)
    biased = scores + b_ref[...]
    sub = lax.broadcasted_iota(jnp.int32, (EXP_PER_GROUP, tm), 0)
    slabs = [biased[g * EXP_PER_GROUP:(g + 1) * EXP_PER_GROUP, :] for g in range(N_EXP_GROUPS)]
    big = jnp.int32(N_EXPERTS)

    gscore = []
    for g in range(N_EXP_GROUPS):
        m1, a1 = _first_max([slabs[g]], [sub], big)
        rest = jnp.where(sub == a1, NEG_INF, slabs[g])
        m2 = jnp.max(rest, axis=0, keepdims=True)
        gscore.append(m1 + m2)
    gs = jnp.full((N_EXP_GROUPS, tm), NEG_INF, F32)
    for g in range(N_EXP_GROUPS):
        gs = jnp.where(sub == g, gscore[g], gs)
    gsel = jnp.zeros((N_EXP_GROUPS, tm), F32)
    for _ in range(TOPK_GROUPS):
        _, a = _first_max([gs], [sub], big)
        hit = sub == a
        gsel = jnp.where(hit, 1.0, gsel)
        gs = jnp.where(hit, NEG_INF, gs)

    masked = [jnp.where(gsel[g:g + 1, :] > 0.5, slabs[g], NEG_INF) for g in range(N_EXP_GROUPS)]
    flat = [sub + g * EXP_PER_GROUP for g in range(N_EXP_GROUPS)]
    chosen = [jnp.zeros((EXP_PER_GROUP, tm), jnp.bool_) for _ in range(N_EXP_GROUPS)]
    picks = []
    for _ in range(TOP_K):
        _, a = _first_max(masked, flat, big)
        picks.append(a)
        for g in range(N_EXP_GROUPS):
            hit = flat[g] == a
            chosen[g] = jnp.logical_or(chosen[g], hit)
            masked[g] = jnp.where(hit, NEG_INF, masked[g])
    score_slabs = [scores[g * EXP_PER_GROUP:(g + 1) * EXP_PER_GROUP, :] for g in range(N_EXP_GROUPS)]
    w = [jnp.where(chosen[g], score_slabs[g], 0.0) for g in range(N_EXP_GROUPS)]
    tot = w[0]
    for part in w[1:]:
        tot = tot + part
    tot = jnp.sum(tot, axis=0, keepdims=True)

    sel_t = jnp.concatenate([jnp.where(c, 1.0, 0.0) for c in chosen], axis=0)
    t_row = lax.broadcasted_iota(jnp.int32, (tm, tm), 0)
    t_col = lax.broadcasted_iota(jnp.int32, (tm, tm), 1)
    rank_t = _dot(sel_t.astype(BF16), (t_row < t_col).astype(BF16))
    cnt = jnp.sum(sel_t, axis=-1, keepdims=True)
    pc = jnp.floor((cnt + (CHUNK_ROWS - 1.0)) * (1.0 / CHUNK_ROWS)) * CHUNK_ROWS
    total = jnp.sum(pc, axis=0, keepdims=True)
    fill = jnp.ceil(total * (1.0 / ROW_TILE)) * ROW_TILE - total
    e_row = lax.broadcasted_iota(jnp.int32, (N_EXPERTS, 1), 0)
    pc = pc + jnp.where(e_row == N_EXPERTS - 1, fill, 0.0)
    e_r = lax.broadcasted_iota(jnp.int32, (N_EXPERTS, N_EXPERTS), 0)
    e_c = lax.broadcasted_iota(jnp.int32, (N_EXPERTS, N_EXPERTS), 1)
    pc_rep = jnp.broadcast_to(pc, (N_EXPERTS, LANES))
    off = _dot((e_c < e_r).astype(BF16), pc_rep.astype(BF16))
    slot_t = off[:, :1] + rank_t
    slot8 = jnp.full((8, tm), -1.0, F32)
    w8 = jnp.zeros((8, tm), F32)
    for k, a in enumerate(picks):
        s_acc = jnp.zeros((EXP_PER_GROUP, tm), F32)
        w_acc = jnp.zeros((EXP_PER_GROUP, tm), F32)
        for g in range(N_EXP_GROUPS):
            hit = flat[g] == a
            s_acc = jnp.where(hit, slot_t[g * EXP_PER_GROUP:(g + 1) * EXP_PER_GROUP, :], s_acc)
            w_acc = jnp.where(hit, score_slabs[g], w_acc)
        slot8 = jnp.where(sub == k, jnp.sum(s_acc, axis=0, keepdims=True), slot8)
        w8 = jnp.where(sub == k, jnp.sum(w_acc, axis=0, keepdims=True) / tot * ROUTE_SCALE, w8)

    info_t = jnp.concatenate([part / tot * ROUTE_SCALE for part in w] + [slot8, w8]
                             + [jnp.zeros((LANES - N_EXPERTS - 16, tm), F32)], axis=0)
    o_ref[0] = info_t.T
    slot_o[0] = slot8
    w_o[0] = w8
    cnt_o[0] = pc_rep


def _router(h2, w_router_t, b_col, tm):
    b, s, d = h2.shape
    nj = s // tm
    return pl.pallas_call(
        functools.partial(_router_kernel, tm=tm),
        grid=(b, nj),
        in_specs=[pl.BlockSpec((1, tm, d), lambda i, j: (i, j, 0)),
                  _const_spec(w_router_t), _const_spec(b_col)],
        out_specs=[pl.BlockSpec((1, tm, LANES), lambda i, j: (i, j, 0)),
                   pl.BlockSpec((1, 8, tm), lambda i, j: (i, 0, j)),
                   pl.BlockSpec((1, 8, tm), lambda i, j: (i, 0, j)),
                   pl.BlockSpec((1, N_EXPERTS, LANES), lambda i, j: (i * nj + j, 0, 0))],
        out_shape=[jax.ShapeDtypeStruct((b, s, LANES), F32),
                   jax.ShapeDtypeStruct((b, 8, s), F32),
                   jax.ShapeDtypeStruct((b, 8, s), F32),
                   jax.ShapeDtypeStruct((b * nj, N_EXPERTS, LANES), F32)],
        compiler_params=_params("arbitrary", "arbitrary"),
        name="router",
    )(h2, w_router_t, b_col)


def _dispatch_kernel(nk_ref, h_ref, slot_ref, w_ref, xs_ref, os_ref, *, tm, cap):
    os_ref[...] = jnp.zeros(os_ref.shape, BF16)
    nk = nk_ref[pl.program_id(0) * pl.num_programs(1) + pl.program_id(1)]
    h = h_ref[0]
    slots = slot_ref[0]
    ws = w_ref[0]
    lane = lax.broadcasted_iota(jnp.int32, (ROW_TILE, LANES), 1)
    for rt in range(cap // ROW_TILE):
        rsl = slice(rt * ROW_TILE, (rt + 1) * ROW_TILE)

        @pl.when(rt < nk)
        def _():
            rows = (rt * ROW_TILE + lax.broadcasted_iota(jnp.int32, (ROW_TILE, tm), 0)).astype(F32)
            p = jnp.zeros((ROW_TILE, tm), F32)
            pw = jnp.zeros((ROW_TILE, tm), F32)
            for k in range(TOP_K):
                hit = rows == slots[k:k + 1, :]
                p = jnp.where(hit, 1.0, p)
                pw = jnp.where(hit, ws[k:k + 1, :], pw)
            x = _dot(p.astype(BF16), h)
            hi, mid, lo = (part.astype(F32) for part in _split3(jnp.sum(pw, axis=-1, keepdims=True)))
            extra = jnp.where(lane == 0, hi, jnp.where(lane == 1, mid, jnp.where(lane == 2, lo, 0.0)))
            xs_ref[0, rsl, :D_MODEL] = x.astype(BF16)
            xs_ref[0, rsl, D_MODEL:] = extra.astype(BF16)

        @pl.when(rt >= nk)
        def _():
            xs_ref[0, rsl, :] = jnp.zeros((ROW_TILE, D_MODEL + LANES), BF16)


def _dispatch(nk, h2, slotk, wk, tm, cap):
    b, s, d = h2.shape
    nj = s // tm
    grid_spec = pltpu.PrefetchScalarGridSpec(
        num_scalar_prefetch=1,
        grid=(b, nj),
        in_specs=[pl.BlockSpec((1, tm, d), lambda i, j, n_: (i, j, 0)),
                  pl.BlockSpec((1, 8, tm), lambda i, j, n_: (i, 0, j)),
                  pl.BlockSpec((1, 8, tm), lambda i, j, n_: (i, 0, j))],
        out_specs=[pl.BlockSpec((1, cap, d + LANES), lambda i, j, n_: (i * nj + j, 0, 0)),
                   pl.BlockSpec((1, cap, d), lambda i, j, n_: (i * nj + j, 0, 0))],
    )
    return pl.pallas_call(
        functools.partial(_dispatch_kernel, tm=tm, cap=cap),
        grid_spec=grid_spec,
        out_shape=[jax.ShapeDtypeStruct((b * nj, cap, d + LANES), BF16),
                   jax.ShapeDtypeStruct((b * nj, cap, d), BF16)],
        compiler_params=_params("arbitrary", "arbitrary"),
        name="moe_dispatch",
    )(nk, h2, slotk, wk)


def _expert_kernel(te_ref, src_ref, dst_ref, nt_ref, xs_hbm, wg_ref, wu_ref, wd_ref, os_zero_hbm,
                   os_hbm, xbuf, obuf, wgb, wub, wdb, sem):
    del os_zero_hbm
    i = pl.program_id(0)
    nt = nt_ref[0]

    def in_copies(tile, slot):
        return [pltpu.make_async_copy(xs_hbm.at[src_ref[tile * TILE_CHUNKS + q]], xbuf.at[slot, q],
                                      sem.at[0, slot]) for q in range(TILE_CHUNKS)]

    def out_copies(tile, slot):
        return [pltpu.make_async_copy(obuf.at[slot, q], os_hbm.at[dst_ref[tile * TILE_CHUNKS + q]],
                                      sem.at[1, slot]) for q in range(TILE_CHUNKS)]

    @pl.when(i < nt)
    def _():
        slot = lax.rem(i, 2)

        @pl.when(i == 0)
        def _():
            for cpy in in_copies(0, 0):
                cpy.start()

        @pl.when(i + 1 < nt)
        def _():
            for cpy in in_copies(i + 1, 1 - slot):
                cpy.start()

        @pl.when(jnp.logical_or(i == 0, te_ref[i] != te_ref[jnp.maximum(i - 1, 0)]))
        def _():
            wgb[...] = wg_ref[0].astype(BF16)
            wub[...] = wu_ref[0].astype(BF16)
            wdb[...] = wd_ref[0].astype(BF16)

        for cpy in in_copies(i, slot):
            cpy.wait()
        x = xbuf[slot].reshape(ROW_TILE, D_MODEL + LANES)
        xb = x[:, :D_MODEL]
        wrow = jnp.sum(x[:, D_MODEL:].astype(F32), axis=-1, keepdims=True)
        gte = _dot(xb, wgb[...])
        upe = _dot(xb, wub[...])
        act = (_silu(gte) * upe * wrow).astype(BF16)
        out = _dot(act, wdb[...]).astype(BF16)

        @pl.when(i >= 2)
        def _():
            for cpy in out_copies(i - 2, slot):
                cpy.wait()

        obuf[slot] = out.reshape(TILE_CHUNKS, CHUNK_ROWS, D_MODEL)
        for cpy in out_copies(i, slot):
            cpy.start()

        @pl.when(i == nt - 1)
        def _():
            @pl.when(i >= 1)
            def _():
                for cpy in out_copies(i - 1, 1 - slot):
                    cpy.wait()

            for cpy in out_copies(i, slot):
                cpy.wait()


def _experts(tile_expert, src, dst, n_tiles, xs_chunks, os_zero, w_gate_e, w_up_e, w_down_e, max_tiles):
    n_chunks, _, width = xs_chunks.shape
    d = width - LANES
    grid_spec = pltpu.PrefetchScalarGridSpec(
        num_scalar_prefetch=4,
        grid=(max_tiles,),
        in_specs=[pl.BlockSpec(memory_space=pl.ANY),
                  pl.BlockSpec((1, d, F_EXPERT), lambda i, te, s_, d_, n_: (te[i], 0, 0)),
                  pl.BlockSpec((1, d, F_EXPERT), lambda i, te, s_, d_, n_: (te[i], 0, 0)),
                  pl.BlockSpec((1, F_EXPERT, d), lambda i, te, s_, d_, n_: (te[i], 0, 0)),
                  pl.BlockSpec(memory_space=pl.ANY)],
        out_specs=pl.BlockSpec(memory_space=pl.ANY),
        scratch_shapes=[pltpu.VMEM((2, TILE_CHUNKS, CHUNK_ROWS, width), BF16),
                        pltpu.VMEM((2, TILE_CHUNKS, CHUNK_ROWS, d), BF16),
                        pltpu.VMEM((d, F_EXPERT), BF16), pltpu.VMEM((d, F_EXPERT), BF16),
                        pltpu.VMEM((F_EXPERT, d), BF16),
                        pltpu.SemaphoreType.DMA((2, 2))],
    )
    return pl.pallas_call(
        _expert_kernel,
        grid_spec=grid_spec,
        out_shape=jax.ShapeDtypeStruct((n_chunks, CHUNK_ROWS, d), BF16),
        input_output_aliases={8: 0},
        compiler_params=_params("arbitrary"),
        name="moe_experts",
    )(tile_expert, src, dst, n_tiles, xs_chunks, w_gate_e, w_up_e, w_down_e, os_zero)


def _combine_kernel(nk_ref, cum_ref, info_ref, h_ref, x1_ref, g2_ref, os_hbm, wgs, wus, wds, gpost,
                    y_o, obuf, sem, *, tm):
    nj = pl.num_programs(1)
    blk = pl.program_id(0) * nj + pl.program_id(1)
    nblk = pl.num_programs(0) * nj
    nk = nk_ref[blk]

    def fetch(bk, kt, slot):
        rows = pl.ds(pl.multiple_of(kt * ROW_TILE, ROW_TILE), ROW_TILE)
        return pltpu.make_async_copy(os_hbm.at[bk, rows], obuf.at[slot], sem.at[slot])

    @pl.when(blk == 0)
    def _():
        fetch(0, 0, 0).start()

    hb = h_ref[0]
    hid = _silu(_dot(hb, wgs[...])) * _dot(hb, wus[...])
    acc0 = _dot(hid.astype(BF16), wds[...])
    info = info_ref[0]
    slot_cols = [jnp.broadcast_to(info[:, N_EXPERTS + k:N_EXPERTS + k + 1], (tm, ROW_TILE))
                 for k in range(TOP_K)]
    lane_row = lax.broadcasted_iota(jnp.int32, (tm, ROW_TILE), 1).astype(F32)

    def body(kt, acc):
        slot = lax.rem(cum_ref[blk] + kt, 2)

        @pl.when(kt + 1 < nk)
        def _():
            fetch(blk, kt + 1, 1 - slot).start()

        @pl.when(jnp.logical_and(kt + 1 == nk, blk + 1 < nblk))
        def _():
            fetch(blk + 1, 0, 1 - slot).start()

        fetch(blk, kt, slot).wait()
        rows = lane_row + (kt * ROW_TILE).astype(F32)
        pt = jnp.zeros((tm, ROW_TILE), F32)
        for col in slot_cols:
            pt = jnp.where(rows == col, 1.0, pt)
        return acc + _dot(pt.astype(BF16), obuf[slot])

    acc = lax.fori_loop(0, nk, body, acc0)
    y_o[0] = x1_ref[0] + g2_ref[0] * _rms(acc, gpost[...])


def _combine(nk, cum_nk, info, h2, x1, g2, os_blocks, shared, gpost, tm):
    b, s, d = h2.shape
    nj = s // tm

    def tok(n):
        return pl.BlockSpec((1, tm, n), lambda i, j, *_: (i, j, 0))

    grid_spec = pltpu.PrefetchScalarGridSpec(
        num_scalar_prefetch=2,
        grid=(b, nj),
        in_specs=[tok(LANES), tok(d), tok(d), pl.BlockSpec((1, 1, d), lambda i, j, *_: (i, 0, 0)),
                  pl.BlockSpec(memory_space=pl.ANY)]
        + [_const_spec(w) for w in shared] + [_const_spec(gpost)],
        out_specs=tok(d),
        scratch_shapes=[pltpu.VMEM((2, ROW_TILE, d), BF16), pltpu.SemaphoreType.DMA((2,))],
    )
    return pl.pallas_call(
        functools.partial(_combine_kernel, tm=tm),
        grid_spec=grid_spec,
        out_shape=jax.ShapeDtypeStruct((b, s, d), F32),
        compiler_params=_params("arbitrary", "arbitrary"),
        name="moe_combine",
    )(nk, cum_nk, info, h2, x1, g2, os_blocks, *shared, gpost)


def _moe_plan(pc_rep, cap_chunks, max_tiles):
    pc = pc_rep[:, :, 0].astype(jnp.int32)
    nblk = pc.shape[0]
    nch = pc // CHUNK_ROWS
    off = jnp.cumsum(nch, axis=1) - nch
    nk = jnp.sum(pc, axis=1) // ROW_TILE
    cum_nk = jnp.cumsum(nk) - nk
    cum_blk = jnp.cumsum(nch, axis=0)
    tot = cum_blk[-1]
    tiles = (tot + TILE_CHUNKS - 1) // TILE_CHUNKS
    tile_end = jnp.cumsum(tiles)
    n_tiles = tile_end[-1]
    tile_ids = jnp.arange(max_tiles, dtype=jnp.int32)
    last_expert = jnp.sum((tile_end < n_tiles).astype(jnp.int32))
    te = jnp.sum((tile_end[None, :] <= tile_ids[:, None]).astype(jnp.int32), axis=1)
    te = jnp.minimum(te, last_expert)
    oh_e = (te[:, None] == jnp.arange(N_EXPERTS, dtype=jnp.int32)[None, :]).astype(jnp.int32)

    def by_expert(v):
        return jnp.sum(oh_e * v[None, :], axis=1) if v.ndim == 1 else jnp.sum(
            oh_e[:, :, None] * v[None, :, :], axis=1)

    pos = ((tile_ids - by_expert(tile_end - tiles))[:, None] * TILE_CHUNKS
           + jnp.arange(TILE_CHUNKS, dtype=jnp.int32)[None, :])
    valid = jnp.logical_and(tile_ids[:, None] < n_tiles, pos < by_expert(tot)[:, None])
    cum_e = by_expert(cum_blk.T)
    nch_e = by_expert(nch.T)
    off_e = by_expert(off.T)
    blk = jnp.sum((cum_e[:, None, :] <= pos[:, :, None]).astype(jnp.int32), axis=-1)
    blk = jnp.minimum(blk, nblk - 1)
    oh_b = (blk[:, :, None] == jnp.arange(nblk, dtype=jnp.int32)[None, None, :]).astype(jnp.int32)
    before = jnp.sum(oh_b * (cum_e - nch_e)[:, None, :], axis=-1)
    local = jnp.sum(oh_b * off_e[:, None, :], axis=-1) + (pos - before)
    chunk = blk * cap_chunks + local
    spare = cap_chunks - TILE_CHUNKS + jnp.arange(TILE_CHUNKS, dtype=jnp.int32)[None, :]
    src = jnp.where(valid, chunk, cap_chunks - 1)
    dst = jnp.where(valid, chunk, spare)
    return (te.astype(jnp.int32), src.reshape(-1).astype(jnp.int32), dst.reshape(-1).astype(jnp.int32),
            n_tiles.reshape(1).astype(jnp.int32), nk.astype(jnp.int32), cum_nk.astype(jnp.int32))


def _sparse_moe(h2, info, slotk, wk, pc_rep, x1, g2, w_gate_e, w_up_e, w_down_e, shared, gpost, tm):
    b, s, d = h2.shape
    nblk = b * (s // tm)
    cap = TOP_K * tm + N_EXPERTS * CHUNK_ROWS + ROW_TILE
    cap = -(-cap // ROW_TILE) * ROW_TILE
    cap_chunks = cap // CHUNK_ROWS
    max_tiles = nblk * (cap_chunks - TILE_CHUNKS) // TILE_CHUNKS + N_EXPERTS
    te, src, dst, n_tiles, nk, cum_nk = _moe_plan(pc_rep, cap_chunks, max_tiles)
    xs, os_zero = _dispatch(nk, h2, slotk, wk, tm, cap)
    os_chunks = _experts(te, src, dst, n_tiles, xs.reshape(nblk * cap_chunks, CHUNK_ROWS, d + LANES),
                         os_zero.reshape(nblk * cap_chunks, CHUNK_ROWS, d),
                         w_gate_e, w_up_e, w_down_e, max_tiles)
    return _combine(nk, cum_nk, info, h2, x1, g2, os_chunks.reshape(nblk, cap, d), shared, gpost, tm)


def _moe_kernel(h_ref, c_ref, x1_ref, g2_ref, wg_ref, wu_ref, wd_ref, wgs, wus, wds, gpost,
                y_o, acc_ref, *, tm):
    e = pl.program_id(2)
    ne = pl.num_programs(2)
    hb = h_ref[0]

    @pl.when(e == 0)
    def _():
        hid = _silu(_dot(hb, wgs[...])) * _dot(hb, wus[...])
        acc_ref[...] = _dot(hid.astype(BF16), wds[...])

    lane = lax.broadcasted_iota(jnp.int32, (tm, LANES), 1)
    ce = jnp.sum(jnp.where(lane == e, c_ref[0], 0.0), axis=-1, keepdims=True)
    gte = _dot(hb, wg_ref[0].astype(BF16))
    upe = _dot(hb, wu_ref[0].astype(BF16))
    act = (_silu(gte) * upe * ce).astype(BF16)
    acc_ref[...] += _dot(act, wd_ref[0].astype(BF16))

    @pl.when(e == ne - 1)
    def _():
        y_o[0] = x1_ref[0] + g2_ref[0] * _rms(acc_ref[...], gpost[...])


def _moe(h2, comb, x1, g2, w_gate_e, w_up_e, w_down_e, shared, gpost, tm):
    b, s, d = h2.shape
    per_row_mod = g2.shape[1] != 1
    mod_spec = (pl.BlockSpec((1, tm, d), lambda i, j, e: (i, j, 0)) if per_row_mod
                else pl.BlockSpec((1, 1, d), lambda i, j, e: (i, 0, 0)))

    def tok(n):
        return pl.BlockSpec((1, tm, n), lambda i, j, e: (i, j, 0))

    return pl.pallas_call(
        functools.partial(_moe_kernel, tm=tm),
        grid=(b, s // tm, N_EXPERTS),
        in_specs=[tok(d), tok(LANES), tok(d), mod_spec,
                  pl.BlockSpec((1, d, F_EXPERT), lambda i, j, e: (e, 0, 0)),
                  pl.BlockSpec((1, d, F_EXPERT), lambda i, j, e: (e, 0, 0)),
                  pl.BlockSpec((1, F_EXPERT, d), lambda i, j, e: (e, 0, 0))]
        + [_const_spec(w) for w in shared] + [_const_spec(gpost)],
        out_specs=tok(d),
        out_shape=jax.ShapeDtypeStruct((b, s, d), F32),
        scratch_shapes=[pltpu.VMEM((tm, d), F32)],
        compiler_params=_params("arbitrary", "arbitrary", "arbitrary"),
        name="moe",
    )(h2, comb, x1, g2, w_gate_e, w_up_e, w_down_e, *shared, gpost)


def _qlat_kernel(q_ref, wuk_ref, o_ref):
    for hd in range(A_HEADS):
        qn = q_ref[:, hd * HEAD_PAD:hd * HEAD_PAD + QK_NOPE]
        o_ref[hd] = _dot_nt(qn, wuk_ref[hd]).astype(BF16)


def _qlat(q, wuk_heads):
    nb = q.shape[0]
    return pl.pallas_call(
        _qlat_kernel,
        in_specs=[_const_spec(q), _const_spec(wuk_heads)],
        out_specs=pl.BlockSpec((A_HEADS, nb, KV_LORA), lambda: (0, 0, 0)),
        out_shape=jax.ShapeDtypeStruct((A_HEADS, nb, KV_LORA), BF16),
        grid=(),
        name="q_latent",
    )(q, wuk_heads)


def _decode_kernel(pt_ref, ql_ref, qp_ref, cn_ref, kn_ref, ckv_hbm, kpet_hbm, o_ref,
                   cbuf, kbuf, sem, *, pages_per_chunk, n_chunks):
    b = pl.program_id(0)
    nb = pl.num_programs(0)
    cp = pages_per_chunk
    halves = 2 if cp % 2 == 0 else 1
    hp = cp // halves
    total = nb * n_chunks
    ahead = DECODE_RING - 1

    def copies(g):
        g = jnp.asarray(g, jnp.int32)
        bb = lax.div(g, jnp.int32(n_chunks))
        chunk = g - bb * n_chunks
        slot = lax.rem(g, jnp.int32(DECODE_RING))
        out = []
        for p in range(cp):
            page = pt_ref[bb, chunk * cp + p]
            out.append(pltpu.make_async_copy(ckv_hbm.at[page], cbuf.at[slot, p], sem.at[0, slot]))
            out.append(pltpu.make_async_copy(kpet_hbm.at[page], kbuf.at[slot, p], sem.at[1, slot]))
        return out

    def start(g):
        for n, cpy in enumerate(copies(g)):
            cpy.start(priority=(n // 2) % 2)

    @pl.when(b == 0)
    def _():
        for g in range(ahead):
            @pl.when(g < total)
            def _():
                start(g)

    ql = ql_ref[0]
    qp = qp_ref[0]
    c_new = cn_ref[0]
    k_new = kn_ref[0]
    s_new = (jnp.sum(ql.astype(F32) * c_new, axis=-1, keepdims=True)
             + jnp.sum(qp.astype(F32) * k_new, axis=-1, keepdims=True)) * ATTN_SCALE
    m0 = s_new
    l0 = jnp.ones((A_HEADS, 1), F32)
    acc0 = jnp.broadcast_to(c_new, (A_HEADS, KV_LORA))

    def body(chunk, carry):
        m_prev, l_prev, acc = carry
        g = b * n_chunks + chunk
        slot = lax.rem(g, DECODE_RING)

        @pl.when(g + ahead < total)
        def _():
            start(g + ahead)

        for cpy in copies(g):
            cpy.wait()
        for hf in range(halves):
            kc = cbuf[slot, hf * hp:(hf + 1) * hp].reshape(hp * PAGE_SIZE, KV_LORA).astype(BF16)
            kt = jnp.concatenate([kbuf[slot, hf * hp + p] for p in range(hp)], axis=1).astype(BF16)
            s = (_dot_nt(ql, kc) + _dot(qp, kt)) * ATTN_SCALE
            m_new = jnp.maximum(m_prev, jnp.max(s, axis=-1, keepdims=True))
            alpha = jnp.exp(m_prev - m_new)
            p = jnp.exp(s - m_new)
            l_prev = alpha * l_prev + jnp.sum(p, axis=-1, keepdims=True)
            acc = alpha * acc + _dot(p.astype(BF16), kc)
            m_prev = m_new
        return m_prev, l_prev, acc

    _, l_fin, acc = lax.fori_loop(0, n_chunks, body, (m0, l0, acc0))
    o_ref[0] = acc / l_fin


def _decode_attention(page_table, qlat, qpe, ckv_new, kpe_new, cache_ckv, cache_kpe_t):
    nb, n_pages = page_table.shape
    cp = min(16, n_pages)
    n_chunks = n_pages // cp
    grid_spec = pltpu.PrefetchScalarGridSpec(
        num_scalar_prefetch=1,
        grid=(nb,),
        in_specs=[pl.BlockSpec((1, A_HEADS, KV_LORA), lambda i, pt: (i, 0, 0)),
                  pl.BlockSpec((1, A_HEADS, QK_ROPE), lambda i, pt: (i, 0, 0)),
                  pl.BlockSpec((1, 1, KV_LORA), lambda i, pt: (i, 0, 0)),
                  pl.BlockSpec((1, 1, QK_ROPE), lambda i, pt: (i, 0, 0)),
                  pl.BlockSpec(memory_space=pl.ANY),
                  pl.BlockSpec(memory_space=pl.ANY)],
        out_specs=pl.BlockSpec((1, A_HEADS, KV_LORA), lambda i, pt: (i, 0, 0)),
        scratch_shapes=[pltpu.VMEM((DECODE_RING, cp, PAGE_SIZE, KV_LORA), F32),
                        pltpu.VMEM((DECODE_RING, cp, QK_ROPE, PAGE_SIZE), F32),
                        pltpu.SemaphoreType.DMA((2, DECODE_RING))],
    )
    return pl.pallas_call(
        functools.partial(_decode_kernel, pages_per_chunk=cp, n_chunks=n_chunks),
        grid_spec=grid_spec,
        out_shape=jax.ShapeDtypeStruct((nb, A_HEADS, KV_LORA), F32),
        compiler_params=_params("arbitrary"),
        name="decode_attention",
    )(page_table, qlat, qpe, ckv_new, kpe_new, cache_ckv, cache_kpe_t)


def _vup_kernel(ol_ref, wuv_ref, o_ref):
    for hd in range(A_HEADS):
        o_ref[:, hd * V_DIM:(hd + 1) * V_DIM] = _dot(ol_ref[hd].astype(BF16), wuv_ref[hd]).astype(BF16)


def _value_up(o_lat_heads, wuv_heads):
    nb = o_lat_heads.shape[1]
    return pl.pallas_call(
        _vup_kernel,
        in_specs=[_const_spec(o_lat_heads), _const_spec(wuv_heads)],
        out_specs=pl.BlockSpec((nb, A_HEADS * V_DIM), lambda: (0, 0)),
        out_shape=jax.ShapeDtypeStruct((nb, A_HEADS * V_DIM), BF16),
        grid=(),
        name="value_up",
    )(o_lat_heads, wuv_heads)


def _ssd_step_kernel(xbc_ref, cs_ref, z_ref, dt_ref, st_ref, cw_ref, cb_ref, dtb_ref, na_ref,
                     dsk_ref, gs_ref, y_o, st_o, y_ref):
    conv = cb_ref[...] + cw_ref[CONV_W - 1:CONV_W, :] * xbc_ref[0]
    for j in range(CONV_W - 1):
        conv = conv + cw_ref[j:j + 1, :] * cs_ref[0, j:j + 1, :]
    xbc = _silu(conv)
    dt = _softplus(dt_ref[0] + dtb_ref[...])
    dec = jnp.exp(dt * na_ref[...])
    hi, mid, lo = (part.astype(F32) for part in _split3(xbc[:, :D_INNER]))
    r8 = lax.broadcasted_iota(jnp.int32, (8, D_INNER), 0)
    x8 = jnp.where(r8 == 0, hi, jnp.where(r8 == 1, mid, jnp.where(r8 == 2, lo, 0.0))).astype(BF16)
    ones8 = (lax.broadcasted_iota(jnp.int32, (8, D_STATE), 0) < 3).astype(BF16)
    x_col = _dot_tn(x8, ones8)
    for hd in range(S_HEADS):
        g = hd // GROUP_HEADS
        hsl = slice(hd * S_HEAD_DIM, (hd + 1) * S_HEAD_DIM)
        bm = xbc[:, D_INNER + g * D_STATE:D_INNER + (g + 1) * D_STATE]
        cm = xbc[:, D_INNER + (S_GROUPS + g) * D_STATE:D_INNER + (S_GROUPS + g + 1) * D_STATE]
        new = st_ref[0, hd] * dec[:, hd:hd + 1] + x_col[hsl, :] * (bm * dt[:, hd:hd + 1])
        st_o[0, hd] = new
        cb8 = jnp.broadcast_to(cm, (8, D_STATE)).astype(BF16)
        y_ref[:, hsl] = _dot_nt(cb8, new.astype(BF16))
    y = y_ref[0:1, :] + dsk_ref[...] * xbc[:, :D_INNER]
    y = y * _silu(z_ref[0])
    for g in range(S_GROUPS):
        gsl = slice(g * GROUP_CH, (g + 1) * GROUP_CH)
        y_o[0, :, gsl] = _rms(y[:, gsl], gs_ref[:, gsl]).astype(BF16)


def _ssd_sample(xbc, conv_state, z, dt, ssm_state, conv_w, conv_b, dtb_row, na_row, dskip, g_ssd):
    nb = xbc.shape[0]
    consts = (conv_w, conv_b, dtb_row, na_row, dskip, g_ssd)
    st_spec = pl.BlockSpec((1, S_HEADS, S_HEAD_DIM, D_STATE), lambda i: (i, 0, 0, 0))
    return pl.pallas_call(
        _ssd_step_kernel,
        grid=(nb,),
        in_specs=[pl.BlockSpec((1, 1, CONV_CH), lambda i: (i, 0, 0)),
                  pl.BlockSpec((1, CONV_W - 1, CONV_CH), lambda i: (i, 0, 0)),
                  pl.BlockSpec((1, 1, D_INNER), lambda i: (i, 0, 0)),
                  pl.BlockSpec((1, 1, LANES), lambda i: (i, 0, 0)),
                  st_spec] + [_const_spec(a) for a in consts],
        out_specs=[pl.BlockSpec((1, 1, D_INNER), lambda i: (i, 0, 0)), st_spec],
        out_shape=[jax.ShapeDtypeStruct((nb, 1, D_INNER), BF16),
                   jax.ShapeDtypeStruct(ssm_state.shape, F32)],
        scratch_shapes=[pltpu.VMEM((8, D_INNER), F32)],
        compiler_params=_params("arbitrary"),
        name="ssd_sample",
    )(xbc, conv_state, z, dt, ssm_state, *consts)


def _rot_half(w):
    half = QK_ROPE // 2
    return jnp.concatenate([-w[..., half:], w[..., :half]], axis=-1)


def _pad_cols(w, start, total):
    return jnp.pad(w, ((0, 0), (start, total - start - w.shape[1])))


def _head_pad(w_nope, w_rope):
    k = w_nope.shape[0]
    pad = jnp.zeros((k, A_HEADS, HEAD_PAD - QK_NOPE - QK_ROPE), w_nope.dtype)
    return jnp.concatenate([w_nope, w_rope, pad], axis=-1).reshape(k, A_HEADS * HEAD_PAD)


def _rope_tables(pos):
    half = QK_ROPE // 2
    inv = ROPE_BASE ** (-jnp.arange(half, dtype=F32) / half)
    ang = pos.astype(F32)[:, None] * inv[None, :]
    cos, sin = jnp.cos(ang), jnp.sin(ang)
    n = pos.shape[0]
    ctab = jnp.concatenate([jnp.ones((n, QK_NOPE), F32), cos, cos,
                            jnp.zeros((n, HEAD_PAD - QK_NOPE - QK_ROPE), F32)], axis=1)
    stab = jnp.concatenate([jnp.zeros((n, QK_NOPE), F32), sin, sin,
                            jnp.zeros((n, HEAD_PAD - QK_NOPE - QK_ROPE), F32)], axis=1)
    return ctab, stab


def _pick_tile(n, target):
    t = min(n, target)
    assert n % t == 0, (n, t)
    return t


def kernel(x_prompt, x_sample, cache_ckv, cache_kpe, state_conv, state_ssm, page_table, c_prompt, c_sample, w_ada, b_ada, g_pre_mix, g_post_mix, g_pre_ffn, g_post_ffn, w_in, g_q_a, w_q_b, g_kv_a, w_uk, w_uv, w_o_attn, conv_w, conv_b, dt_bias, a_log, d_skip, g_ssd, w_o_ssd, w_out, w_router, b_router, w_gate_e, w_up_e, w_down_e, w_gate_s, w_up_s, w_down_s):
    bp, sp, d = x_prompt.shape
    nb, ds, _ = x_sample.shape
    depth = w_in.shape[0]
    assert depth == 1 and ds == 1 and d == D_MODEL
    n_pages = page_table.shape[1]
    past_len = n_pages * PAGE_SIZE
    lyr = 0

    offs = [0]
    for n in IN_SPLITS:
        offs.append(offs[-1] + n)
    win = w_in[lyr]
    w_qa, w_kva, w_kpe, w_z, w_xbc, w_dt, w_ga, w_gb = (win[:, offs[i]:offs[i + 1]] for i in range(8))
    wqb = w_q_b[lyr].reshape(Q_LORA, A_HEADS, QK_NOPE + QK_ROPE)
    wq_pad = _head_pad(wqb[..., :QK_NOPE], wqb[..., QK_NOPE:])
    wq_rot = _head_pad(jnp.zeros_like(wqb[..., :QK_NOPE]), _rot_half(wqb[..., QK_NOPE:]))
    wuk_pad = _head_pad(w_uk[lyr], jnp.zeros((KV_LORA, A_HEADS, QK_ROPE), F32))
    premix_w = [
        w_qa.astype(BF16), w_kva.astype(BF16),
        _pad_cols(w_kpe, QK_NOPE, LANES).astype(BF16),
        _pad_cols(_rot_half(w_kpe), QK_NOPE, LANES).astype(BF16),
        w_z.astype(BF16), w_xbc.astype(BF16),
        _pad_cols(w_dt, 0, LANES).astype(BF16), w_dt.T.astype(BF16),
        w_ga.astype(BF16), w_gb.astype(BF16),
        g_q_a[lyr][None, :], wq_pad.astype(BF16), wq_rot.astype(BF16),
        g_kv_a[lyr][None, :], wuk_pad.astype(BF16),
        w_uv[lyr].reshape(KV_LORA, A_HEADS * V_DIM).astype(BF16),
    ]
    merge_w = [w_o_attn[lyr].astype(BF16), w_o_ssd[lyr].astype(BF16), w_out[lyr].astype(BF16),
               g_post_mix[lyr][None, :], g_pre_ffn[lyr][None, :]]
    shared_w = [w_gate_s[lyr].astype(BF16), w_up_s[lyr].astype(BF16), w_down_s[lyr].astype(BF16)]
    w_router_t = w_router[lyr].T.astype(BF16)
    b_router_col = b_router[lyr][:, None]
    g_pre = g_pre_mix[lyr][None, :]
    g_post_ffn_row = g_post_ffn[lyr][None, :]
    cw = conv_w[lyr]
    cb = conv_b[lyr][None, :]
    neg_a = -jnp.exp(a_log[lyr].astype(F32))
    dtb_row = jnp.pad(dt_bias[lyr], (0, LANES - S_HEADS))[None, :]
    na_row = jnp.pad(neg_a, (0, LANES - S_HEADS))[None, :]
    dtb_col = dt_bias[lyr][:, None]
    na_col = neg_a[:, None]
    dskip_row = jnp.repeat(d_skip[lyr].astype(F32), S_HEAD_DIM)[None, :]
    gssd_row = g_ssd[lyr][None, :]

    n_mod_rows = bp + nb
    pad_rows = (-n_mod_rows) % 16
    c_all = jnp.concatenate([c_prompt, c_sample, jnp.zeros((pad_rows, d), F32)], axis=0)
    mod = _adaln(c_all, w_ada[lyr], b_ada[lyr][None, :])
    mods = [mod[:, i * d:(i + 1) * d] for i in range(6)]
    mods_p = [m[:bp][:, None, :] for m in mods]
    mods_s = [m[bp:bp + nb][None, :, :] for m in mods]

    ctab_p, stab_p = _rope_tables(jnp.arange(sp))
    tm = _pick_tile(sp, 256)
    (q_p, k_p, v_p, ckv_p, kpe_pad_p, z_p, xbc_p, dt_p, dtt_p, ga_p, gb_p) = _premix(
        x_prompt, mods_p[0], mods_p[1], g_pre, ctab_p, stab_p, premix_w, tm)
    o_attn_p = _flash_attention(q_p, k_p, v_p, _pick_tile(sp, 512), 2)
    y_ssd_p, ssm_t_p = _ssd_prompt(xbc_p, z_p, dt_p, dtt_p, cw, cb, dtb_row, dtb_col, na_row, na_col,
                                   dskip_row, gssd_row)
    x1_p, h2_p = _merge(x_prompt, o_attn_p, y_ssd_p, ga_p, gb_p, mods_p[2], mods_p[3], mods_p[4],
                        merge_w, _pick_tile(sp, 512))
    tb = _pick_tile(sp, MOE_BLOCK)
    info_p, slotk_p, wk_p, pc_p = _router(h2_p, w_router_t, b_router_col, tb)
    y_prompt = _sparse_moe(h2_p, info_p, slotk_p, wk_p, pc_p, x1_p, mods_p[5],
                           w_gate_e[lyr], w_up_e[lyr], w_down_e[lyr], shared_w, g_post_ffn_row, tb)

    xs = x_sample.reshape(1, nb, d)
    ctab_s, stab_s = _rope_tables(past_len + jnp.arange(ds))
    (q_s, _, _, ckv_s, kpe_pad_s, z_s, xbc_s, dt_s, _, ga_s, gb_s) = _premix(
        xs, mods_s[0], mods_s[1], g_pre, ctab_s, stab_s, premix_w, nb)
    kpe_s = kpe_pad_s[0, :, QK_NOPE:QK_NOPE + QK_ROPE]
    wuk_heads = jnp.transpose(w_uk[lyr], (1, 0, 2)).astype(BF16)
    wuv_heads = jnp.transpose(w_uv[lyr], (1, 0, 2)).astype(BF16)
    qlat = jnp.transpose(_qlat(q_s[0], wuk_heads), (1, 0, 2))
    qpe = q_s[0].reshape(nb, A_HEADS, HEAD_PAD)[:, :, QK_NOPE:QK_NOPE + QK_ROPE]
    o_lat = _decode_attention(page_table, qlat, qpe,
                              ckv_s.reshape(nb, 1, KV_LORA), kpe_s.reshape(nb, 1, QK_ROPE),
                              cache_ckv.reshape(cache_ckv.shape[1:]),
                              jnp.swapaxes(cache_kpe.reshape(cache_kpe.shape[1:]), 1, 2))
    o_attn_s = _value_up(jnp.transpose(o_lat, (1, 0, 2)), wuv_heads)[None]
    y_ssd_s, ssm_s = _ssd_sample(xbc_s.reshape(nb, 1, CONV_CH), state_conv[lyr],
                                 z_s.reshape(nb, 1, D_INNER), dt_s.reshape(nb, 1, LANES),
                                 state_ssm.reshape(state_ssm.shape[1:]), cw, cb, dtb_row, na_row,
                                 dskip_row, gssd_row)
    x1_s, h2_s = _merge(xs, o_attn_s, y_ssd_s.reshape(1, nb, D_INNER), ga_s, gb_s,
                        mods_s[2], mods_s[3], mods_s[4], merge_w, nb)
    comb_s = _router(h2_s, w_router_t, b_router_col, nb)[0]
    y_s = _moe(h2_s, comb_s, x1_s, mods_s[5], w_gate_e[lyr], w_up_e[lyr], w_down_e[lyr],
               shared_w, g_post_ffn_row, nb)

    kpe_p = kpe_pad_p[:, :, QK_NOPE:QK_NOPE + QK_ROPE]
    conv_p = xbc_p[:, sp - (CONV_W - 1):, :]
    ssm_p = jnp.transpose(ssm_t_p.reshape(bp, D_STATE, S_HEADS, S_HEAD_DIM), (0, 2, 3, 1))
    conv_s = jnp.concatenate([state_conv[lyr][:, 1:, :], xbc_s.reshape(nb, 1, CONV_CH)], axis=1)
    return (y_prompt, y_s.reshape(nb, ds, d),
            ckv_p[None], kpe_p[None], conv_p[None], ssm_p[None].astype(x_prompt.dtype),
            ckv_s.reshape(1, nb, ds, KV_LORA), kpe_s.reshape(1, nb, ds, QK_ROPE),
            conv_s[None], ssm_s[None].astype(x_sample.dtype))
```

```python
import functools

import jax
import jax.numpy as jnp
from jax import lax
from jax.experimental import pallas as pl
from jax.experimental.pallas import tpu as pltpu

F32 = jnp.float32
BF16 = jnp.bfloat16

D_MODEL = 1024
PAGE_SIZE = 128
A_HEADS = 8
QK_NOPE = 64
QK_ROPE = 32
V_DIM = 64
Q_LORA = 384
KV_LORA = 256
ROPE_BASE = 10000.0
ATTN_SCALE = (QK_NOPE + QK_ROPE) ** -0.5
S_HEADS = 16
S_HEAD_DIM = 64
D_INNER = S_HEADS * S_HEAD_DIM
S_GROUPS = 2
D_STATE = 128
CONV_W = 4
CONV_CH = D_INNER + 2 * S_GROUPS * D_STATE
CHUNK = 128
N_EXPERTS = 64
TOP_K = 6
N_EXP_GROUPS = 8
TOPK_GROUPS = 4
F_EXPERT = 256
ROUTE_SCALE = 2.5
EPS = 1e-6
IN_SPLITS = (Q_LORA, KV_LORA, QK_ROPE, D_INNER, CONV_CH, S_HEADS, D_MODEL, D_MODEL)

LANES = 128
HEAD_PAD = LANES
GROUP_HEADS = S_HEADS // S_GROUPS
GROUP_CH = D_INNER // S_GROUPS
EXP_PER_GROUP = N_EXPERTS // N_EXP_GROUPS
MOE_BLOCK = 256
CHUNK_ROWS = 16
ROW_TILE = 256
SUB_CHUNKS = ROW_TILE // CHUNK_ROWS
TILE_CHUNKS = 2 * SUB_CHUNKS
DECODE_RING = 4
COMBINE_RING = 4
VMEM_LIMIT_BYTES = 56 * 1024 * 1024

NEG_INF = float("-inf")
LOG2_E = 1.4426950408889634


def _params(*semantics):
    return pltpu.CompilerParams(dimension_semantics=semantics, vmem_limit_bytes=VMEM_LIMIT_BYTES)


def _const_spec(arr):
    nd = arr.ndim
    return pl.BlockSpec(arr.shape, lambda *_: (0,) * nd)


def _dot(a, b):
    return jnp.dot(a, b, preferred_element_type=F32)


def _dot_nt(a, b):
    return lax.dot_general(a, b, (((1,), (1,)), ((), ())), preferred_element_type=F32)


def _dot_tn(a, b):
    return lax.dot_general(a, b, (((0,), (0,)), ((), ())), preferred_element_type=F32)


def _rms(x, g):
    return x * lax.rsqrt(jnp.mean(x * x, axis=-1, keepdims=True) + EPS) * g


def _silu(x):
    return x * (1.0 / (1.0 + jnp.exp(-x)))


def _sigmoid(x):
    return 1.0 / (1.0 + jnp.exp(-x))


def _softplus(x):
    return jnp.maximum(x, 0.0) + jnp.log(1.0 + jnp.exp(-jnp.abs(x)))


def _split3(x):
    hi = x.astype(BF16)
    r1 = x - hi.astype(F32)
    mid = r1.astype(BF16)
    lo = (r1 - mid.astype(F32)).astype(BF16)
    return hi, mid, lo


def _adaln_kernel(c_ref, w_ref, b_ref, o_ref):
    c = _silu(c_ref[...]).astype(BF16)
    o_ref[...] = _dot(c, w_ref[...].astype(BF16)) + b_ref[...]


def _adaln(c_all, w_ada, b_ada):
    rows, d = c_all.shape
    n = w_ada.shape[1]
    tn = 512
    return pl.pallas_call(
        _adaln_kernel,
        grid=(n // tn,),
        in_specs=[pl.BlockSpec((rows, d), lambda j: (0, 0)),
                  pl.BlockSpec((d, tn), lambda j: (0, j)),
                  pl.BlockSpec((1, tn), lambda j: (0, j))],
        out_specs=pl.BlockSpec((rows, tn), lambda j: (0, j)),
        out_shape=jax.ShapeDtypeStruct((rows, n), F32),
        compiler_params=_params("arbitrary"),
        name="adaln",
    )(c_all, w_ada, b_ada)


def _premix_kernel(x_ref, sh_ref, sc_ref, g_ref, ct_ref, st_ref,
                   wqa, wkva, wkpe, wkper, wz, wxbc, wdt, wdtt, wga, wgb,
                   gqa, wq, wqr, gkv, wuk, wuv,
                   q_o, k_o, v_o, ckv_o, kpe_o, z_o, xbc_o, dt_o, dtt_o, ga_o, gb_o):
    x = x_ref[0]
    h = _rms(x, g_ref[...]) * (1.0 + sc_ref[0]) + sh_ref[0]
    hb = h.astype(BF16)
    ct = ct_ref[...]
    st = st_ref[...]

    qn = _rms(_dot(hb, wqa[...]), gqa[...]).astype(BF16)
    q = _dot(qn, wq[...])
    qr = _dot(qn, wqr[...])
    for hd in range(A_HEADS):
        sl = slice(hd * HEAD_PAD, (hd + 1) * HEAD_PAD)
        q_o[0, :, sl] = (q[:, sl] * ct + qr[:, sl] * st).astype(BF16)

    ckv = _rms(_dot(hb, wkva[...]), gkv[...])
    ckv_o[0] = ckv
    cb = ckv.astype(BF16)
    kpe = _dot(hb, wkpe[...]) * ct + _dot(hb, wkper[...]) * st
    kpe_o[0] = kpe
    kn = _dot(cb, wuk[...])
    for hd in range(A_HEADS):
        sl = slice(hd * HEAD_PAD, (hd + 1) * HEAD_PAD)
        k_o[0, :, sl] = (kn[:, sl] + kpe).astype(BF16)
    v_o[0] = _dot(cb, wuv[...]).astype(BF16)

    z_o[0] = _dot(hb, wz[...])
    xbc_o[0] = _dot(hb, wxbc[...])
    dt_o[0] = _dot(hb, wdt[...])
    dtt_o[0] = _dot_nt(wdtt[...], hb)
    ga_o[0] = _dot(hb, wga[...])
    gb_o[0] = _dot(hb, wgb[...])


def _premix(x, sh, sc, g, ctab, stab, weights, tm):
    b, s, d = x.shape
    per_row_mod = sh.shape[1] != 1
    per_row_tab = ctab.shape[0] != 1
    mod_spec = (pl.BlockSpec((1, tm, d), lambda i, j: (i, j, 0)) if per_row_mod
                else pl.BlockSpec((1, 1, d), lambda i, j: (i, 0, 0)))
    tab_spec = (pl.BlockSpec((tm, LANES), lambda i, j: (j, 0)) if per_row_tab
                else pl.BlockSpec((1, LANES), lambda i, j: (0, 0)))

    def tok(n):
        return pl.BlockSpec((1, tm, n), lambda i, j: (i, j, 0))

    out_cols = [(A_HEADS * HEAD_PAD, BF16), (A_HEADS * HEAD_PAD, BF16), (A_HEADS * V_DIM, BF16),
                (KV_LORA, F32), (LANES, F32), (D_INNER, F32), (CONV_CH, F32), (LANES, F32)]
    out_shape = [jax.ShapeDtypeStruct((b, s, n), dt) for n, dt in out_cols]
    out_specs = [tok(n) for n, _ in out_cols]
    out_shape.append(jax.ShapeDtypeStruct((b, S_HEADS, s), F32))
    out_specs.append(pl.BlockSpec((1, S_HEADS, tm), lambda i, j: (i, 0, j)))
    out_shape += [jax.ShapeDtypeStruct((b, s, D_MODEL), F32)] * 2
    out_specs += [tok(D_MODEL)] * 2
    order = [out_shape[i] for i in (0, 1, 2, 3, 4, 5, 6, 7, 8, 9, 10)]
    return pl.pallas_call(
        _premix_kernel,
        grid=(b, s // tm),
        in_specs=[tok(d), mod_spec, mod_spec, _const_spec(g), tab_spec, tab_spec]
        + [_const_spec(w) for w in weights],
        out_specs=out_specs,
        out_shape=order,
        compiler_params=_params("arbitrary", "arbitrary"),
        name="premix",
    )(x, sh, sc, g, ctab, stab, *weights)


def _flash_kernel(q_ref, k_ref, v_ref, o_ref, m_ref, l_ref, acc_ref, *, tq, heads):
    qi = pl.program_id(2)
    m_ref[...] = jnp.full(m_ref.shape, NEG_INF, F32)
    l_ref[...] = jnp.zeros(l_ref.shape, F32)
    acc_ref[...] = jnp.zeros(acc_ref.shape, F32)
    row = lax.broadcasted_iota(jnp.int32, (tq, tq), 0)
    col = lax.broadcasted_iota(jnp.int32, (tq, tq), 1)
    c2 = ATTN_SCALE * LOG2_E

    def tile(kj, on_diagonal):
        koff = pl.multiple_of(kj * tq, tq)
        for hh in range(heads):
            sl = slice(hh * HEAD_PAD, (hh + 1) * HEAD_PAD)
            vsl = slice((hh // 2) * 2 * V_DIM, (hh // 2 + 1) * 2 * V_DIM)
            s = _dot_nt(q_ref[0, :, sl], k_ref[0, pl.ds(koff, tq), sl])
            if on_diagonal:
                s = jnp.where(col <= row, s, NEG_INF)
            m_prev = m_ref[hh]
            m_new = jnp.maximum(m_prev, jnp.max(s, axis=-1, keepdims=True))
            alpha = jnp.exp2((m_prev - m_new) * c2)
            p_parts = [jnp.exp2((s[:, j * LANES:(j + 1) * LANES] - m_new) * c2)
                       for j in range(tq // LANES)]
            l_ref[hh] = alpha * l_ref[hh] + sum(p_parts)
            p = jnp.concatenate(p_parts, axis=1).astype(BF16)
            acc_ref[hh] = alpha * acc_ref[hh] + _dot(p, v_ref[0, pl.ds(koff, tq), vsl])
            m_ref[hh] = m_new

    def body(kj, carry):
        tile(kj, False)
        return carry

    lax.fori_loop(0, qi, body, 0)
    tile(qi, True)
    lane = lax.broadcasted_iota(jnp.int32, (tq, 2 * V_DIM), 1)
    for pair in range(heads // 2):
        o0 = acc_ref[2 * pair] / jnp.sum(l_ref[2 * pair], axis=-1, keepdims=True)
        o1 = acc_ref[2 * pair + 1] / jnp.sum(l_ref[2 * pair + 1], axis=-1, keepdims=True)
        o_ref[0, :, pair * 2 * V_DIM:(pair + 1) * 2 * V_DIM] = jnp.where(lane < V_DIM, o0, o1).astype(BF16)


def _flash_attention(q, k, v, tq, heads):
    b, s, _ = q.shape
    return pl.pallas_call(
        functools.partial(_flash_kernel, tq=tq, heads=heads),
        grid=(b, A_HEADS // heads, s // tq),
        in_specs=[pl.BlockSpec((1, tq, heads * HEAD_PAD), lambda bi, hg, qi: (bi, qi, hg)),
                  pl.BlockSpec((1, s, heads * HEAD_PAD), lambda bi, hg, qi: (bi, 0, hg)),
                  pl.BlockSpec((1, s, heads * V_DIM), lambda bi, hg, qi: (bi, 0, hg))],
        out_specs=pl.BlockSpec((1, tq, heads * V_DIM), lambda bi, hg, qi: (bi, qi, hg)),
        out_shape=jax.ShapeDtypeStruct((b, s, A_HEADS * V_DIM), BF16),
        scratch_shapes=[pltpu.VMEM((heads, tq, LANES), F32), pltpu.VMEM((heads, tq, LANES), F32),
                        pltpu.VMEM((heads, tq, 2 * V_DIM), F32)],
        compiler_params=_params("arbitrary", "arbitrary", "arbitrary"),
        name="flash_attention",
    )(q, k, v)


def _ssd_chunk_kernel(xbc_ref, z_ref, dt_ref, dtt_ref, cw_ref, cb_ref, dtb_ref, dtbc_ref,
                      na_ref, nac_ref, dsk_ref, gs_ref, y_o, st_o, xc_ref, state_ref, y_ref):
    c = pl.program_id(1)
    nc = pl.num_programs(1)

    @pl.when(c == 0)
    def _():
        xc_ref[0:8, :] = jnp.zeros((8, CONV_CH), F32)
        state_ref[...] = jnp.zeros(state_ref.shape, F32)

    xc_ref[8:8 + CHUNK, :] = xbc_ref[0]
    conv = cb_ref[...] + cw_ref[CONV_W - 1:CONV_W, :] * xc_ref[8:8 + CHUNK, :]
    for j in range(1, CONV_W):
        conv = conv + cw_ref[CONV_W - 1 - j:CONV_W - j, :] * xc_ref[8 - j:8 - j + CHUNK, :]
    xc_ref[0:8, :] = xc_ref[CHUNK:CHUNK + 8, :]
    xbc = _silu(conv)

    dt_c = _softplus(dt_ref[0] + dtb_ref[...])
    a_c = dt_c * na_ref[...]
    dt_r = _softplus(dtt_ref[0] + dtbc_ref[...])
    a_r = dt_r * nac_ref[...]
    li = lax.broadcasted_iota(jnp.int32, (CHUNK, CHUNK), 0)
    si = lax.broadcasted_iota(jnp.int32, (CHUNK, CHUNK), 1)
    lower = (si <= li)
    tri = lower.astype(BF16)
    acum_c = sum(_dot(tri, part) for part in _split3(a_c))
    acum_r = sum(_dot_nt(part, tri) for part in _split3(a_r))
    a_last = acum_c[CHUNK - 1:CHUNK, :]
    dec_end = jnp.exp(a_last - acum_c)
    exp_ac = jnp.exp(acum_c)
    dec_all = jnp.exp(a_last)

    for g in range(S_GROUPS):
        bm = xbc[:, D_INNER + g * D_STATE:D_INNER + (g + 1) * D_STATE]
        cm = xbc[:, D_INNER + (S_GROUPS + g) * D_STATE:D_INNER + (S_GROUPS + g + 1) * D_STATE]
        bmb = bm.astype(BF16)
        cmb = cm.astype(BF16)
        gmat = _dot_nt(cmb, bmb)
        gsl = slice(g * GROUP_CH, (g + 1) * GROUP_CH)
        y_off = _dot(cmb, state_ref[:, gsl].astype(BF16))
        xde_parts = []
        for hh in range(GROUP_HEADS):
            hd = g * GROUP_HEADS + hh
            hsl = slice(hd * S_HEAD_DIM, (hd + 1) * S_HEAD_DIM)
            xs = xbc[:, hsl]
            xdt = xs * dt_c[:, hd:hd + 1]
            seg = acum_c[:, hd:hd + 1] - acum_r[hd:hd + 1, :]
            lmat = jnp.exp(jnp.where(lower, seg, NEG_INF))
            y_diag = _dot((gmat * lmat).astype(BF16), xdt.astype(BF16))
            yo = y_off[:, hh * S_HEAD_DIM:(hh + 1) * S_HEAD_DIM] * exp_ac[:, hd:hd + 1]
            y_ref[:, hsl] = y_diag + yo
            xde_parts.append((xdt * dec_end[:, hd:hd + 1]).astype(BF16))
        xde = jnp.concatenate(xde_parts, axis=1)
        new_states = _dot_tn(bmb, xde)
        for hh in range(GROUP_HEADS):
            hd = g * GROUP_HEADS + hh
            hsl = slice(hd * S_HEAD_DIM, (hd + 1) * S_HEAD_DIM)
            lsl = slice(hh * S_HEAD_DIM, (hh + 1) * S_HEAD_DIM)
            state_ref[:, hsl] = state_ref[:, hsl] * dec_all[:, hd:hd + 1] + new_states[:, lsl]

    y = y_ref[...] + dsk_ref[...] * xbc[:, :D_INNER]
    y = y * _silu(z_ref[0])
    for g in range(S_GROUPS):
        gsl = slice(g * GROUP_CH, (g + 1) * GROUP_CH)
        y_o[0, :, gsl] = _rms(y[:, gsl], gs_ref[:, gsl]).astype(BF16)

    @pl.when(c == nc - 1)
    def _():
        st_o[0] = state_ref[...]


def _ssd_prompt(xbc, z, dt, dtt, conv_w, conv_b, dtb_row, dtb_col, na_row, na_col, dskip, g_ssd):
    b, s, _ = xbc.shape
    nc = s // CHUNK
    consts = (conv_w, conv_b, dtb_row, dtb_col, na_row, na_col, dskip, g_ssd)
    return pl.pallas_call(
        _ssd_chunk_kernel,
        grid=(b, nc),
        in_specs=[pl.BlockSpec((1, CHUNK, CONV_CH), lambda i, j: (i, j, 0)),
                  pl.BlockSpec((1, CHUNK, D_INNER), lambda i, j: (i, j, 0)),
                  pl.BlockSpec((1, CHUNK, LANES), lambda i, j: (i, j, 0)),
                  pl.BlockSpec((1, S_HEADS, CHUNK), lambda i, j: (i, 0, j))]
        + [_const_spec(a) for a in consts],
        out_specs=[pl.BlockSpec((1, CHUNK, D_INNER), lambda i, j: (i, j, 0)),
                   pl.BlockSpec((1, D_STATE, D_INNER), lambda i, j: (i, 0, 0))],
        out_shape=[jax.ShapeDtypeStruct((b, s, D_INNER), BF16),
                   jax.ShapeDtypeStruct((b, D_STATE, D_INNER), F32)],
        scratch_shapes=[pltpu.VMEM((CHUNK + 8, CONV_CH), F32),
                        pltpu.VMEM((D_STATE, D_INNER), F32),
                        pltpu.VMEM((CHUNK, D_INNER), F32)],
        compiler_params=_params("arbitrary", "arbitrary"),
        name="ssd_prompt",
    )(xbc, z, dt, dtt, *consts)


def _merge_kernel(x_ref, oa_ref, ys_ref, ga_ref, gb_ref, g1_ref, sh2_ref, sc2_ref,
                  woa, wos, wout, gpost, gpre, x1_o, h2_o):
    o_attn = _dot(oa_ref[0], woa[...])
    o_ssd = _dot(ys_ref[0], wos[...])
    merged = _sigmoid(ga_ref[0]) * o_attn + _sigmoid(gb_ref[0]) * o_ssd
    m = _dot(merged.astype(BF16), wout[...])
    x1 = x_ref[0] + g1_ref[0] * _rms(m, gpost[...])
    x1_o[0] = x1
    h2_o[0] = (_rms(x1, gpre[...]) * (1.0 + sc2_ref[0]) + sh2_ref[0]).astype(BF16)


def _merge(x, o_attn, y_ssd, ga, gb, g1, sh2, sc2, weights, tm):
    b, s, d = x.shape
    per_row_mod = g1.shape[1] != 1
    mod_spec = (pl.BlockSpec((1, tm, d), lambda i, j: (i, j, 0)) if per_row_mod
                else pl.BlockSpec((1, 1, d), lambda i, j: (i, 0, 0)))

    def tok(n):
        return pl.BlockSpec((1, tm, n), lambda i, j: (i, j, 0))

    return pl.pallas_call(
        _merge_kernel,
        grid=(b, s // tm),
        in_specs=[tok(d), tok(A_HEADS * V_DIM), tok(D_INNER), tok(d), tok(d),
                  mod_spec, mod_spec, mod_spec] + [_const_spec(w) for w in weights],
        out_specs=[tok(d), tok(d)],
        out_shape=[jax.ShapeDtypeStruct((b, s, d), F32), jax.ShapeDtypeStruct((b, s, d), BF16)],
        compiler_params=_params("arbitrary", "arbitrary"),
        name="merge",
    )(x, o_attn, y_ssd, ga, gb, g1, sh2, sc2, *weights)


def _first_max(vals, idx, big):
    m = vals[0]
    for v in vals[1:]:
        m = jnp.maximum(m, v)
    m = jnp.max(m, axis=0, keepdims=True)
    cand = [jnp.where(v == m, i, big) for v, i in zip(vals, idx)]
    a = cand[0]
    for cnd in cand[1:]:
        a = jnp.minimum(a, cnd)
    a = jnp.min(a, axis=0, keepdims=True)
    return m, a


def _router_kernel(h_ref, wt_ref, b_ref, o_ref, slot_o, w_o, cnt_o, *, tm):
    logits = _dot_nt(wt_ref[...], h_ref[0])
    scores = _sigmoid(logits)
    biased = scores + b_ref[...]
    sub = lax.broadcasted_iota(jnp.int32, (EXP_PER_GROUP, tm), 0)
    slabs = [biased[g * EXP_PER_GROUP:(g + 1) * EXP_PER_GROUP, :] for g in range(N_EXP_GROUPS)]
    big = jnp.int32(N_EXPERTS)

    gscore = []
    for g in range(N_EXP_GROUPS):
        m1, a1 = _first_max([slabs[g]], [sub], big)
        rest = jnp.where(sub == a1, NEG_INF, slabs[g])
        m2 = jnp.max(rest, axis=0, keepdims=True)
        gscore.append(m1 + m2)
    gs = jnp.full((N_EXP_GROUPS, tm), NEG_INF, F32)
    for g in range(N_EXP_GROUPS):
        gs = jnp.where(sub == g, gscore[g], gs)
    gsel = jnp.zeros((N_EXP_GROUPS, tm), F32)
    for _ in range(TOPK_GROUPS):
        _, a = _first_max([gs], [sub], big)
        hit = sub == a
        gsel = jnp.where(hit, 1.0, gsel)
        gs = jnp.where(hit, NEG_INF, gs)

    masked = [jnp.where(gsel[g:g + 1, :] > 0.5, slabs[g], NEG_INF) for g in range(N_EXP_GROUPS)]
    flat = [sub + g * EXP_PER_GROUP for g in range(N_EXP_GROUPS)]
    chosen = [jnp.zeros((EXP_PER_GROUP, tm), jnp.bool_) for _ in range(N_EXP_GROUPS)]
    picks = []
    for _ in range(TOP_K):
        _, a = _first_max(masked, flat, big)
        picks.append(a)
        for g in range(N_EXP_GROUPS):
            hit = flat[g] == a
            chosen[g] = jnp.logical_or(chosen[g], hit)
            masked[g] = jnp.where(hit, NEG_INF, masked[g])
    score_slabs = [scores[g * EXP_PER_GROUP:(g + 1) * EXP_PER_GROUP, :] for g in range(N_EXP_GROUPS)]
    w = [jnp.where(chosen[g], score_slabs[g], 0.0) for g in range(N_EXP_GROUPS)]
    tot = w[0]
    for part in w[1:]:
        tot = tot + part
    tot = jnp.sum(tot, axis=0, keepdims=True)

    sel_t = jnp.concatenate([jnp.where(c, 1.0, 0.0) for c in chosen], axis=0)
    t_row = lax.broadcasted_iota(jnp.int32, (tm, tm), 0)
    t_col = lax.broadcasted_iota(jnp.int32, (tm, tm), 1)
    rank_t = _dot(sel_t.astype(BF16), (t_row < t_col).astype(BF16))
    cnt = jnp.sum(sel_t, axis=-1, keepdims=True)
    pc = jnp.floor((cnt + (CHUNK_ROWS - 1.0)) * (1.0 / CHUNK_ROWS)) * CHUNK_ROWS
    total = jnp.sum(pc, axis=0, keepdims=True)
    fill = jnp.ceil(total * (1.0 / ROW_TILE)) * ROW_TILE - total
    e_row = lax.broadcasted_iota(jnp.int32, (N_EXPERTS, 1), 0)
    pc = pc + jnp.where(e_row == N_EXPERTS - 1, fill, 0.0)
    e_r = lax.broadcasted_iota(jnp.int32, (N_EXPERTS, N_EXPERTS), 0)
    e_c = lax.broadcasted_iota(jnp.int32, (N_EXPERTS, N_EXPERTS), 1)
    pc_rep = jnp.broadcast_to(pc, (N_EXPERTS, LANES))
    off = _dot((e_c < e_r).astype(BF16), pc_rep.astype(BF16))
    slot_t = off[:, :1] + rank_t
    slot8 = jnp.full((8, tm), -1.0, F32)
    w8 = jnp.zeros((8, tm), F32)
    for k, a in enumerate(picks):
        s_acc = jnp.zeros((EXP_PER_GROUP, tm), F32)
        w_acc = jnp.zeros((EXP_PER_GROUP, tm), F32)
        for g in range(N_EXP_GROUPS):
            hit = flat[g] == a
            s_acc = jnp.where(hit, slot_t[g * EXP_PER_GROUP:(g + 1) * EXP_PER_GROUP, :], s_acc)
            w_acc = jnp.where(hit, score_slabs[g], w_acc)
        slot8 = jnp.where(sub == k, jnp.sum(s_acc, axis=0, keepdims=True), slot8)
        w8 = jnp.where(sub == k, jnp.sum(w_acc, axis=0, keepdims=True) / tot * ROUTE_SCALE, w8)

    info_t = jnp.concatenate([part / tot * ROUTE_SCALE for part in w] + [slot8, w8]
                             + [jnp.zeros((LANES - N_EXPERTS - 16, tm), F32)], axis=0)
    o_ref[0] = info_t.T
    slot_o[0] = slot8
    w_o[0] = w8
    cnt_o[0] = pc_rep


def _router(h2, w_router_t, b_col, tm):
    b, s, d = h2.shape
    nj = s // tm
    return pl.pallas_call(
        functools.partial(_router_kernel, tm=tm),
        grid=(b, nj),
        in_specs=[pl.BlockSpec((1, tm, d), lambda i, j: (i, j, 0)),
                  _const_spec(w_router_t), _const_spec(b_col)],
        out_specs=[pl.BlockSpec((1, tm, LANES), lambda i, j: (i, j, 0)),
                   pl.BlockSpec((1, 8, tm), lambda i, j: (i, 0, j)),
                   pl.BlockSpec((1, 8, tm), lambda i, j: (i, 0, j)),
                   pl.BlockSpec((1, N_EXPERTS, LANES), lambda i, j: (i * nj + j, 0, 0))],
        out_shape=[jax.ShapeDtypeStruct((b, s, LANES), F32),
                   jax.ShapeDtypeStruct((b, 8, s), F32),
                   jax.ShapeDtypeStruct((b, 8, s), F32),
                   jax.ShapeDtypeStruct((b * nj, N_EXPERTS, LANES), F32)],
        compiler_params=_params("arbitrary", "arbitrary"),
        name="router",
    )(h2, w_router_t, b_col)


def _dispatch_kernel(nk_ref, h_ref, slot_ref, w_ref, xs_ref, os_ref, *, tm, cap):
    os_ref[...] = jnp.zeros(os_ref.shape, BF16)
    nk = nk_ref[pl.program_id(0) * pl.num_programs(1) + pl.program_id(1)]
    h = h_ref[0]
    slots = slot_ref[0]
    ws = w_ref[0]
    lane = lax.broadcasted_iota(jnp.int32, (ROW_TILE, LANES), 1)
    for rt in range(cap // ROW_TILE):
        rsl = slice(rt * ROW_TILE, (rt + 1) * ROW_TILE)

        @pl.when(rt < nk)
        def _():
            rows = (rt * ROW_TILE + lax.broadcasted_iota(jnp.int32, (ROW_TILE, tm), 0)).astype(F32)
            p = jnp.zeros((ROW_TILE, tm), F32)
            pw = jnp.zeros((ROW_TILE, tm), F32)
            for k in range(TOP_K):
                hit = rows == slots[k:k + 1, :]
                p = jnp.where(hit, 1.0, p)
                pw = jnp.where(hit, ws[k:k + 1, :], pw)
            x = _dot(p.astype(BF16), h)
            hi, mid, lo = (part.astype(F32) for part in _split3(jnp.sum(pw, axis=-1, keepdims=True)))
            extra = jnp.where(lane == 0, hi, jnp.where(lane == 1, mid, jnp.where(lane == 2, lo, 0.0)))
            xs_ref[0, rsl, :D_MODEL] = x.astype(BF16)
            xs_ref[0, rsl, D_MODEL:] = extra.astype(BF16)

        @pl.when(rt >= nk)
        def _():
            xs_ref[0, rsl, :] = jnp.zeros((ROW_TILE, D_MODEL + LANES), BF16)


def _dispatch(nk, h2, slotk, wk, tm, cap):
    b, s, d = h2.shape
    nj = s // tm
    grid_spec = pltpu.PrefetchScalarGridSpec(
        num_scalar_prefetch=1,
        grid=(b, nj),
        in_specs=[pl.BlockSpec((1, tm, d), lambda i, j, n_: (i, j, 0)),
                  pl.BlockSpec((1, 8, tm), lambda i, j, n_: (i, 0, j)),
                  pl.BlockSpec((1, 8, tm), lambda i, j, n_: (i, 0, j))],
        out_specs=[pl.BlockSpec((1, cap, d + LANES), lambda i, j, n_: (i * nj + j, 0, 0)),
                   pl.BlockSpec((1, cap, d), lambda i, j, n_: (i * nj + j, 0, 0))],
    )
    return pl.pallas_call(
        functools.partial(_dispatch_kernel, tm=tm, cap=cap),
        grid_spec=grid_spec,
        out_shape=[jax.ShapeDtypeStruct((b * nj, cap, d + LANES), BF16),
                   jax.ShapeDtypeStruct((b * nj, cap, d), BF16)],
        compiler_params=_params("arbitrary", "arbitrary"),
        name="moe_dispatch",
    )(nk, h2, slotk, wk)


def _expert_kernel(te_ref, src_ref, dst_ref, nt_ref, xs_hbm, wg_ref, wu_ref, wd_ref, os_zero_hbm,
                   os_hbm, xbuf, obuf, wgb, wub, wdb, sem):
    del os_zero_hbm
    i = pl.program_id(0)
    nt = nt_ref[0]

    def in_copies(tile, slot):
        return [pltpu.make_async_copy(xs_hbm.at[src_ref[tile * TILE_CHUNKS + q]], xbuf.at[slot, q],
                                      sem.at[0, slot]) for q in range(TILE_CHUNKS)]

    def out_copies(tile, slot):
        return [pltpu.make_async_copy(obuf.at[slot, q], os_hbm.at[dst_ref[tile * TILE_CHUNKS + q]],
                                      sem.at[1, slot]) for q in range(TILE_CHUNKS)]

    @pl.when(i < nt)
    def _():
        slot = lax.rem(i, 2)

        @pl.when(i == 0)
        def _():
            for cpy in in_copies(0, 0):
                cpy.start()

        @pl.when(i + 1 < nt)
        def _():
            for cpy in in_copies(i + 1, 1 - slot):
                cpy.start()

        @pl.when(jnp.logical_or(i == 0, te_ref[i] != te_ref[jnp.maximum(i - 1, 0)]))
        def _():
            wgb[...] = wg_ref[0].astype(BF16)
            wub[...] = wu_ref[0].astype(BF16)
            wdb[...] = wd_ref[0].astype(BF16)

        for cpy in in_copies(i, slot):
            cpy.wait()
        outs = []
        for half in range(TILE_CHUNKS // SUB_CHUNKS):
            csl = slice(half * SUB_CHUNKS, (half + 1) * SUB_CHUNKS)
            x = xbuf[slot, csl].reshape(ROW_TILE, D_MODEL + LANES)
            xb = x[:, :D_MODEL]
            wrow = jnp.sum(x[:, D_MODEL:].astype(F32), axis=-1, keepdims=True)
            gte = _dot(xb, wgb[...])
            upe = _dot(xb, wub[...])
            act = (_silu(gte) * upe * wrow).astype(BF16)
            outs.append(_dot(act, wdb[...]).astype(BF16))

        @pl.when(i >= 2)
        def _():
            for cpy in out_copies(i - 2, slot):
                cpy.wait()

        for half, out in enumerate(outs):
            csl = slice(half * SUB_CHUNKS, (half + 1) * SUB_CHUNKS)
            obuf[slot, csl] = out.reshape(SUB_CHUNKS, CHUNK_ROWS, D_MODEL)
        for cpy in out_copies(i, slot):
            cpy.start()

        @pl.when(i == nt - 1)
        def _():
            @pl.when(i >= 1)
            def _():
                for cpy in out_copies(i - 1, 1 - slot):
                    cpy.wait()

            for cpy in out_copies(i, slot):
                cpy.wait()


def _experts(tile_expert, src, dst, n_tiles, xs_chunks, os_zero, w_gate_e, w_up_e, w_down_e, max_tiles):
    n_chunks, _, width = xs_chunks.shape
    d = width - LANES
    grid_spec = pltpu.PrefetchScalarGridSpec(
        num_scalar_prefetch=4,
        grid=(max_tiles,),
        in_specs=[pl.BlockSpec(memory_space=pl.ANY),
                  pl.BlockSpec((1, d, F_EXPERT), lambda i, te, s_, d_, n_: (te[i], 0, 0)),
                  pl.BlockSpec((1, d, F_EXPERT), lambda i, te, s_, d_, n_: (te[i], 0, 0)),
                  pl.BlockSpec((1, F_EXPERT, d), lambda i, te, s_, d_, n_: (te[i], 0, 0)),
                  pl.BlockSpec(memory_space=pl.ANY)],
        out_specs=pl.BlockSpec(memory_space=pl.ANY),
        scratch_shapes=[pltpu.VMEM((2, TILE_CHUNKS, CHUNK_ROWS, width), BF16),
                        pltpu.VMEM((2, TILE_CHUNKS, CHUNK_ROWS, d), BF16),
                        pltpu.VMEM((d, F_EXPERT), BF16), pltpu.VMEM((d, F_EXPERT), BF16),
                        pltpu.VMEM((F_EXPERT, d), BF16),
                        pltpu.SemaphoreType.DMA((2, 2))],
    )
    return pl.pallas_call(
        _expert_kernel,
        grid_spec=grid_spec,
        out_shape=jax.ShapeDtypeStruct((n_chunks, CHUNK_ROWS, d), BF16),
        input_output_aliases={8: 0},
        compiler_params=_params("arbitrary"),
        name="moe_experts",
    )(tile_expert, src, dst, n_tiles, xs_chunks, w_gate_e, w_up_e, w_down_e, os_zero)


def _combine_kernel(nk_ref, cum_ref, info_ref, h_ref, x1_ref, g2_ref, os_hbm, wgs, wus, wds, gpost,
                    y_o, obuf, sem, *, tm):
    nj = pl.num_programs(1)
    blk = pl.program_id(0) * nj + pl.program_id(1)
    nblk = pl.num_programs(0) * nj
    nk = nk_ref[blk]

    def fetch(bk, kt, slot):
        rows = pl.ds(pl.multiple_of(kt * ROW_TILE, ROW_TILE), ROW_TILE)
        return pltpu.make_async_copy(os_hbm.at[bk, rows], obuf.at[slot], sem.at[slot])

    ahead = COMBINE_RING - 1
    assert TOP_K * tm >= ahead * ROW_TILE

    @pl.when(blk == 0)
    def _():
        for t in range(ahead):
            fetch(0, t, t).start()

    hb = h_ref[0]
    hid = _silu(_dot(hb, wgs[...])) * _dot(hb, wus[...])
    acc0 = _dot(hid.astype(BF16), wds[...])
    info = info_ref[0]
    slot_cols = [jnp.broadcast_to(info[:, N_EXPERTS + k:N_EXPERTS + k + 1], (tm, ROW_TILE))
                 for k in range(TOP_K)]
    lane_row = lax.broadcasted_iota(jnp.int32, (tm, ROW_TILE), 1).astype(F32)

    def body(kt, acc):
        g = cum_ref[blk] + kt
        slot = lax.rem(g, COMBINE_RING)
        nslot = lax.rem(g + ahead, COMBINE_RING)

        @pl.when(kt + ahead < nk)
        def _():
            fetch(blk, kt + ahead, nslot).start()

        @pl.when(jnp.logical_and(kt + ahead >= nk, blk + 1 < nblk))
        def _():
            fetch(blk + 1, kt + ahead - nk, nslot).start()

        fetch(blk, kt, slot).wait()
        rows = lane_row + (kt * ROW_TILE).astype(F32)
        pt = jnp.zeros((tm, ROW_TILE), F32)
        for col in slot_cols:
            pt = jnp.where(rows == col, 1.0, pt)
        return acc + _dot(pt.astype(BF16), obuf[slot])

    acc = lax.fori_loop(0, nk, body, acc0)
    y_o[0] = x1_ref[0] + g2_ref[0] * _rms(acc, gpost[...])


def _combine(nk, cum_nk, info, h2, x1, g2, os_blocks, shared, gpost, tm):
    b, s, d = h2.shape
    nj = s // tm

    def tok(n):
        return pl.BlockSpec((1, tm, n), lambda i, j, *_: (i, j, 0))

    grid_spec = pltpu.PrefetchScalarGridSpec(
        num_scalar_prefetch=2,
        grid=(b, nj),
        in_specs=[tok(LANES), tok(d), tok(d), pl.BlockSpec((1, 1, d), lambda i, j, *_: (i, 0, 0)),
                  pl.BlockSpec(memory_space=pl.ANY)]
        + [_const_spec(w) for w in shared] + [_const_spec(gpost)],
        out_specs=tok(d),
        scratch_shapes=[pltpu.VMEM((COMBINE_RING, ROW_TILE, d), BF16),
                        pltpu.SemaphoreType.DMA((COMBINE_RING,))],
    )
    return pl.pallas_call(
        functools.partial(_combine_kernel, tm=tm),
        grid_spec=grid_spec,
        out_shape=jax.ShapeDtypeStruct((b, s, d), F32),
        compiler_params=_params("arbitrary", "arbitrary"),
        name="moe_combine",
    )(nk, cum_nk, info, h2, x1, g2, os_blocks, *shared, gpost)


def _moe_plan(pc_rep, cap_chunks, max_tiles):
    pc = pc_rep[:, :, 0].astype(jnp.int32)
    nblk = pc.shape[0]
    nch = pc // CHUNK_ROWS
    off = jnp.cumsum(nch, axis=1) - nch
    nk = jnp.sum(pc, axis=1) // ROW_TILE
    cum_nk = jnp.cumsum(nk) - nk
    cum_blk = jnp.cumsum(nch, axis=0)
    tot = cum_blk[-1]
    tiles = (tot + TILE_CHUNKS - 1) // TILE_CHUNKS
    tile_end = jnp.cumsum(tiles)
    n_tiles = tile_end[-1]
    tile_ids = jnp.arange(max_tiles, dtype=jnp.int32)
    last_expert = jnp.sum((tile_end < n_tiles).astype(jnp.int32))
    te = jnp.sum((tile_end[None, :] <= tile_ids[:, None]).astype(jnp.int32), axis=1)
    te = jnp.minimum(te, last_expert)
    oh_e = (te[:, None] == jnp.arange(N_EXPERTS, dtype=jnp.int32)[None, :]).astype(jnp.int32)

    def by_expert(v):
        return jnp.sum(oh_e * v[None, :], axis=1) if v.ndim == 1 else jnp.sum(
            oh_e[:, :, None] * v[None, :, :], axis=1)

    pos = ((tile_ids - by_expert(tile_end - tiles))[:, None] * TILE_CHUNKS
           + jnp.arange(TILE_CHUNKS, dtype=jnp.int32)[None, :])
    valid = jnp.logical_and(tile_ids[:, None] < n_tiles, pos < by_expert(tot)[:, None])
    cum_e = by_expert(cum_blk.T)
    nch_e = by_expert(nch.T)
    off_e = by_expert(off.T)
    blk = jnp.sum((cum_e[:, None, :] <= pos[:, :, None]).astype(jnp.int32), axis=-1)
    blk = jnp.minimum(blk, nblk - 1)
    oh_b = (blk[:, :, None] == jnp.arange(nblk, dtype=jnp.int32)[None, None, :]).astype(jnp.int32)
    before = jnp.sum(oh_b * (cum_e - nch_e)[:, None, :], axis=-1)
    local = jnp.sum(oh_b * off_e[:, None, :], axis=-1) + (pos - before)
    chunk = blk * cap_chunks + local
    assert nblk * SUB_CHUNKS >= TILE_CHUNKS
    q = jnp.arange(TILE_CHUNKS, dtype=jnp.int32)[None, :]
    spare = (q // SUB_CHUNKS) * cap_chunks + cap_chunks - SUB_CHUNKS + q % SUB_CHUNKS
    src = jnp.where(valid, chunk, cap_chunks - 1)
    dst = jnp.where(valid, chunk, spare)
    return (te.astype(jnp.int32), src.reshape(-1).astype(jnp.int32), dst.reshape(-1).astype(jnp.int32),
            n_tiles.reshape(1).astype(jnp.int32), nk.astype(jnp.int32), cum_nk.astype(jnp.int32))


def _sparse_moe(h2, info, slotk, wk, pc_rep, x1, g2, w_gate_e, w_up_e, w_down_e, shared, gpost, tm):
    b, s, d = h2.shape
    nblk = b * (s // tm)
    cap = TOP_K * tm + N_EXPERTS * CHUNK_ROWS + ROW_TILE
    cap = -(-cap // ROW_TILE) * ROW_TILE
    cap_chunks = cap // CHUNK_ROWS
    max_tiles = nblk * (cap_chunks - SUB_CHUNKS) // TILE_CHUNKS + N_EXPERTS
    te, src, dst, n_tiles, nk, cum_nk = _moe_plan(pc_rep, cap_chunks, max_tiles)
    xs, os_zero = _dispatch(nk, h2, slotk, wk, tm, cap)
    os_chunks = _experts(te, src, dst, n_tiles, xs.reshape(nblk * cap_chunks, CHUNK_ROWS, d + LANES),
                         os_zero.reshape(nblk * cap_chunks, CHUNK_ROWS, d),
                         w_gate_e, w_up_e, w_down_e, max_tiles)
    return _combine(nk, cum_nk, info, h2, x1, g2, os_chunks.reshape(nblk, cap, d), shared, gpost, tm)


def _moe_kernel(h_ref, c_ref, x1_ref, g2_ref, wg_ref, wu_ref, wd_ref, wgs, wus, wds, gpost,
                y_o, acc_ref, *, tm):
    e = pl.program_id(2)
    ne = pl.num_programs(2)
    hb = h_ref[0]

    @pl.when(e == 0)
    def _():
        hid = _silu(_dot(hb, wgs[...])) * _dot(hb, wus[...])
        acc_ref[...] = _dot(hid.astype(BF16), wds[...])

    lane = lax.broadcasted_iota(jnp.int32, (tm, LANES), 1)
    ce = jnp.sum(jnp.where(lane == e, c_ref[0], 0.0), axis=-1, keepdims=True)
    gte = _dot(hb, wg_ref[0].astype(BF16))
    upe = _dot(hb, wu_ref[0].astype(BF16))
    act = (_silu(gte) * upe * ce).astype(BF16)
    acc_ref[...] += _dot(act, wd_ref[0].astype(BF16))

    @pl.when(e == ne - 1)
    def _():
        y_o[0] = x1_ref[0] + g2_ref[0] * _rms(acc_ref[...], gpost[...])


def _moe(h2, comb, x1, g2, w_gate_e, w_up_e, w_down_e, shared, gpost, tm):
    b, s, d = h2.shape
    per_row_mod = g2.shape[1] != 1
    mod_spec = (pl.BlockSpec((1, tm, d), lambda i, j, e: (i, j, 0)) if per_row_mod
                else pl.BlockSpec((1, 1, d), lambda i, j, e: (i, 0, 0)))

    def tok(n):
        return pl.BlockSpec((1, tm, n), lambda i, j, e: (i, j, 0))

    return pl.pallas_call(
        functools.partial(_moe_kernel, tm=tm),
        grid=(b, s // tm, N_EXPERTS),
        in_specs=[tok(d), tok(LANES), tok(d), mod_spec,
                  pl.BlockSpec((1, d, F_EXPERT), lambda i, j, e: (e, 0, 0)),
                  pl.BlockSpec((1, d, F_EXPERT), lambda i, j, e: (e, 0, 0)),
                  pl.BlockSpec((1, F_EXPERT, d), lambda i, j, e: (e, 0, 0))]
        + [_const_spec(w) for w in shared] + [_const_spec(gpost)],
        out_specs=tok(d),
        out_shape=jax.ShapeDtypeStruct((b, s, d), F32),
        scratch_shapes=[pltpu.VMEM((tm, d), F32)],
        compiler_params=_params("arbitrary", "arbitrary", "arbitrary"),
        name="moe",
    )(h2, comb, x1, g2, w_gate_e, w_up_e, w_down_e, *shared, gpost)


def _qlat_kernel(q_ref, wuk_ref, o_ref):
    for hd in range(A_HEADS):
        qn = q_ref[:, hd * HEAD_PAD:hd * HEAD_PAD + QK_NOPE]
        o_ref[hd] = _dot_nt(qn, wuk_ref[hd]).astype(BF16)


def _qlat(q, wuk_heads):
    nb = q.shape[0]
    return pl.pallas_call(
        _qlat_kernel,
        in_specs=[_const_spec(q), _const_spec(wuk_heads)],
        out_specs=pl.BlockSpec((A_HEADS, nb, KV_LORA), lambda: (0, 0, 0)),
        out_shape=jax.ShapeDtypeStruct((A_HEADS, nb, KV_LORA), BF16),
        grid=(),
        name="q_latent",
    )(q, wuk_heads)


def _decode_kernel(pt_ref, ql_ref, qp_ref, cn_ref, kn_ref, ckv_hbm, kpet_hbm, o_ref,
                   cbuf, kbuf, sem, *, pages_per_chunk, n_chunks):
    b = pl.program_id(0)
    nb = pl.num_programs(0)
    cp = pages_per_chunk
    halves = 2 if cp % 2 == 0 else 1
    hp = cp // halves
    total = nb * n_chunks
    ahead = DECODE_RING - 1

    def copies(g):
        g = jnp.asarray(g, jnp.int32)
        bb = lax.div(g, jnp.int32(n_chunks))
        chunk = g - bb * n_chunks
        slot = lax.rem(g, jnp.int32(DECODE_RING))
        out = []
        for p in range(cp):
            page = pt_ref[bb, chunk * cp + p]
            out.append(pltpu.make_async_copy(ckv_hbm.at[page], cbuf.at[slot, p], sem.at[0, slot]))
            out.append(pltpu.make_async_copy(kpet_hbm.at[page], kbuf.at[slot, p], sem.at[1, slot]))
        return out

    def start(g):
        for n, cpy in enumerate(copies(g)):
            cpy.start(priority=(n // 2) % 2)

    @pl.when(b == 0)
    def _():
        for g in range(ahead):
            @pl.when(g < total)
            def _():
                start(g)

    ql = ql_ref[0]
    qp = qp_ref[0]
    c_new = cn_ref[0]
    k_new = kn_ref[0]
    s_new = (jnp.sum(ql.astype(F32) * c_new, axis=-1, keepdims=True)
             + jnp.sum(qp.astype(F32) * k_new, axis=-1, keepdims=True)) * ATTN_SCALE
    m0 = s_new
    l0 = jnp.ones((A_HEADS, 1), F32)
    acc0 = jnp.broadcast_to(c_new, (A_HEADS, KV_LORA))

    def body(chunk, carry):
        m_prev, l_prev, acc = carry
        g = b * n_chunks + chunk
        slot = lax.rem(g, DECODE_RING)

        @pl.when(g + ahead < total)
        def _():
            start(g + ahead)

        for cpy in copies(g):
            cpy.wait()
        for hf in range(halves):
            kc = cbuf[slot, hf * hp:(hf + 1) * hp].reshape(hp * PAGE_SIZE, KV_LORA).astype(BF16)
            kt = jnp.concatenate([kbuf[slot, hf * hp + p] for p in range(hp)], axis=1).astype(BF16)
            s = (_dot_nt(ql, kc) + _dot(qp, kt)) * ATTN_SCALE
            m_new = jnp.maximum(m_prev, jnp.max(s, axis=-1, keepdims=True))
            alpha = jnp.exp(m_prev - m_new)
            p = jnp.exp(s - m_new)
            l_prev = alpha * l_prev + jnp.sum(p, axis=-1, keepdims=True)
            acc = alpha * acc + _dot(p.astype(BF16), kc)
            m_prev = m_new
        return m_prev, l_prev, acc

    _, l_fin, acc = lax.fori_loop(0, n_chunks, body, (m0, l0, acc0))
    o_ref[0] = acc / l_fin


def _decode_attention(page_table, qlat, qpe, ckv_new, kpe_new, cache_ckv, cache_kpe_t):
    nb, n_pages = page_table.shape
    cp = min(16, n_pages)
    n_chunks = n_pages // cp
    grid_spec = pltpu.PrefetchScalarGridSpec(
        num_scalar_prefetch=1,
        grid=(nb,),
        in_specs=[pl.BlockSpec((1, A_HEADS, KV_LORA), lambda i, pt: (i, 0, 0)),
                  pl.BlockSpec((1, A_HEADS, QK_ROPE), lambda i, pt: (i, 0, 0)),
                  pl.BlockSpec((1, 1, KV_LORA), lambda i, pt: (i, 0, 0)),
                  pl.BlockSpec((1, 1, QK_ROPE), lambda i, pt: (i, 0, 0)),
                  pl.BlockSpec(memory_space=pl.ANY),
                  pl.BlockSpec(memory_space=pl.ANY)],
        out_specs=pl.BlockSpec((1, A_HEADS, KV_LORA), lambda i, pt: (i, 0, 0)),
        scratch_shapes=[pltpu.VMEM((DECODE_RING, cp, PAGE_SIZE, KV_LORA), F32),
                        pltpu.VMEM((DECODE_RING, cp, QK_ROPE, PAGE_SIZE), F32),
                        pltpu.SemaphoreType.DMA((2, DECODE_RING))],
    )
    return pl.pallas_call(
        functools.partial(_decode_kernel, pages_per_chunk=cp, n_chunks=n_chunks),
        grid_spec=grid_spec,
        out_shape=jax.ShapeDtypeStruct((nb, A_HEADS, KV_LORA), F32),
        compiler_params=_params("arbitrary"),
        name="decode_attention",
    )(page_table, qlat, qpe, ckv_new, kpe_new, cache_ckv, cache_kpe_t)


def _vup_kernel(ol_ref, wuv_ref, o_ref):
    for hd in range(A_HEADS):
        o_ref[:, hd * V_DIM:(hd + 1) * V_DIM] = _dot(ol_ref[hd].astype(BF16), wuv_ref[hd]).astype(BF16)


def _value_up(o_lat_heads, wuv_heads):
    nb = o_lat_heads.shape[1]
    return pl.pallas_call(
        _vup_kernel,
        in_specs=[_const_spec(o_lat_heads), _const_spec(wuv_heads)],
        out_specs=pl.BlockSpec((nb, A_HEADS * V_DIM), lambda: (0, 0)),
        out_shape=jax.ShapeDtypeStruct((nb, A_HEADS * V_DIM), BF16),
        grid=(),
        name="value_up",
    )(o_lat_heads, wuv_heads)


def _ssd_step_kernel(xbc_ref, cs_ref, z_ref, dt_ref, st_ref, cw_ref, cb_ref, dtb_ref, na_ref,
                     dsk_ref, gs_ref, y_o, st_o, y_ref):
    conv = cb_ref[...] + cw_ref[CONV_W - 1:CONV_W, :] * xbc_ref[0]
    for j in range(CONV_W - 1):
        conv = conv + cw_ref[j:j + 1, :] * cs_ref[0, j:j + 1, :]
    xbc = _silu(conv)
    dt = _softplus(dt_ref[0] + dtb_ref[...])
    dec = jnp.exp(dt * na_ref[...])
    hi, mid, lo = (part.astype(F32) for part in _split3(xbc[:, :D_INNER]))
    r8 = lax.broadcasted_iota(jnp.int32, (8, D_INNER), 0)
    x8 = jnp.where(r8 == 0, hi, jnp.where(r8 == 1, mid, jnp.where(r8 == 2, lo, 0.0))).astype(BF16)
    ones8 = (lax.broadcasted_iota(jnp.int32, (8, D_STATE), 0) < 3).astype(BF16)
    x_col = _dot_tn(x8, ones8)
    for hd in range(S_HEADS):
        g = hd // GROUP_HEADS
        hsl = slice(hd * S_HEAD_DIM, (hd + 1) * S_HEAD_DIM)
        bm = xbc[:, D_INNER + g * D_STATE:D_INNER + (g + 1) * D_STATE]
        cm = xbc[:, D_INNER + (S_GROUPS + g) * D_STATE:D_INNER + (S_GROUPS + g + 1) * D_STATE]
        new = st_ref[0, hd] * dec[:, hd:hd + 1] + x_col[hsl, :] * (bm * dt[:, hd:hd + 1])
        st_o[0, hd] = new
        cb8 = jnp.broadcast_to(cm, (8, D_STATE)).astype(BF16)
        y_ref[:, hsl] = _dot_nt(cb8, new.astype(BF16))
    y = y_ref[0:1, :] + dsk_ref[...] * xbc[:, :D_INNER]
    y = y * _silu(z_ref[0])
    for g in range(S_GROUPS):
        gsl = slice(g * GROUP_CH, (g + 1) * GROUP_CH)
        y_o[0, :, gsl] = _rms(y[:, gsl], gs_ref[:, gsl]).astype(BF16)


def _ssd_sample(xbc, conv_state, z, dt, ssm_state, conv_w, conv_b, dtb_row, na_row, dskip, g_ssd):
    nb = xbc.shape[0]
    consts = (conv_w, conv_b, dtb_row, na_row, dskip, g_ssd)
    st_spec = pl.BlockSpec((1, S_HEADS, S_HEAD_DIM, D_STATE), lambda i: (i, 0, 0, 0))
    return pl.pallas_call(
        _ssd_step_kernel,
        grid=(nb,),
        in_specs=[pl.BlockSpec((1, 1, CONV_CH), lambda i: (i, 0, 0)),
                  pl.BlockSpec((1, CONV_W - 1, CONV_CH), lambda i: (i, 0, 0)),
                  pl.BlockSpec((1, 1, D_INNER), lambda i: (i, 0, 0)),
                  pl.BlockSpec((1, 1, LANES), lambda i: (i, 0, 0)),
                  st_spec] + [_const_spec(a) for a in consts],
        out_specs=[pl.BlockSpec((1, 1, D_INNER), lambda i: (i, 0, 0)), st_spec],
        out_shape=[jax.ShapeDtypeStruct((nb, 1, D_INNER), BF16),
                   jax.ShapeDtypeStruct(ssm_state.shape, F32)],
        scratch_shapes=[pltpu.VMEM((8, D_INNER), F32)],
        compiler_params=_params("arbitrary"),
        name="ssd_sample",
    )(xbc, conv_state, z, dt, ssm_state, *consts)


def _rot_half(w):
    half = QK_ROPE // 2
    return jnp.concatenate([-w[..., half:], w[..., :half]], axis=-1)


def _pad_cols(w, start, total):
    return jnp.pad(w, ((0, 0), (start, total - start - w.shape[1])))


def _head_pad(w_nope, w_rope):
    k = w_nope.shape[0]
    pad = jnp.zeros((k, A_HEADS, HEAD_PAD - QK_NOPE - QK_ROPE), w_nope.dtype)
    return jnp.concatenate([w_nope, w_rope, pad], axis=-1).reshape(k, A_HEADS * HEAD_PAD)


def _rope_tables(pos):
    half = QK_ROPE // 2
    inv = ROPE_BASE ** (-jnp.arange(half, dtype=F32) / half)
    ang = pos.astype(F32)[:, None] * inv[None, :]
    cos, sin = jnp.cos(ang), jnp.sin(ang)
    n = pos.shape[0]
    ctab = jnp.concatenate([jnp.ones((n, QK_NOPE), F32), cos, cos,
                            jnp.zeros((n, HEAD_PAD - QK_NOPE - QK_ROPE), F32)], axis=1)
    stab = jnp.concatenate([jnp.zeros((n, QK_NOPE), F32), sin, sin,
                            jnp.zeros((n, HEAD_PAD - QK_NOPE - QK_ROPE), F32)], axis=1)
    return ctab, stab


def _pick_tile(n, target):
    t = min(n, target)
    assert n % t == 0, (n, t)
    return t


def kernel(x_prompt, x_sample, cache_ckv, cache_kpe, state_conv, state_ssm, page_table, c_prompt, c_sample, w_ada, b_ada, g_pre_mix, g_post_mix, g_pre_ffn, g_post_ffn, w_in, g_q_a, w_q_b, g_kv_a, w_uk, w_uv, w_o_attn, conv_w, conv_b, dt_bias, a_log, d_skip, g_ssd, w_o_ssd, w_out, w_router, b_router, w_gate_e, w_up_e, w_down_e, w_gate_s, w_up_s, w_down_s):
    bp, sp, d = x_prompt.shape
    nb, ds, _ = x_sample.shape
    depth = w_in.shape[0]
    assert depth == 1 and ds == 1 and d == D_MODEL
    n_pages = page_table.shape[1]
    past_len = n_pages * PAGE_SIZE
    lyr = 0

    offs = [0]
    for n in IN_SPLITS:
        offs.append(offs[-1] + n)
    win = w_in[lyr]
    w_qa, w_kva, w_kpe, w_z, w_xbc, w_dt, w_ga, w_gb = (win[:, offs[i]:offs[i + 1]] for i in range(8))
    wqb = w_q_b[lyr].reshape(Q_LORA, A_HEADS, QK_NOPE + QK_ROPE)
    wq_pad = _head_pad(wqb[..., :QK_NOPE], wqb[..., QK_NOPE:])
    wq_rot = _head_pad(jnp.zeros_like(wqb[..., :QK_NOPE]), _rot_half(wqb[..., QK_NOPE:]))
    wuk_pad = _head_pad(w_uk[lyr], jnp.zeros((KV_LORA, A_HEADS, QK_ROPE), F32))
    premix_w = [
        w_qa.astype(BF16), w_kva.astype(BF16),
        _pad_cols(w_kpe, QK_NOPE, LANES).astype(BF16),
        _pad_cols(_rot_half(w_kpe), QK_NOPE, LANES).astype(BF16),
        w_z.astype(BF16), w_xbc.astype(BF16),
        _pad_cols(w_dt, 0, LANES).astype(BF16), w_dt.T.astype(BF16),
        w_ga.astype(BF16), w_gb.astype(BF16),
        g_q_a[lyr][None, :], wq_pad.astype(BF16), wq_rot.astype(BF16),
        g_kv_a[lyr][None, :], wuk_pad.astype(BF16),
        w_uv[lyr].reshape(KV_LORA, A_HEADS * V_DIM).astype(BF16),
    ]
    merge_w = [w_o_attn[lyr].astype(BF16), w_o_ssd[lyr].astype(BF16), w_out[lyr].astype(BF16),
               g_post_mix[lyr][None, :], g_pre_ffn[lyr][None, :]]
    shared_w = [w_gate_s[lyr].astype(BF16), w_up_s[lyr].astype(BF16), w_down_s[lyr].astype(BF16)]
    w_router_t = w_router[lyr].T.astype(BF16)
    b_router_col = b_router[lyr][:, None]
    g_pre = g_pre_mix[lyr][None, :]
    g_post_ffn_row = g_post_ffn[lyr][None, :]
    cw = conv_w[lyr]
    cb = conv_b[lyr][None, :]
    neg_a = -jnp.exp(a_log[lyr].astype(F32))
    dtb_row = jnp.pad(dt_bias[lyr], (0, LANES - S_HEADS))[None, :]
    na_row = jnp.pad(neg_a, (0, LANES - S_HEADS))[None, :]
    dtb_col = dt_bias[lyr][:, None]
    na_col = neg_a[:, None]
    dskip_row = jnp.repeat(d_skip[lyr].astype(F32), S_HEAD_DIM)[None, :]
    gssd_row = g_ssd[lyr][None, :]

    n_mod_rows = bp + nb
    pad_rows = (-n_mod_rows) % 16
    c_all = jnp.concatenate([c_prompt, c_sample, jnp.zeros((pad_rows, d), F32)], axis=0)
    mod = _adaln(c_all, w_ada[lyr], b_ada[lyr][None, :])
    mods = [mod[:, i * d:(i + 1) * d] for i in range(6)]
    mods_p = [m[:bp][:, None, :] for m in mods]
    mods_s = [m[bp:bp + nb][None, :, :] for m in mods]

    ctab_p, stab_p = _rope_tables(jnp.arange(sp))
    tm = _pick_tile(sp, 256)
    (q_p, k_p, v_p, ckv_p, kpe_pad_p, z_p, xbc_p, dt_p, dtt_p, ga_p, gb_p) = _premix(
        x_prompt, mods_p[0], mods_p[1], g_pre, ctab_p, stab_p, premix_w, tm)
    o_attn_p = _flash_attention(q_p, k_p, v_p, _pick_tile(sp, 512), 2)
    y_ssd_p, ssm_t_p = _ssd_prompt(xbc_p, z_p, dt_p, dtt_p, cw, cb, dtb_row, dtb_col, na_row, na_col,
                                   dskip_row, gssd_row)
    x1_p, h2_p = _merge(x_prompt, o_attn_p, y_ssd_p, ga_p, gb_p, mods_p[2], mods_p[3], mods_p[4],
                        merge_w, _pick_tile(sp, 512))
    tb = _pick_tile(sp, MOE_BLOCK)
    info_p, slotk_p, wk_p, pc_p = _router(h2_p, w_router_t, b_router_col, tb)
    y_prompt = _sparse_moe(h2_p, info_p, slotk_p, wk_p, pc_p, x1_p, mods_p[5],
                           w_gate_e[lyr], w_up_e[lyr], w_down_e[lyr], shared_w, g_post_ffn_row, tb)

    xs = x_sample.reshape(1, nb, d)
    ctab_s, stab_s = _rope_tables(past_len + jnp.arange(ds))
    (q_s, _, _, ckv_s, kpe_pad_s, z_s, xbc_s, dt_s, _, ga_s, gb_s) = _premix(
        xs, mods_s[0], mods_s[1], g_pre, ctab_s, stab_s, premix_w, nb)
    kpe_s = kpe_pad_s[0, :, QK_NOPE:QK_NOPE + QK_ROPE]
    wuk_heads = jnp.transpose(w_uk[lyr], (1, 0, 2)).astype(BF16)
    wuv_heads = jnp.transpose(w_uv[lyr], (1, 0, 2)).astype(BF16)
    qlat = jnp.transpose(_qlat(q_s[0], wuk_heads), (1, 0, 2))
    qpe = q_s[0].reshape(nb, A_HEADS, HEAD_PAD)[:, :, QK_NOPE:QK_NOPE + QK_ROPE]
    o_lat = _decode_attention(page_table, qlat, qpe,
                              ckv_s.reshape(nb, 1, KV_LORA), kpe_s.reshape(nb, 1, QK_ROPE),
                              cache_ckv.reshape(cache_ckv.shape[1:]),
                              jnp.swapaxes(cache_kpe.reshape(cache_kpe.shape[1:]), 1, 2))
    o_attn_s = _value_up(jnp.transpose(o_lat, (1, 0, 2)), wuv_heads)[None]
    y_ssd_s, ssm_s = _ssd_sample(xbc_s.reshape(nb, 1, CONV_CH), state_conv[lyr],
                                 z_s.reshape(nb, 1, D_INNER), dt_s.reshape(nb, 1, LANES),
                                 state_ssm.reshape(state_ssm.shape[1:]), cw, cb, dtb_row, na_row,
                                 dskip_row, gssd_row)
    x1_s, h2_s = _merge(xs, o_attn_s, y_ssd_s.reshape(1, nb, D_INNER), ga_s, gb_s,
                        mods_s[2], mods_s[3], mods_s[4], merge_w, nb)
    comb_s = _router(h2_s, w_router_t, b_router_col, nb)[0]
    y_s = _moe(h2_s, comb_s, x1_s, mods_s[5], w_gate_e[lyr], w_up_e[lyr], w_down_e[lyr],
               shared_w, g_post_ffn_row, nb)

    kpe_p = kpe_pad_p[:, :, QK_NOPE:QK_NOPE + QK_ROPE]
    conv_p = xbc_p[:, sp - (CONV_W - 1):, :]
    ssm_p = jnp.transpose(ssm_t_p.reshape(bp, D_STATE, S_HEADS, S_HEAD_DIM), (0, 2, 3, 1))
    conv_s = jnp.concatenate([state_conv[lyr][:, 1:, :], xbc_s.reshape(nb, 1, CONV_CH)], axis=1)
    return (y_prompt, y_s.reshape(nb, ds, d),
            ckv_p[None], kpe_p[None], conv_p[None], ssm_p[None].astype(x_prompt.dtype),
            ckv_s.reshape(1, nb, ds, KV_LORA), kpe_s.reshape(1, nb, ds, QK_ROPE),
            conv_s[None], ssm_s[None].astype(x_sample.dtype))
```

```python
import functools

import jax
import jax.numpy as jnp
from jax import lax
from jax.experimental import pallas as pl
from jax.experimental.pallas import tpu as pltpu

F32 = jnp.float32
BF16 = jnp.bfloat16

D_MODEL = 1024
PAGE_SIZE = 128
A_HEADS = 8
QK_NOPE = 64
QK_ROPE = 32
V_DIM = 64
Q_LORA = 384
KV_LORA = 256
ROPE_BASE = 10000.0
ATTN_SCALE = (QK_NOPE + QK_ROPE) ** -0.5
S_HEADS = 16
S_HEAD_DIM = 64
D_INNER = S_HEADS * S_HEAD_DIM
S_GROUPS = 2
D_STATE = 128
CONV_W = 4
CONV_CH = D_INNER + 2 * S_GROUPS * D_STATE
CHUNK = 128
N_EXPERTS = 64
TOP_K = 6
N_EXP_GROUPS = 8
TOPK_GROUPS = 4
F_EXPERT = 256
ROUTE_SCALE = 2.5
EPS = 1e-6
IN_SPLITS = (Q_LORA, KV_LORA, QK_ROPE, D_INNER, CONV_CH, S_HEADS, D_MODEL, D_MODEL)

LANES = 128
HEAD_PAD = LANES
GROUP_HEADS = S_HEADS // S_GROUPS
GROUP_CH = D_INNER // S_GROUPS
EXP_PER_GROUP = N_EXPERTS // N_EXP_GROUPS
MOE_BLOCK = 256
CHUNK_ROWS = 16
ROW_TILE = 256
SUB_CHUNKS = ROW_TILE // CHUNK_ROWS
TILE_CHUNKS = 2 * SUB_CHUNKS
DECODE_RING = 4
COMBINE_RING = 4
VMEM_LIMIT_BYTES = 56 * 1024 * 1024

NEG_INF = float("-inf")
LOG2_E = 1.4426950408889634


def _params(*semantics):
    return pltpu.CompilerParams(dimension_semantics=semantics, vmem_limit_bytes=VMEM_LIMIT_BYTES)


def _const_spec(arr):
    nd = arr.ndim
    return pl.BlockSpec(arr.shape, lambda *_: (0,) * nd)


def _dot(a, b):
    return jnp.dot(a, b, preferred_element_type=F32)


def _dot_nt(a, b):
    return lax.dot_general(a, b, (((1,), (1,)), ((), ())), preferred_element_type=F32)


def _dot_tn(a, b):
    return lax.dot_general(a, b, (((0,), (0,)), ((), ())), preferred_element_type=F32)


def _rms(x, g):
    return x * lax.rsqrt(jnp.mean(x * x, axis=-1, keepdims=True) + EPS) * g


def _silu(x):
    return x * (1.0 / (1.0 + jnp.exp(-x)))


def _sigmoid(x):
    return 1.0 / (1.0 + jnp.exp(-x))


def _softplus(x):
    return jnp.maximum(x, 0.0) + jnp.log(1.0 + jnp.exp(-jnp.abs(x)))


def _split3(x):
    hi = x.astype(BF16)
    r1 = x - hi.astype(F32)
    mid = r1.astype(BF16)
    lo = (r1 - mid.astype(F32)).astype(BF16)
    return hi, mid, lo


def _adaln_kernel(c_ref, w_ref, b_ref, o_ref):
    c = _silu(c_ref[...]).astype(BF16)
    o_ref[...] = _dot(c, w_ref[...].astype(BF16)) + b_ref[...]


def _adaln(c_all, w_ada, b_ada):
    rows, d = c_all.shape
    n = w_ada.shape[1]
    tn = 512
    return pl.pallas_call(
        _adaln_kernel,
        grid=(n // tn,),
        in_specs=[pl.BlockSpec((rows, d), lambda j: (0, 0)),
                  pl.BlockSpec((d, tn), lambda j: (0, j)),
                  pl.BlockSpec((1, tn), lambda j: (0, j))],
        out_specs=pl.BlockSpec((rows, tn), lambda j: (0, j)),
        out_shape=jax.ShapeDtypeStruct((rows, n), F32),
        compiler_params=_params("arbitrary"),
        name="adaln",
    )(c_all, w_ada, b_ada)


def _premix_kernel(x_ref, sh_ref, sc_ref, g_ref, ct_ref, st_ref,
                   wqa, wkva, wkpe, wkper, wz, wxbc, wdt, wdtt, wga, wgb,
                   gqa, wq, wqr, gkv, wuk, wuv,
                   q_o, k_o, v_o, ckv_o, kpe_o, z_o, xbc_o, dt_o, dtt_o, ga_o, gb_o):
    x = x_ref[0]
    h = _rms(x, g_ref[...]) * (1.0 + sc_ref[0]) + sh_ref[0]
    hb = h.astype(BF16)
    ct = ct_ref[...]
    st = st_ref[...]

    qn = _rms(_dot(hb, wqa[...]), gqa[...]).astype(BF16)
    q = _dot(qn, wq[...])
    qr = _dot(qn, wqr[...])
    for hd in range(A_HEADS):
        sl = slice(hd * HEAD_PAD, (hd + 1) * HEAD_PAD)
        q_o[0, :, sl] = (q[:, sl] * ct + qr[:, sl] * st).astype(BF16)

    ckv = _rms(_dot(hb, wkva[...]), gkv[...])
    ckv_o[0] = ckv
    cb = ckv.astype(BF16)
    kpe = _dot(hb, wkpe[...]) * ct + _dot(hb, wkper[...]) * st
    kpe_o[0] = kpe
    kn = _dot(cb, wuk[...])
    for hd in range(A_HEADS):
        sl = slice(hd * HEAD_PAD, (hd + 1) * HEAD_PAD)
        k_o[0, :, sl] = (kn[:, sl] + kpe).astype(BF16)
    v_o[0] = _dot(cb, wuv[...]).astype(BF16)

    z_o[0] = _dot(hb, wz[...])
    xbc_o[0] = _dot(hb, wxbc[...])
    dt_o[0] = _dot(hb, wdt[...])
    dtt_o[0] = _dot_nt(wdtt[...], hb)
    ga_o[0] = _dot(hb, wga[...])
    gb_o[0] = _dot(hb, wgb[...])


def _premix(x, sh, sc, g, ctab, stab, weights, tm):
    b, s, d = x.shape
    per_row_mod = sh.shape[1] != 1
    per_row_tab = ctab.shape[0] != 1
    mod_spec = (pl.BlockSpec((1, tm, d), lambda i, j: (i, j, 0)) if per_row_mod
                else pl.BlockSpec((1, 1, d), lambda i, j: (i, 0, 0)))
    tab_spec = (pl.BlockSpec((tm, LANES), lambda i, j: (j, 0)) if per_row_tab
                else pl.BlockSpec((1, LANES), lambda i, j: (0, 0)))

    def tok(n):
        return pl.BlockSpec((1, tm, n), lambda i, j: (i, j, 0))

    out_cols = [(A_HEADS * HEAD_PAD, BF16), (A_HEADS * HEAD_PAD, BF16), (A_HEADS * V_DIM, BF16),
                (KV_LORA, F32), (LANES, F32), (D_INNER, F32), (CONV_CH, F32), (LANES, F32)]
    out_shape = [jax.ShapeDtypeStruct((b, s, n), dt) for n, dt in out_cols]
    out_specs = [tok(n) for n, _ in out_cols]
    out_shape.append(jax.ShapeDtypeStruct((b, S_HEADS, s), F32))
    out_specs.append(pl.BlockSpec((1, S_HEADS, tm), lambda i, j: (i, 0, j)))
    out_shape += [jax.ShapeDtypeStruct((b, s, D_MODEL), F32)] * 2
    out_specs += [tok(D_MODEL)] * 2
    order = [out_shape[i] for i in (0, 1, 2, 3, 4, 5, 6, 7, 8, 9, 10)]
    return pl.pallas_call(
        _premix_kernel,
        grid=(b, s // tm),
        in_specs=[tok(d), mod_spec, mod_spec, _const_spec(g), tab_spec, tab_spec]
        + [_const_spec(w) for w in weights],
        out_specs=out_specs,
        out_shape=order,
        compiler_params=_params("arbitrary", "arbitrary"),
        name="premix",
    )(x, sh, sc, g, ctab, stab, *weights)


def _flash_kernel(q_ref, k_ref, v_ref, o_ref, m_ref, l_ref, acc_ref, *, tq, heads):
    qi = pl.program_id(2)
    m_ref[...] = jnp.full(m_ref.shape, NEG_INF, F32)
    l_ref[...] = jnp.zeros(l_ref.shape, F32)
    acc_ref[...] = jnp.zeros(acc_ref.shape, F32)
    row = lax.broadcasted_iota(jnp.int32, (tq, tq), 0)
    col = lax.broadcasted_iota(jnp.int32, (tq, tq), 1)
    c2 = ATTN_SCALE * LOG2_E

    def tile(kj, on_diagonal):
        koff = pl.multiple_of(kj * tq, tq)
        for hh in range(heads):
            sl = slice(hh * HEAD_PAD, (hh + 1) * HEAD_PAD)
            vsl = slice((hh // 2) * 2 * V_DIM, (hh // 2 + 1) * 2 * V_DIM)
            s = _dot_nt(q_ref[0, :, sl], k_ref[0, pl.ds(koff, tq), sl])
            if on_diagonal:
                s = jnp.where(col <= row, s, NEG_INF)
            m_prev = m_ref[hh]
            m_new = jnp.maximum(m_prev, jnp.max(s, axis=-1, keepdims=True))
            alpha = jnp.exp2((m_prev - m_new) * c2)
            p_parts = [jnp.exp2((s[:, j * LANES:(j + 1) * LANES] - m_new) * c2)
                       for j in range(tq // LANES)]
            l_ref[hh] = alpha * l_ref[hh] + sum(p_parts)
            p = jnp.concatenate(p_parts, axis=1).astype(BF16)
            acc_ref[hh] = alpha * acc_ref[hh] + _dot(p, v_ref[0, pl.ds(koff, tq), vsl])
            m_ref[hh] = m_new

    def body(kj, carry):
        tile(kj, False)
        return carry

    lax.fori_loop(0, qi, body, 0)
    tile(qi, True)
    lane = lax.broadcasted_iota(jnp.int32, (tq, 2 * V_DIM), 1)
    for pair in range(heads // 2):
        o0 = acc_ref[2 * pair] / jnp.sum(l_ref[2 * pair], axis=-1, keepdims=True)
        o1 = acc_ref[2 * pair + 1] / jnp.sum(l_ref[2 * pair + 1], axis=-1, keepdims=True)
        o_ref[0, :, pair * 2 * V_DIM:(pair + 1) * 2 * V_DIM] = jnp.where(lane < V_DIM, o0, o1).astype(BF16)


def _flash_attention(q, k, v, tq, heads):
    b, s, _ = q.shape
    return pl.pallas_call(
        functools.partial(_flash_kernel, tq=tq, heads=heads),
        grid=(b, A_HEADS // heads, s // tq),
        in_specs=[pl.BlockSpec((1, tq, heads * HEAD_PAD), lambda bi, hg, qi: (bi, qi, hg)),
                  pl.BlockSpec((1, s, heads * HEAD_PAD), lambda bi, hg, qi: (bi, 0, hg)),
                  pl.BlockSpec((1, s, heads * V_DIM), lambda bi, hg, qi: (bi, 0, hg))],
        out_specs=pl.BlockSpec((1, tq, heads * V_DIM), lambda bi, hg, qi: (bi, qi, hg)),
        out_shape=jax.ShapeDtypeStruct((b, s, A_HEADS * V_DIM), BF16),
        scratch_shapes=[pltpu.VMEM((heads, tq, LANES), F32), pltpu.VMEM((heads, tq, LANES), F32),
                        pltpu.VMEM((heads, tq, 2 * V_DIM), F32)],
        compiler_params=_params("arbitrary", "arbitrary", "arbitrary"),
        name="flash_attention",
    )(q, k, v)


def _ssd_chunk_kernel(xbc_ref, z_ref, dt_ref, dtt_ref, cw_ref, cb_ref, dtb_ref, dtbc_ref,
                      na_ref, nac_ref, dsk_ref, gs_ref, e3_ref, y_o, st_o, xc_ref, state_ref, y_ref):
    c = pl.program_id(1)
    nc = pl.num_programs(1)

    @pl.when(c == 0)
    def _():
        xc_ref[0:8, :] = jnp.zeros((8, CONV_CH), F32)
        state_ref[...] = jnp.zeros(state_ref.shape, F32)

    xc_ref[8:8 + CHUNK, :] = xbc_ref[0]
    conv = cb_ref[...] + cw_ref[CONV_W - 1:CONV_W, :] * xc_ref[8:8 + CHUNK, :]
    for j in range(1, CONV_W):
        conv = conv + cw_ref[CONV_W - 1 - j:CONV_W - j, :] * xc_ref[8 - j:8 - j + CHUNK, :]
    xc_ref[0:8, :] = xc_ref[CHUNK:CHUNK + 8, :]
    xbc = _silu(conv)

    dt_c = _softplus(dt_ref[0] + dtb_ref[...])
    a_c = dt_c * na_ref[...]
    dt_r = _softplus(dtt_ref[0] + dtbc_ref[...])
    a_r = dt_r * nac_ref[...]
    li = lax.broadcasted_iota(jnp.int32, (CHUNK, CHUNK), 0)
    si = lax.broadcasted_iota(jnp.int32, (CHUNK, CHUNK), 1)
    lower = (si <= li)
    tri = lower.astype(BF16)
    acum_c = sum(_dot(tri, part) for part in _split3(a_c))
    acum_r = sum(_dot_nt(part, tri) for part in _split3(a_r))
    a_last = acum_c[CHUNK - 1:CHUNK, :]
    dec_end = jnp.exp(a_last - acum_c)
    exp_ac = jnp.exp(acum_c)

    head_lane = lax.broadcasted_iota(jnp.int32, (CHUNK, LANES), 1) < S_HEADS

    def per_channel(v):
        hi, mid, lo = (part.astype(F32) for part in _split3(jnp.where(head_lane, v, 0.0)))
        packed = hi + pltpu.roll(mid, S_HEADS, axis=1) + pltpu.roll(lo, 2 * S_HEADS, axis=1)
        return _dot(packed.astype(BF16), e3_ref[...])

    dt_x = per_channel(dt_c)
    w_end_x = per_channel(dt_c * dec_end)
    ex_x = per_channel(exp_ac)
    xs_all = xbc[:, :D_INNER]
    xdt_all = (xs_all * dt_x).astype(BF16)
    xde_all = (xs_all * w_end_x).astype(BF16)
    pair_lane = lax.broadcasted_iota(jnp.int32, (CHUNK, 2 * S_HEAD_DIM), 1) < S_HEAD_DIM

    for g in range(S_GROUPS):
        bm = xbc[:, D_INNER + g * D_STATE:D_INNER + (g + 1) * D_STATE]
        cm = xbc[:, D_INNER + (S_GROUPS + g) * D_STATE:D_INNER + (S_GROUPS + g + 1) * D_STATE]
        bmb = bm.astype(BF16)
        cmb = cm.astype(BF16)
        gmat = _dot_nt(cmb, bmb)
        gsl = slice(g * GROUP_CH, (g + 1) * GROUP_CH)
        y_off = _dot(cmb, state_ref[:, gsl].astype(BF16)) * ex_x[:, gsl]
        for pr in range(GROUP_HEADS // 2):
            h0 = g * GROUP_HEADS + 2 * pr
            psl = slice(h0 * S_HEAD_DIM, (h0 + 2) * S_HEAD_DIM)
            xdt_pair = xdt_all[:, psl]
            halves = []
            for hd in (h0, h0 + 1):
                seg = acum_c[:, hd:hd + 1] - acum_r[hd:hd + 1, :]
                lmat = jnp.exp(jnp.where(lower, seg, NEG_INF))
                halves.append(_dot((gmat * lmat).astype(BF16), xdt_pair))
            lsl = slice(2 * pr * S_HEAD_DIM, (2 * pr + 2) * S_HEAD_DIM)
            y_ref[:, psl] = jnp.where(pair_lane, halves[0], halves[1]) + y_off[:, lsl]
        new_states = _dot_tn(bmb, xde_all[:, gsl])
        state_ref[:, gsl] = state_ref[:, gsl] * ex_x[CHUNK - 1:CHUNK, gsl] + new_states

    y = y_ref[...] + dsk_ref[...] * xbc[:, :D_INNER]
    y = y * _silu(z_ref[0])
    for g in range(S_GROUPS):
        gsl = slice(g * GROUP_CH, (g + 1) * GROUP_CH)
        y_o[0, :, gsl] = _rms(y[:, gsl], gs_ref[:, gsl]).astype(BF16)

    @pl.when(c == nc - 1)
    def _():
        st_o[0] = state_ref[...]


def _ssd_prompt(xbc, z, dt, dtt, conv_w, conv_b, dtb_row, dtb_col, na_row, na_col, dskip, g_ssd):
    b, s, _ = xbc.shape
    nc = s // CHUNK
    r = jnp.arange(LANES)[:, None]
    ch = jnp.arange(D_INNER)[None, :]
    e3 = jnp.logical_and(r < 3 * S_HEADS, r % S_HEADS == ch // S_HEAD_DIM).astype(BF16)
    consts = (conv_w, conv_b, dtb_row, dtb_col, na_row, na_col, dskip, g_ssd, e3)
    return pl.pallas_call(
        _ssd_chunk_kernel,
        grid=(b, nc),
        in_specs=[pl.BlockSpec((1, CHUNK, CONV_CH), lambda i, j: (i, j, 0)),
                  pl.BlockSpec((1, CHUNK, D_INNER), lambda i, j: (i, j, 0)),
                  pl.BlockSpec((1, CHUNK, LANES), lambda i, j: (i, j, 0)),
                  pl.BlockSpec((1, S_HEADS, CHUNK), lambda i, j: (i, 0, j))]
        + [_const_spec(a) for a in consts],
        out_specs=[pl.BlockSpec((1, CHUNK, D_INNER), lambda i, j: (i, j, 0)),
                   pl.BlockSpec((1, D_STATE, D_INNER), lambda i, j: (i, 0, 0))],
        out_shape=[jax.ShapeDtypeStruct((b, s, D_INNER), BF16),
                   jax.ShapeDtypeStruct((b, D_STATE, D_INNER), F32)],
        scratch_shapes=[pltpu.VMEM((CHUNK + 8, CONV_CH), F32),
                        pltpu.VMEM((D_STATE, D_INNER), F32),
                        pltpu.VMEM((CHUNK, D_INNER), F32)],
        compiler_params=_params("arbitrary", "arbitrary"),
        name="ssd_prompt",
    )(xbc, z, dt, dtt, *consts)


def _merge_kernel(x_ref, oa_ref, ys_ref, ga_ref, gb_ref, g1_ref, sh2_ref, sc2_ref,
                  woa, wos, wout, gpost, gpre, x1_o, h2_o):
    o_attn = _dot(oa_ref[0], woa[...])
    o_ssd = _dot(ys_ref[0], wos[...])
    merged = _sigmoid(ga_ref[0]) * o_attn + _sigmoid(gb_ref[0]) * o_ssd
    m = _dot(merged.astype(BF16), wout[...])
    x1 = x_ref[0] + g1_ref[0] * _rms(m, gpost[...])
    x1_o[0] = x1
    h2_o[0] = (_rms(x1, gpre[...]) * (1.0 + sc2_ref[0]) + sh2_ref[0]).astype(BF16)


def _merge(x, o_attn, y_ssd, ga, gb, g1, sh2, sc2, weights, tm):
    b, s, d = x.shape
    per_row_mod = g1.shape[1] != 1
    mod_spec = (pl.BlockSpec((1, tm, d), lambda i, j: (i, j, 0)) if per_row_mod
                else pl.BlockSpec((1, 1, d), lambda i, j: (i, 0, 0)))

    def tok(n):
        return pl.BlockSpec((1, tm, n), lambda i, j: (i, j, 0))

    return pl.pallas_call(
        _merge_kernel,
        grid=(b, s // tm),
        in_specs=[tok(d), tok(A_HEADS * V_DIM), tok(D_INNER), tok(d), tok(d),
                  mod_spec, mod_spec, mod_spec] + [_const_spec(w) for w in weights],
        out_specs=[tok(d), tok(d)],
        out_shape=[jax.ShapeDtypeStruct((b, s, d), F32), jax.ShapeDtypeStruct((b, s, d), BF16)],
        compiler_params=_params("arbitrary", "arbitrary"),
        name="merge",
    )(x, o_attn, y_ssd, ga, gb, g1, sh2, sc2, *weights)


def _first_max(vals, idx, big):
    m = vals[0]
    for v in vals[1:]:
        m = jnp.maximum(m, v)
    m = jnp.max(m, axis=0, keepdims=True)
    cand = [jnp.where(v == m, i, big) for v, i in zip(vals, idx)]
    a = cand[0]
    for cnd in cand[1:]:
        a = jnp.minimum(a, cnd)
    a = jnp.min(a, axis=0, keepdims=True)
    return m, a


def _router_kernel(h_ref, wt_ref, b_ref, o_ref, slot_o, w_o, cnt_o, *, tm):
    logits = _dot_nt(wt_ref[...], h_ref[0])
    scores = _sigmoid(logits)
    biased = scores + b_ref[...]
    sub = lax.broadcasted_iota(jnp.int32, (EXP_PER_GROUP, tm), 0)
    slabs = [biased[g * EXP_PER_GROUP:(g + 1) * EXP_PER_GROUP, :] for g in range(N_EXP_GROUPS)]
    big = jnp.int32(N_EXPERTS)

    gscore = []
    for g in range(N_EXP_GROUPS):
        m1, a1 = _first_max([slabs[g]], [sub], big)
        rest = jnp.where(sub == a1, NEG_INF, slabs[g])
        m2 = jnp.max(rest, axis=0, keepdims=True)
        gscore.append(m1 + m2)
    gs = jnp.full((N_EXP_GROUPS, tm), NEG_INF, F32)
    for g in range(N_EXP_GROUPS):
        gs = jnp.where(sub == g, gscore[g], gs)
    gsel = jnp.zeros((N_EXP_GROUPS, tm), F32)
    for _ in range(TOPK_GROUPS):
        _, a = _first_max([gs], [sub], big)
        hit = sub == a
        gsel = jnp.where(hit, 1.0, gsel)
        gs = jnp.where(hit, NEG_INF, gs)

    masked = [jnp.where(gsel[g:g + 1, :] > 0.5, slabs[g], NEG_INF) for g in range(N_EXP_GROUPS)]
    flat = [sub + g * EXP_PER_GROUP for g in range(N_EXP_GROUPS)]
    chosen = [jnp.zeros((EXP_PER_GROUP, tm), jnp.bool_) for _ in range(N_EXP_GROUPS)]
    picks = []
    for _ in range(TOP_K):
        _, a = _first_max(masked, flat, big)
        picks.append(a)
        for g in range(N_EXP_GROUPS):
            hit = flat[g] == a
            chosen[g] = jnp.logical_or(chosen[g], hit)
            masked[g] = jnp.where(hit, NEG_INF, masked[g])
    score_slabs = [scores[g * EXP_PER_GROUP:(g + 1) * EXP_PER_GROUP, :] for g in range(N_EXP_GROUPS)]
    w = [jnp.where(chosen[g], score_slabs[g], 0.0) for g in range(N_EXP_GROUPS)]
    tot = w[0]
    for part in w[1:]:
        tot = tot + part
    tot = jnp.sum(tot, axis=0, keepdims=True)

    sel_t = jnp.concatenate([jnp.where(c, 1.0, 0.0) for c in chosen], axis=0)
    t_row = lax.broadcasted_iota(jnp.int32, (tm, tm), 0)
    t_col = lax.broadcasted_iota(jnp.int32, (tm, tm), 1)
    rank_t = _dot(sel_t.astype(BF16), (t_row < t_col).astype(BF16))
    cnt = jnp.sum(sel_t, axis=-1, keepdims=True)
    pc = jnp.floor((cnt + (CHUNK_ROWS - 1.0)) * (1.0 / CHUNK_ROWS)) * CHUNK_ROWS
    total = jnp.sum(pc, axis=0, keepdims=True)
    fill = jnp.ceil(total * (1.0 / ROW_TILE)) * ROW_TILE - total
    e_row = lax.broadcasted_iota(jnp.int32, (N_EXPERTS, 1), 0)
    pc = pc + jnp.where(e_row == N_EXPERTS - 1, fill, 0.0)
    e_r = lax.broadcasted_iota(jnp.int32, (N_EXPERTS, N_EXPERTS), 0)
    e_c = lax.broadcasted_iota(jnp.int32, (N_EXPERTS, N_EXPERTS), 1)
    pc_rep = jnp.broadcast_to(pc, (N_EXPERTS, LANES))
    off = _dot((e_c < e_r).astype(BF16), pc_rep.astype(BF16))
    slot_t = off[:, :1] + rank_t
    slot8 = jnp.full((8, tm), -1.0, F32)
    w8 = jnp.zeros((8, tm), F32)
    for k, a in enumerate(picks):
        s_acc = jnp.zeros((EXP_PER_GROUP, tm), F32)
        w_acc = jnp.zeros((EXP_PER_GROUP, tm), F32)
        for g in range(N_EXP_GROUPS):
            hit = flat[g] == a
            s_acc = jnp.where(hit, slot_t[g * EXP_PER_GROUP:(g + 1) * EXP_PER_GROUP, :], s_acc)
            w_acc = jnp.where(hit, score_slabs[g], w_acc)
        slot8 = jnp.where(sub == k, jnp.sum(s_acc, axis=0, keepdims=True), slot8)
        w8 = jnp.where(sub == k, jnp.sum(w_acc, axis=0, keepdims=True) / tot * ROUTE_SCALE, w8)

    info_t = jnp.concatenate([part / tot * ROUTE_SCALE for part in w] + [slot8, w8]
                             + [jnp.zeros((LANES - N_EXPERTS - 16, tm), F32)], axis=0)
    o_ref[0] = info_t.T
    slot_o[0] = slot8
    w_o[0] = w8
    cnt_o[0] = pc_rep


def _router(h2, w_router_t, b_col, tm):
    b, s, d = h2.shape
    nj = s // tm
    return pl.pallas_call(
        functools.partial(_router_kernel, tm=tm),
        grid=(b, nj),
        in_specs=[pl.BlockSpec((1, tm, d), lambda i, j: (i, j, 0)),
                  _const_spec(w_router_t), _const_spec(b_col)],
        out_specs=[pl.BlockSpec((1, tm, LANES), lambda i, j: (i, j, 0)),
                   pl.BlockSpec((1, 8, tm), lambda i, j: (i, 0, j)),
                   pl.BlockSpec((1, 8, tm), lambda i, j: (i, 0, j)),
                   pl.BlockSpec((1, N_EXPERTS, LANES), lambda i, j: (i * nj + j, 0, 0))],
        out_shape=[jax.ShapeDtypeStruct((b, s, LANES), F32),
                   jax.ShapeDtypeStruct((b, 8, s), F32),
                   jax.ShapeDtypeStruct((b, 8, s), F32),
                   jax.ShapeDtypeStruct((b * nj, N_EXPERTS, LANES), F32)],
        compiler_params=_params("arbitrary", "arbitrary"),
        name="router",
    )(h2, w_router_t, b_col)


def _dispatch_kernel(nk_ref, h_ref, slot_ref, w_ref, xs_ref, os_ref, *, tm, cap):
    os_ref[...] = jnp.zeros(os_ref.shape, BF16)
    nk = nk_ref[pl.program_id(0) * pl.num_programs(1) + pl.program_id(1)]
    h = h_ref[0]
    slots = slot_ref[0]
    ws = w_ref[0]
    lane = lax.broadcasted_iota(jnp.int32, (ROW_TILE, LANES), 1)
    for rt in range(cap // ROW_TILE):
        rsl = slice(rt * ROW_TILE, (rt + 1) * ROW_TILE)

        @pl.when(rt < nk)
        def _():
            rows = (rt * ROW_TILE + lax.broadcasted_iota(jnp.int32, (ROW_TILE, tm), 0)).astype(F32)
            p = jnp.zeros((ROW_TILE, tm), F32)
            pw = jnp.zeros((ROW_TILE, tm), F32)
            for k in range(TOP_K):
                hit = rows == slots[k:k + 1, :]
                p = jnp.where(hit, 1.0, p)
                pw = jnp.where(hit, ws[k:k + 1, :], pw)
            x = _dot(p.astype(BF16), h)
            hi, mid, lo = (part.astype(F32) for part in _split3(jnp.sum(pw, axis=-1, keepdims=True)))
            extra = jnp.where(lane == 0, hi, jnp.where(lane == 1, mid, jnp.where(lane == 2, lo, 0.0)))
            xs_ref[0, rsl, :D_MODEL] = x.astype(BF16)
            xs_ref[0, rsl, D_MODEL:] = extra.astype(BF16)

        @pl.when(rt >= nk)
        def _():
            xs_ref[0, rsl, :] = jnp.zeros((ROW_TILE, D_MODEL + LANES), BF16)


def _dispatch(nk, h2, slotk, wk, tm, cap):
    b, s, d = h2.shape
    nj = s // tm
    grid_spec = pltpu.PrefetchScalarGridSpec(
        num_scalar_prefetch=1,
        grid=(b, nj),
        in_specs=[pl.BlockSpec((1, tm, d), lambda i, j, n_: (i, j, 0)),
                  pl.BlockSpec((1, 8, tm), lambda i, j, n_: (i, 0, j)),
                  pl.BlockSpec((1, 8, tm), lambda i, j, n_: (i, 0, j))],
        out_specs=[pl.BlockSpec((1, cap, d + LANES), lambda i, j, n_: (i * nj + j, 0, 0)),
                   pl.BlockSpec((1, cap, d), lambda i, j, n_: (i * nj + j, 0, 0))],
    )
    return pl.pallas_call(
        functools.partial(_dispatch_kernel, tm=tm, cap=cap),
        grid_spec=grid_spec,
        out_shape=[jax.ShapeDtypeStruct((b * nj, cap, d + LANES), BF16),
                   jax.ShapeDtypeStruct((b * nj, cap, d), BF16)],
        compiler_params=_params("arbitrary", "arbitrary"),
        name="moe_dispatch",
    )(nk, h2, slotk, wk)


def _expert_kernel(te_ref, src_ref, dst_ref, nt_ref, xs_hbm, wg_ref, wu_ref, wd_ref, os_zero_hbm,
                   os_hbm, xbuf, obuf, wgb, wub, wdb, sem):
    del os_zero_hbm
    i = pl.program_id(0)
    nt = nt_ref[0]

    def in_copies(tile, slot):
        return [pltpu.make_async_copy(xs_hbm.at[src_ref[tile * TILE_CHUNKS + q]], xbuf.at[slot, q],
                                      sem.at[0, slot]) for q in range(TILE_CHUNKS)]

    def out_copies(tile, slot):
        return [pltpu.make_async_copy(obuf.at[slot, q], os_hbm.at[dst_ref[tile * TILE_CHUNKS + q]],
                                      sem.at[1, slot]) for q in range(TILE_CHUNKS)]

    @pl.when(i < nt)
    def _():
        slot = lax.rem(i, 2)

        @pl.when(i == 0)
        def _():
            for cpy in in_copies(0, 0):
                cpy.start()

        @pl.when(i + 1 < nt)
        def _():
            for cpy in in_copies(i + 1, 1 - slot):
                cpy.start()

        @pl.when(jnp.logical_or(i == 0, te_ref[i] != te_ref[jnp.maximum(i - 1, 0)]))
        def _():
            wgb[...] = wg_ref[0].astype(BF16)
            wub[...] = wu_ref[0].astype(BF16)
            wdb[...] = wd_ref[0].astype(BF16)

        for cpy in in_copies(i, slot):
            cpy.wait()
        outs = []
        for half in range(TILE_CHUNKS // SUB_CHUNKS):
            csl = slice(half * SUB_CHUNKS, (half + 1) * SUB_CHUNKS)
            x = xbuf[slot, csl].reshape(ROW_TILE, D_MODEL + LANES)
            xb = x[:, :D_MODEL]
            wrow = jnp.sum(x[:, D_MODEL:].astype(F32), axis=-1, keepdims=True)
            gte = _dot(xb, wgb[...])
            upe = _dot(xb, wub[...])
            act = (_silu(gte) * upe * wrow).astype(BF16)
            outs.append(_dot(act, wdb[...]).astype(BF16))

        @pl.when(i >= 2)
        def _():
            for cpy in out_copies(i - 2, slot):
                cpy.wait()

        for half, out in enumerate(outs):
            csl = slice(half * SUB_CHUNKS, (half + 1) * SUB_CHUNKS)
            obuf[slot, csl] = out.reshape(SUB_CHUNKS, CHUNK_ROWS, D_MODEL)
        for cpy in out_copies(i, slot):
            cpy.start()

        @pl.when(i == nt - 1)
        def _():
            @pl.when(i >= 1)
            def _():
                for cpy in out_copies(i - 1, 1 - slot):
                    cpy.wait()

            for cpy in out_copies(i, slot):
                cpy.wait()


def _experts(tile_expert, src, dst, n_tiles, xs_chunks, os_zero, w_gate_e, w_up_e, w_down_e, max_tiles):
    n_chunks, _, width = xs_chunks.shape
    d = width - LANES
    grid_spec = pltpu.PrefetchScalarGridSpec(
        num_scalar_prefetch=4,
        grid=(max_tiles,),
        in_specs=[pl.BlockSpec(memory_space=pl.ANY),
                  pl.BlockSpec((1, d, F_EXPERT), lambda i, te, s_, d_, n_: (te[i], 0, 0)),
                  pl.BlockSpec((1, d, F_EXPERT), lambda i, te, s_, d_, n_: (te[i], 0, 0)),
                  pl.BlockSpec((1, F_EXPERT, d), lambda i, te, s_, d_, n_: (te[i], 0, 0)),
                  pl.BlockSpec(memory_space=pl.ANY)],
        out_specs=pl.BlockSpec(memory_space=pl.ANY),
        scratch_shapes=[pltpu.VMEM((2, TILE_CHUNKS, CHUNK_ROWS, width), BF16),
                        pltpu.VMEM((2, TILE_CHUNKS, CHUNK_ROWS, d), BF16),
                        pltpu.VMEM((d, F_EXPERT), BF16), pltpu.VMEM((d, F_EXPERT), BF16),
                        pltpu.VMEM((F_EXPERT, d), BF16),
                        pltpu.SemaphoreType.DMA((2, 2))],
    )
    return pl.pallas_call(
        _expert_kernel,
        grid_spec=grid_spec,
        out_shape=jax.ShapeDtypeStruct((n_chunks, CHUNK_ROWS, d), BF16),
        input_output_aliases={8: 0},
        compiler_params=_params("arbitrary"),
        name="moe_experts",
    )(tile_expert, src, dst, n_tiles, xs_chunks, w_gate_e, w_up_e, w_down_e, os_zero)


def _combine_kernel(nk_ref, cum_ref, info_ref, h_ref, x1_ref, g2_ref, os_hbm, wgs, wus, wds, gpost,
                    y_o, obuf, sem, *, tm):
    nj = pl.num_programs(1)
    blk = pl.program_id(0) * nj + pl.program_id(1)
    nblk = pl.num_programs(0) * nj
    nk = nk_ref[blk]

    def fetch(bk, kt, slot):
        rows = pl.ds(pl.multiple_of(kt * ROW_TILE, ROW_TILE), ROW_TILE)
        return pltpu.make_async_copy(os_hbm.at[bk, rows], obuf.at[slot], sem.at[slot])

    ahead = COMBINE_RING - 1
    assert TOP_K * tm >= ahead * ROW_TILE

    @pl.when(blk == 0)
    def _():
        for t in range(ahead):
            fetch(0, t, t).start()

    hb = h_ref[0]
    hid = _silu(_dot(hb, wgs[...])) * _dot(hb, wus[...])
    acc0 = _dot(hid.astype(BF16), wds[...])
    info = info_ref[0]
    slot_cols = [jnp.broadcast_to(info[:, N_EXPERTS + k:N_EXPERTS + k + 1], (tm, ROW_TILE))
                 for k in range(TOP_K)]
    lane_row = lax.broadcasted_iota(jnp.int32, (tm, ROW_TILE), 1).astype(F32)

    def body(kt, acc):
        g = cum_ref[blk] + kt
        slot = lax.rem(g, COMBINE_RING)
        nslot = lax.rem(g + ahead, COMBINE_RING)

        @pl.when(kt + ahead < nk)
        def _():
            fetch(blk, kt + ahead, nslot).start()

        @pl.when(jnp.logical_and(kt + ahead >= nk, blk + 1 < nblk))
        def _():
            fetch(blk + 1, kt + ahead - nk, nslot).start()

        fetch(blk, kt, slot).wait()
        rows = lane_row + (kt * ROW_TILE).astype(F32)
        pt = jnp.zeros((tm, ROW_TILE), F32)
        for col in slot_cols:
            pt = jnp.where(rows == col, 1.0, pt)
        return acc + _dot(pt.astype(BF16), obuf[slot])

    acc = lax.fori_loop(0, nk, body, acc0)
    y_o[0] = x1_ref[0] + g2_ref[0] * _rms(acc, gpost[...])


def _combine(nk, cum_nk, info, h2, x1, g2, os_blocks, shared, gpost, tm):
    b, s, d = h2.shape
    nj = s // tm

    def tok(n):
        return pl.BlockSpec((1, tm, n), lambda i, j, *_: (i, j, 0))

    grid_spec = pltpu.PrefetchScalarGridSpec(
        num_scalar_prefetch=2,
        grid=(b, nj),
        in_specs=[tok(LANES), tok(d), tok(d), pl.BlockSpec((1, 1, d), lambda i, j, *_: (i, 0, 0)),
                  pl.BlockSpec(memory_space=pl.ANY)]
        + [_const_spec(w) for w in shared] + [_const_spec(gpost)],
        out_specs=tok(d),
        scratch_shapes=[pltpu.VMEM((COMBINE_RING, ROW_TILE, d), BF16),
                        pltpu.SemaphoreType.DMA((COMBINE_RING,))],
    )
    return pl.pallas_call(
        functools.partial(_combine_kernel, tm=tm),
        grid_spec=grid_spec,
        out_shape=jax.ShapeDtypeStruct((b, s, d), F32),
        compiler_params=_params("arbitrary", "arbitrary"),
        name="moe_combine",
    )(nk, cum_nk, info, h2, x1, g2, os_blocks, *shared, gpost)


def _moe_plan(pc_rep, cap_chunks, max_tiles):
    pc = pc_rep[:, :, 0].astype(jnp.int32)
    nblk = pc.shape[0]
    nch = pc // CHUNK_ROWS
    off = jnp.cumsum(nch, axis=1) - nch
    nk = jnp.sum(pc, axis=1) // ROW_TILE
    cum_nk = jnp.cumsum(nk) - nk
    cum_blk = jnp.cumsum(nch, axis=0)
    tot = cum_blk[-1]
    tiles = (tot + TILE_CHUNKS - 1) // TILE_CHUNKS
    tile_end = jnp.cumsum(tiles)
    n_tiles = tile_end[-1]
    tile_ids = jnp.arange(max_tiles, dtype=jnp.int32)
    last_expert = jnp.sum((tile_end < n_tiles).astype(jnp.int32))
    te = jnp.sum((tile_end[None, :] <= tile_ids[:, None]).astype(jnp.int32), axis=1)
    te = jnp.minimum(te, last_expert)
    oh_e = (te[:, None] == jnp.arange(N_EXPERTS, dtype=jnp.int32)[None, :]).astype(jnp.int32)

    def by_expert(v):
        return jnp.sum(oh_e * v[None, :], axis=1) if v.ndim == 1 else jnp.sum(
            oh_e[:, :, None] * v[None, :, :], axis=1)

    pos = ((tile_ids - by_expert(tile_end - tiles))[:, None] * TILE_CHUNKS
           + jnp.arange(TILE_CHUNKS, dtype=jnp.int32)[None, :])
    valid = jnp.logical_and(tile_ids[:, None] < n_tiles, pos < by_expert(tot)[:, None])
    cum_e = by_expert(cum_blk.T)
    nch_e = by_expert(nch.T)
    off_e = by_expert(off.T)
    blk = jnp.sum((cum_e[:, None, :] <= pos[:, :, None]).astype(jnp.int32), axis=-1)
    blk = jnp.minimum(blk, nblk - 1)
    oh_b = (blk[:, :, None] == jnp.arange(nblk, dtype=jnp.int32)[None, None, :]).astype(jnp.int32)
    before = jnp.sum(oh_b * (cum_e - nch_e)[:, None, :], axis=-1)
    local = jnp.sum(oh_b * off_e[:, None, :], axis=-1) + (pos - before)
    chunk = blk * cap_chunks + local
    assert nblk * SUB_CHUNKS >= TILE_CHUNKS
    q = jnp.arange(TILE_CHUNKS, dtype=jnp.int32)[None, :]
    spare = (q // SUB_CHUNKS) * cap_chunks + cap_chunks - SUB_CHUNKS + q % SUB_CHUNKS
    src = jnp.where(valid, chunk, cap_chunks - 1)
    dst = jnp.where(valid, chunk, spare)
    return (te.astype(jnp.int32), src.reshape(-1).astype(jnp.int32), dst.reshape(-1).astype(jnp.int32),
            n_tiles.reshape(1).astype(jnp.int32), nk.astype(jnp.int32), cum_nk.astype(jnp.int32))


def _sparse_moe(h2, info, slotk, wk, pc_rep, x1, g2, w_gate_e, w_up_e, w_down_e, shared, gpost, tm):
    b, s, d = h2.shape
    nblk = b * (s // tm)
    cap = TOP_K * tm + N_EXPERTS * CHUNK_ROWS + ROW_TILE
    cap = -(-cap // ROW_TILE) * ROW_TILE
    cap_chunks = cap // CHUNK_ROWS
    max_tiles = nblk * (cap_chunks - SUB_CHUNKS) // TILE_CHUNKS + N_EXPERTS
    te, src, dst, n_tiles, nk, cum_nk = _moe_plan(pc_rep, cap_chunks, max_tiles)
    xs, os_zero = _dispatch(nk, h2, slotk, wk, tm, cap)
    os_chunks = _experts(te, src, dst, n_tiles, xs.reshape(nblk * cap_chunks, CHUNK_ROWS, d + LANES),
                         os_zero.reshape(nblk * cap_chunks, CHUNK_ROWS, d),
                         w_gate_e, w_up_e, w_down_e, max_tiles)
    return _combine(nk, cum_nk, info, h2, x1, g2, os_chunks.reshape(nblk, cap, d), shared, gpost, tm)


def _moe_kernel(h_ref, c_ref, x1_ref, g2_ref, wg_ref, wu_ref, wd_ref, wgs, wus, wds, gpost,
                y_o, acc_ref, *, tm):
    e = pl.program_id(2)
    ne = pl.num_programs(2)
    hb = h_ref[0]

    @pl.when(e == 0)
    def _():
        hid = _silu(_dot(hb, wgs[...])) * _dot(hb, wus[...])
        acc_ref[...] = _dot(hid.astype(BF16), wds[...])

    lane = lax.broadcasted_iota(jnp.int32, (tm, LANES), 1)
    ce = jnp.sum(jnp.where(lane == e, c_ref[0], 0.0), axis=-1, keepdims=True)
    gte = _dot(hb, wg_ref[0].astype(BF16))
    upe = _dot(hb, wu_ref[0].astype(BF16))
    act = (_silu(gte) * upe * ce).astype(BF16)
    acc_ref[...] += _dot(act, wd_ref[0].astype(BF16))

    @pl.when(e == ne - 1)
    def _():
        y_o[0] = x1_ref[0] + g2_ref[0] * _rms(acc_ref[...], gpost[...])


def _moe(h2, comb, x1, g2, w_gate_e, w_up_e, w_down_e, shared, gpost, tm):
    b, s, d = h2.shape
    per_row_mod = g2.shape[1] != 1
    mod_spec = (pl.BlockSpec((1, tm, d), lambda i, j, e: (i, j, 0)) if per_row_mod
                else pl.BlockSpec((1, 1, d), lambda i, j, e: (i, 0, 0)))

    def tok(n):
        return pl.BlockSpec((1, tm, n), lambda i, j, e: (i, j, 0))

    return pl.pallas_call(
        functools.partial(_moe_kernel, tm=tm),
        grid=(b, s // tm, N_EXPERTS),
        in_specs=[tok(d), tok(LANES), tok(d), mod_spec,
                  pl.BlockSpec((1, d, F_EXPERT), lambda i, j, e: (e, 0, 0)),
                  pl.BlockSpec((1, d, F_EXPERT), lambda i, j, e: (e, 0, 0)),
                  pl.BlockSpec((1, F_EXPERT, d), lambda i, j, e: (e, 0, 0))]
        + [_const_spec(w) for w in shared] + [_const_spec(gpost)],
        out_specs=tok(d),
        out_shape=jax.ShapeDtypeStruct((b, s, d), F32),
        scratch_shapes=[pltpu.VMEM((tm, d), F32)],
        compiler_params=_params("arbitrary", "arbitrary", "arbitrary"),
        name="moe",
    )(h2, comb, x1, g2, w_gate_e, w_up_e, w_down_e, *shared, gpost)


def _qlat_kernel(q_ref, wuk_ref, o_ref):
    for hd in range(A_HEADS):
        qn = q_ref[:, hd * HEAD_PAD:hd * HEAD_PAD + QK_NOPE]
        o_ref[hd] = _dot_nt(qn, wuk_ref[hd]).astype(BF16)


def _qlat(q, wuk_heads):
    nb = q.shape[0]
    return pl.pallas_call(
        _qlat_kernel,
        in_specs=[_const_spec(q), _const_spec(wuk_heads)],
        out_specs=pl.BlockSpec((A_HEADS, nb, KV_LORA), lambda: (0, 0, 0)),
        out_shape=jax.ShapeDtypeStruct((A_HEADS, nb, KV_LORA), BF16),
        grid=(),
        name="q_latent",
    )(q, wuk_heads)


def _decode_kernel(pt_ref, ql_ref, qp_ref, cn_ref, kn_ref, ckv_hbm, kpet_hbm, o_ref,
                   cbuf, kbuf, sem, *, pages_per_chunk, n_chunks):
    b = pl.program_id(0)
    nb = pl.num_programs(0)
    cp = pages_per_chunk
    halves = 2 if cp % 2 == 0 else 1
    hp = cp // halves
    total = nb * n_chunks
    ahead = DECODE_RING - 1

    def copies(g):
        g = jnp.asarray(g, jnp.int32)
        bb = lax.div(g, jnp.int32(n_chunks))
        chunk = g - bb * n_chunks
        slot = lax.rem(g, jnp.int32(DECODE_RING))
        out = []
        for p in range(cp):
            page = pt_ref[bb, chunk * cp + p]
            out.append(pltpu.make_async_copy(ckv_hbm.at[page], cbuf.at[slot, p], sem.at[0, slot]))
            out.append(pltpu.make_async_copy(kpet_hbm.at[page], kbuf.at[slot, p], sem.at[1, slot]))
        return out

    def start(g):
        for n, cpy in enumerate(copies(g)):
            cpy.start(priority=(n // 2) % 2)

    @pl.when(b == 0)
    def _():
        for g in range(ahead):
            @pl.when(g < total)
            def _():
                start(g)

    ql = ql_ref[0]
    qp = qp_ref[0]
    c_new = cn_ref[0]
    k_new = kn_ref[0]
    s_new = (jnp.sum(ql.astype(F32) * c_new, axis=-1, keepdims=True)
             + jnp.sum(qp.astype(F32) * k_new, axis=-1, keepdims=True)) * ATTN_SCALE
    m0 = s_new
    l0 = jnp.ones((A_HEADS, 1), F32)
    acc0 = jnp.broadcast_to(c_new, (A_HEADS, KV_LORA))

    def body(chunk, carry):
        m_prev, l_prev, acc = carry
        g = b * n_chunks + chunk
        slot = lax.rem(g, DECODE_RING)

        @pl.when(g + ahead < total)
        def _():
            start(g + ahead)

        for cpy in copies(g):
            cpy.wait()
        for hf in range(halves):
            kc = cbuf[slot, hf * hp:(hf + 1) * hp].reshape(hp * PAGE_SIZE, KV_LORA).astype(BF16)
            kt = jnp.concatenate([kbuf[slot, hf * hp + p] for p in range(hp)], axis=1).astype(BF16)
            s = (_dot_nt(ql, kc) + _dot(qp, kt)) * ATTN_SCALE
            m_new = jnp.maximum(m_prev, jnp.max(s, axis=-1, keepdims=True))
            alpha = jnp.exp(m_prev - m_new)
            p = jnp.exp(s - m_new)
            l_prev = alpha * l_prev + jnp.sum(p, axis=-1, keepdims=True)
            acc = alpha * acc + _dot(p.astype(BF16), kc)
            m_prev = m_new
        return m_prev, l_prev, acc

    _, l_fin, acc = lax.fori_loop(0, n_chunks, body, (m0, l0, acc0))
    o_ref[0] = acc / l_fin


def _decode_attention(page_table, qlat, qpe, ckv_new, kpe_new, cache_ckv, cache_kpe_t):
    nb, n_pages = page_table.shape
    cp = min(16, n_pages)
    n_chunks = n_pages // cp
    grid_spec = pltpu.PrefetchScalarGridSpec(
        num_scalar_prefetch=1,
        grid=(nb,),
        in_specs=[pl.BlockSpec((1, A_HEADS, KV_LORA), lambda i, pt: (i, 0, 0)),
                  pl.BlockSpec((1, A_HEADS, QK_ROPE), lambda i, pt: (i, 0, 0)),
                  pl.BlockSpec((1, 1, KV_LORA), lambda i, pt: (i, 0, 0)),
                  pl.BlockSpec((1, 1, QK_ROPE), lambda i, pt: (i, 0, 0)),
                  pl.BlockSpec(memory_space=pl.ANY),
                  pl.BlockSpec(memory_space=pl.ANY)],
        out_specs=pl.BlockSpec((1, A_HEADS, KV_LORA), lambda i, pt: (i, 0, 0)),
        scratch_shapes=[pltpu.VMEM((DECODE_RING, cp, PAGE_SIZE, KV_LORA), F32),
                        pltpu.VMEM((DECODE_RING, cp, QK_ROPE, PAGE_SIZE), F32),
                        pltpu.SemaphoreType.DMA((2, DECODE_RING))],
    )
    return pl.pallas_call(
        functools.partial(_decode_kernel, pages_per_chunk=cp, n_chunks=n_chunks),
        grid_spec=grid_spec,
        out_shape=jax.ShapeDtypeStruct((nb, A_HEADS, KV_LORA), F32),
        compiler_params=_params("arbitrary"),
        name="decode_attention",
    )(page_table, qlat, qpe, ckv_new, kpe_new, cache_ckv, cache_kpe_t)


def _vup_kernel(ol_ref, wuv_ref, o_ref):
    for hd in range(A_HEADS):
        o_ref[:, hd * V_DIM:(hd + 1) * V_DIM] = _dot(ol_ref[hd].astype(BF16), wuv_ref[hd]).astype(BF16)


def _value_up(o_lat_heads, wuv_heads):
    nb = o_lat_heads.shape[1]
    return pl.pallas_call(
        _vup_kernel,
        in_specs=[_const_spec(o_lat_heads), _const_spec(wuv_heads)],
        out_specs=pl.BlockSpec((nb, A_HEADS * V_DIM), lambda: (0, 0)),
        out_shape=jax.ShapeDtypeStruct((nb, A_HEADS * V_DIM), BF16),
        grid=(),
        name="value_up",
    )(o_lat_heads, wuv_heads)


def _ssd_step_kernel(xbc_ref, cs_ref, z_ref, dt_ref, st_ref, cw_ref, cb_ref, dtb_ref, na_ref,
                     dsk_ref, gs_ref, y_o, st_o, y_ref):
    conv = cb_ref[...] + cw_ref[CONV_W - 1:CONV_W, :] * xbc_ref[0]
    for j in range(CONV_W - 1):
        conv = conv + cw_ref[j:j + 1, :] * cs_ref[0, j:j + 1, :]
    xbc = _silu(conv)
    dt = _softplus(dt_ref[0] + dtb_ref[...])
    dec = jnp.exp(dt * na_ref[...])
    hi, mid, lo = (part.astype(F32) for part in _split3(xbc[:, :D_INNER]))
    r8 = lax.broadcasted_iota(jnp.int32, (8, D_INNER), 0)
    x8 = jnp.where(r8 == 0, hi, jnp.where(r8 == 1, mid, jnp.where(r8 == 2, lo, 0.0))).astype(BF16)
    ones8 = (lax.broadcasted_iota(jnp.int32, (8, D_STATE), 0) < 3).astype(BF16)
    x_col = _dot_tn(x8, ones8)
    for hd in range(S_HEADS):
        g = hd // GROUP_HEADS
        hsl = slice(hd * S_HEAD_DIM, (hd + 1) * S_HEAD_DIM)
        bm = xbc[:, D_INNER + g * D_STATE:D_INNER + (g + 1) * D_STATE]
        st_o[0, hd] = st_ref[0, hd] * dec[:, hd:hd + 1] + x_col[hsl, :] * (bm * dt[:, hd:hd + 1])
    for g in range(S_GROUPS):
        cm = xbc[:, D_INNER + (S_GROUPS + g) * D_STATE:D_INNER + (S_GROUPS + g + 1) * D_STATE]
        cb8 = jnp.broadcast_to(cm, (8, D_STATE)).astype(BF16)
        new_g = st_o[0, g * GROUP_HEADS:(g + 1) * GROUP_HEADS].reshape(GROUP_CH, D_STATE)
        y_ref[:, g * GROUP_CH:(g + 1) * GROUP_CH] = _dot_nt(cb8, new_g.astype(BF16))
    y = y_ref[0:1, :] + dsk_ref[...] * xbc[:, :D_INNER]
    y = y * _silu(z_ref[0])
    for g in range(S_GROUPS):
        gsl = slice(g * GROUP_CH, (g + 1) * GROUP_CH)
        y_o[0, :, gsl] = _rms(y[:, gsl], gs_ref[:, gsl]).astype(BF16)


def _ssd_sample(xbc, conv_state, z, dt, ssm_state, conv_w, conv_b, dtb_row, na_row, dskip, g_ssd):
    nb = xbc.shape[0]
    consts = (conv_w, conv_b, dtb_row, na_row, dskip, g_ssd)
    st_spec = pl.BlockSpec((1, S_HEADS, S_HEAD_DIM, D_STATE), lambda i: (i, 0, 0, 0))
    return pl.pallas_call(
        _ssd_step_kernel,
        grid=(nb,),
        in_specs=[pl.BlockSpec((1, 1, CONV_CH), lambda i: (i, 0, 0)),
                  pl.BlockSpec((1, CONV_W - 1, CONV_CH), lambda i: (i, 0, 0)),
                  pl.BlockSpec((1, 1, D_INNER), lambda i: (i, 0, 0)),
                  pl.BlockSpec((1, 1, LANES), lambda i: (i, 0, 0)),
                  st_spec] + [_const_spec(a) for a in consts],
        out_specs=[pl.BlockSpec((1, 1, D_INNER), lambda i: (i, 0, 0)), st_spec],
        out_shape=[jax.ShapeDtypeStruct((nb, 1, D_INNER), BF16),
                   jax.ShapeDtypeStruct(ssm_state.shape, F32)],
        scratch_shapes=[pltpu.VMEM((8, D_INNER), F32)],
        compiler_params=_params("arbitrary"),
        name="ssd_sample",
    )(xbc, conv_state, z, dt, ssm_state, *consts)


def _rot_half(w):
    half = QK_ROPE // 2
    return jnp.concatenate([-w[..., half:], w[..., :half]], axis=-1)


def _pad_cols(w, start, total):
    return jnp.pad(w, ((0, 0), (start, total - start - w.shape[1])))


def _head_pad(w_nope, w_rope):
    k = w_nope.shape[0]
    pad = jnp.zeros((k, A_HEADS, HEAD_PAD - QK_NOPE - QK_ROPE), w_nope.dtype)
    return jnp.concatenate([w_nope, w_rope, pad], axis=-1).reshape(k, A_HEADS * HEAD_PAD)


def _rope_tables(pos):
    half = QK_ROPE // 2
    inv = ROPE_BASE ** (-jnp.arange(half, dtype=F32) / half)
    ang = pos.astype(F32)[:, None] * inv[None, :]
    cos, sin = jnp.cos(ang), jnp.sin(ang)
    n = pos.shape[0]
    ctab = jnp.concatenate([jnp.ones((n, QK_NOPE), F32), cos, cos,
                            jnp.zeros((n, HEAD_PAD - QK_NOPE - QK_ROPE), F32)], axis=1)
    stab = jnp.concatenate([jnp.zeros((n, QK_NOPE), F32), sin, sin,
                            jnp.zeros((n, HEAD_PAD - QK_NOPE - QK_ROPE), F32)], axis=1)
    return ctab, stab


def _pick_tile(n, target):
    t = min(n, target)
    assert n % t == 0, (n, t)
    return t


def kernel(x_prompt, x_sample, cache_ckv, cache_kpe, state_conv, state_ssm, page_table, c_prompt, c_sample, w_ada, b_ada, g_pre_mix, g_post_mix, g_pre_ffn, g_post_ffn, w_in, g_q_a, w_q_b, g_kv_a, w_uk, w_uv, w_o_attn, conv_w, conv_b, dt_bias, a_log, d_skip, g_ssd, w_o_ssd, w_out, w_router, b_router, w_gate_e, w_up_e, w_down_e, w_gate_s, w_up_s, w_down_s):
    bp, sp, d = x_prompt.shape
    nb, ds, _ = x_sample.shape
    depth = w_in.shape[0]
    assert depth == 1 and ds == 1 and d == D_MODEL
    n_pages = page_table.shape[1]
    past_len = n_pages * PAGE_SIZE
    lyr = 0

    offs = [0]
    for n in IN_SPLITS:
        offs.append(offs[-1] + n)
    win = w_in[lyr]
    w_qa, w_kva, w_kpe, w_z, w_xbc, w_dt, w_ga, w_gb = (win[:, offs[i]:offs[i + 1]] for i in range(8))
    wqb = w_q_b[lyr].reshape(Q_LORA, A_HEADS, QK_NOPE + QK_ROPE)
    wq_pad = _head_pad(wqb[..., :QK_NOPE], wqb[..., QK_NOPE:])
    wq_rot = _head_pad(jnp.zeros_like(wqb[..., :QK_NOPE]), _rot_half(wqb[..., QK_NOPE:]))
    wuk_pad = _head_pad(w_uk[lyr], jnp.zeros((KV_LORA, A_HEADS, QK_ROPE), F32))
    premix_w = [
        w_qa.astype(BF16), w_kva.astype(BF16),
        _pad_cols(w_kpe, QK_NOPE, LANES).astype(BF16),
        _pad_cols(_rot_half(w_kpe), QK_NOPE, LANES).astype(BF16),
        w_z.astype(BF16), w_xbc.astype(BF16),
        _pad_cols(w_dt, 0, LANES).astype(BF16), w_dt.T.astype(BF16),
        w_ga.astype(BF16), w_gb.astype(BF16),
        g_q_a[lyr][None, :], wq_pad.astype(BF16), wq_rot.astype(BF16),
        g_kv_a[lyr][None, :], wuk_pad.astype(BF16),
        w_uv[lyr].reshape(KV_LORA, A_HEADS * V_DIM).astype(BF16),
    ]
    merge_w = [w_o_attn[lyr].astype(BF16), w_o_ssd[lyr].astype(BF16), w_out[lyr].astype(BF16),
               g_post_mix[lyr][None, :], g_pre_ffn[lyr][None, :]]
    shared_w = [w_gate_s[lyr].astype(BF16), w_up_s[lyr].astype(BF16), w_down_s[lyr].astype(BF16)]
    w_router_t = w_router[lyr].T.astype(BF16)
    b_router_col = b_router[lyr][:, None]
    g_pre = g_pre_mix[lyr][None, :]
    g_post_ffn_row = g_post_ffn[lyr][None, :]
    cw = conv_w[lyr]
    cb = conv_b[lyr][None, :]
    neg_a = -jnp.exp(a_log[lyr].astype(F32))
    dtb_row = jnp.pad(dt_bias[lyr], (0, LANES - S_HEADS))[None, :]
    na_row = jnp.pad(neg_a, (0, LANES - S_HEADS))[None, :]
    dtb_col = dt_bias[lyr][:, None]
    na_col = neg_a[:, None]
    dskip_row = jnp.repeat(d_skip[lyr].astype(F32), S_HEAD_DIM)[None, :]
    gssd_row = g_ssd[lyr][None, :]

    n_mod_rows = bp + nb
    pad_rows = (-n_mod_rows) % 16
    c_all = jnp.concatenate([c_prompt, c_sample, jnp.zeros((pad_rows, d), F32)], axis=0)
    mod = _adaln(c_all, w_ada[lyr], b_ada[lyr][None, :])
    mods = [mod[:, i * d:(i + 1) * d] for i in range(6)]
    mods_p = [m[:bp][:, None, :] for m in mods]
    mods_s = [m[bp:bp + nb][None, :, :] for m in mods]

    ctab_p, stab_p = _rope_tables(jnp.arange(sp))
    tm = _pick_tile(sp, 256)
    (q_p, k_p, v_p, ckv_p, kpe_pad_p, z_p, xbc_p, dt_p, dtt_p, ga_p, gb_p) = _premix(
        x_prompt, mods_p[0], mods_p[1], g_pre, ctab_p, stab_p, premix_w, tm)
    o_attn_p = _flash_attention(q_p, k_p, v_p, _pick_tile(sp, 512), 2)
    y_ssd_p, ssm_t_p = _ssd_prompt(xbc_p, z_p, dt_p, dtt_p, cw, cb, dtb_row, dtb_col, na_row, na_col,
                                   dskip_row, gssd_row)
    x1_p, h2_p = _merge(x_prompt, o_attn_p, y_ssd_p, ga_p, gb_p, mods_p[2], mods_p[3], mods_p[4],
                        merge_w, _pick_tile(sp, 512))
    tb = _pick_tile(sp, MOE_BLOCK)
    info_p, slotk_p, wk_p, pc_p = _router(h2_p, w_router_t, b_router_col, tb)
    y_prompt = _sparse_moe(h2_p, info_p, slotk_p, wk_p, pc_p, x1_p, mods_p[5],
                           w_gate_e[lyr], w_up_e[lyr], w_down_e[lyr], shared_w, g_post_ffn_row, tb)

    xs = x_sample.reshape(1, nb, d)
    ctab_s, stab_s = _rope_tables(past_len + jnp.arange(ds))
    (q_s, _, _, ckv_s, kpe_pad_s, z_s, xbc_s, dt_s, _, ga_s, gb_s) = _premix(
        xs, mods_s[0], mods_s[1], g_pre, ctab_s, stab_s, premix_w, nb)
    kpe_s = kpe_pad_s[0, :, QK_NOPE:QK_NOPE + QK_ROPE]
    wuk_heads = jnp.transpose(w_uk[lyr], (1, 0, 2)).astype(BF16)
    wuv_heads = jnp.transpose(w_uv[lyr], (1, 0, 2)).astype(BF16)
    qlat = jnp.transpose(_qlat(q_s[0], wuk_heads), (1, 0, 2))
    qpe = q_s[0].reshape(nb, A_HEADS, HEAD_PAD)[:, :, QK_NOPE:QK_NOPE + QK_ROPE]
    o_lat = _decode_attention(page_table, qlat, qpe,
                              ckv_s.reshape(nb, 1, KV_LORA), kpe_s.reshape(nb, 1, QK_ROPE),
                              cache_ckv.reshape(cache_ckv.shape[1:]),
                              jnp.swapaxes(cache_kpe.reshape(cache_kpe.shape[1:]), 1, 2))
    o_attn_s = _value_up(jnp.transpose(o_lat, (1, 0, 2)), wuv_heads)[None]
    y_ssd_s, ssm_s = _ssd_sample(xbc_s.reshape(nb, 1, CONV_CH), state_conv[lyr],
                                 z_s.reshape(nb, 1, D_INNER), dt_s.reshape(nb, 1, LANES),
                                 state_ssm.reshape(state_ssm.shape[1:]), cw, cb, dtb_row, na_row,
                                 dskip_row, gssd_row)
    x1_s, h2_s = _merge(xs, o_attn_s, y_ssd_s.reshape(1, nb, D_INNER), ga_s, gb_s,
                        mods_s[2], mods_s[3], mods_s[4], merge_w, nb)
    comb_s = _router(h2_s, w_router_t, b_router_col, nb)[0]
    y_s = _moe(h2_s, comb_s, x1_s, mods_s[5], w_gate_e[lyr], w_up_e[lyr], w_down_e[lyr],
               shared_w, g_post_ffn_row, nb)

    kpe_p = kpe_pad_p[:, :, QK_NOPE:QK_NOPE + QK_ROPE]
    conv_p = xbc_p[:, sp - (CONV_W - 1):, :]
    ssm_p = jnp.transpose(ssm_t_p.reshape(bp, D_STATE, S_HEADS, S_HEAD_DIM), (0, 2, 3, 1))
    conv_s = jnp.concatenate([state_conv[lyr][:, 1:, :], xbc_s.reshape(nb, 1, CONV_CH)], axis=1)
    return (y_prompt, y_s.reshape(nb, ds, d),
            ckv_p[None], kpe_p[None], conv_p[None], ssm_p[None].astype(x_prompt.dtype),
            ckv_s.reshape(1, nb, ds, KV_LORA), kpe_s.reshape(1, nb, ds, QK_ROPE),
            conv_s[None], ssm_s[None].astype(x_sample.dtype))
```

```python
import functools

import jax
import jax.numpy as jnp
from jax import lax
from jax.experimental import pallas as pl
from jax.experimental.pallas import tpu as pltpu

F32 = jnp.float32
BF16 = jnp.bfloat16

D_MODEL = 1024
PAGE_SIZE = 128
A_HEADS = 8
QK_NOPE = 64
QK_ROPE = 32
V_DIM = 64
Q_LORA = 384
KV_LORA = 256
ROPE_BASE = 10000.0
ATTN_SCALE = (QK_NOPE + QK_ROPE) ** -0.5
S_HEADS = 16
S_HEAD_DIM = 64
D_INNER = S_HEADS * S_HEAD_DIM
S_GROUPS = 2
D_STATE = 128
CONV_W = 4
CONV_CH = D_INNER + 2 * S_GROUPS * D_STATE
CHUNK = 128
N_EXPERTS = 64
TOP_K = 6
N_EXP_GROUPS = 8
TOPK_GROUPS = 4
F_EXPERT = 256
ROUTE_SCALE = 2.5
EPS = 1e-6
IN_SPLITS = (Q_LORA, KV_LORA, QK_ROPE, D_INNER, CONV_CH, S_HEADS, D_MODEL, D_MODEL)

LANES = 128
HEAD_PAD = LANES
GROUP_HEADS = S_HEADS // S_GROUPS
GROUP_CH = D_INNER // S_GROUPS
EXP_PER_GROUP = N_EXPERTS // N_EXP_GROUPS
MOE_BLOCK = 256
CHUNK_ROWS = 16
ROW_TILE = 256
SUB_CHUNKS = ROW_TILE // CHUNK_ROWS
TILE_CHUNKS = 2 * SUB_CHUNKS
DECODE_RING = 4
COMBINE_RING = 4
VMEM_LIMIT_BYTES = 56 * 1024 * 1024

NEG_INF = float("-inf")
LOG2_E = 1.4426950408889634


def _params(*semantics):
    return pltpu.CompilerParams(dimension_semantics=semantics, vmem_limit_bytes=VMEM_LIMIT_BYTES)


def _const_spec(arr):
    nd = arr.ndim
    return pl.BlockSpec(arr.shape, lambda *_: (0,) * nd)


def _dot(a, b):
    return jnp.dot(a, b, preferred_element_type=F32)


def _dot_nt(a, b):
    return lax.dot_general(a, b, (((1,), (1,)), ((), ())), preferred_element_type=F32)


def _dot_tn(a, b):
    return lax.dot_general(a, b, (((0,), (0,)), ((), ())), preferred_element_type=F32)


def _rms(x, g):
    return x * lax.rsqrt(jnp.mean(x * x, axis=-1, keepdims=True) + EPS) * g


def _silu(x):
    return x * (1.0 / (1.0 + jnp.exp(-x)))


def _sigmoid(x):
    return 1.0 / (1.0 + jnp.exp(-x))


def _softplus(x):
    return jnp.maximum(x, 0.0) + jnp.log(1.0 + jnp.exp(-jnp.abs(x)))


def _split3(x):
    hi = x.astype(BF16)
    r1 = x - hi.astype(F32)
    mid = r1.astype(BF16)
    lo = (r1 - mid.astype(F32)).astype(BF16)
    return hi, mid, lo


def _adaln_kernel(c_ref, w_ref, b_ref, o_ref):
    c = _silu(c_ref[...]).astype(BF16)
    o_ref[...] = _dot(c, w_ref[...].astype(BF16)) + b_ref[...]


def _adaln(c_all, w_ada, b_ada):
    rows, d = c_all.shape
    n = w_ada.shape[1]
    tn = 512
    return pl.pallas_call(
        _adaln_kernel,
        grid=(n // tn,),
        in_specs=[pl.BlockSpec((rows, d), lambda j: (0, 0)),
                  pl.BlockSpec((d, tn), lambda j: (0, j)),
                  pl.BlockSpec((1, tn), lambda j: (0, j))],
        out_specs=pl.BlockSpec((rows, tn), lambda j: (0, j)),
        out_shape=jax.ShapeDtypeStruct((rows, n), F32),
        compiler_params=_params("arbitrary"),
        name="adaln",
    )(c_all, w_ada, b_ada)


def _premix_kernel(x_ref, sh_ref, sc_ref, g_ref, ct_ref, st_ref,
                   wqa, wkva, wkpe, wkper, wz, wxbc, wdt, wdtt, wga, wgb,
                   gqa, wq, wqr, gkv, wuk, wuv,
                   q_o, k_o, v_o, ckv_o, kpe_o, z_o, xbc_o, dt_o, dtt_o, ga_o, gb_o):
    x = x_ref[0]
    h = _rms(x, g_ref[...]) * (1.0 + sc_ref[0]) + sh_ref[0]
    hb = h.astype(BF16)
    ct = ct_ref[...]
    st = st_ref[...]

    qn = _rms(_dot(hb, wqa[...]), gqa[...]).astype(BF16)
    q = _dot(qn, wq[...])
    qr = _dot(qn, wqr[...])
    for hd in range(A_HEADS):
        sl = slice(hd * HEAD_PAD, (hd + 1) * HEAD_PAD)
        q_o[0, :, sl] = (q[:, sl] * ct + qr[:, sl] * st).astype(BF16)

    ckv = _rms(_dot(hb, wkva[...]), gkv[...])
    ckv_o[0] = ckv
    cb = ckv.astype(BF16)
    kpe = _dot(hb, wkpe[...]) * ct + _dot(hb, wkper[...]) * st
    kpe_o[0] = kpe
    kn = _dot(cb, wuk[...])
    for hd in range(A_HEADS):
        sl = slice(hd * HEAD_PAD, (hd + 1) * HEAD_PAD)
        k_o[0, :, sl] = (kn[:, sl] + kpe).astype(BF16)
    v_o[0] = _dot(cb, wuv[...]).astype(BF16)

    z_o[0] = _dot(hb, wz[...])
    xbc_o[0] = _dot(hb, wxbc[...])
    dt_o[0] = _dot(hb, wdt[...])
    dtt_o[0] = _dot_nt(wdtt[...], hb)
    ga_o[0] = _dot(hb, wga[...])
    gb_o[0] = _dot(hb, wgb[...])


def _premix(x, sh, sc, g, ctab, stab, weights, tm):
    b, s, d = x.shape
    per_row_mod = sh.shape[1] != 1
    per_row_tab = ctab.shape[0] != 1
    mod_spec = (pl.BlockSpec((1, tm, d), lambda i, j: (i, j, 0)) if per_row_mod
                else pl.BlockSpec((1, 1, d), lambda i, j: (i, 0, 0)))
    tab_spec = (pl.BlockSpec((tm, LANES), lambda i, j: (j, 0)) if per_row_tab
                else pl.BlockSpec((1, LANES), lambda i, j: (0, 0)))

    def tok(n):
        return pl.BlockSpec((1, tm, n), lambda i, j: (i, j, 0))

    out_cols = [(A_HEADS * HEAD_PAD, BF16), (A_HEADS * HEAD_PAD, BF16), (A_HEADS * V_DIM, BF16),
                (KV_LORA, F32), (LANES, F32), (D_INNER, F32), (CONV_CH, F32), (LANES, F32)]
    out_shape = [jax.ShapeDtypeStruct((b, s, n), dt) for n, dt in out_cols]
    out_specs = [tok(n) for n, _ in out_cols]
    out_shape.append(jax.ShapeDtypeStruct((b, S_HEADS, s), F32))
    out_specs.append(pl.BlockSpec((1, S_HEADS, tm), lambda i, j: (i, 0, j)))
    out_shape += [jax.ShapeDtypeStruct((b, s, D_MODEL), F32)] * 2
    out_specs += [tok(D_MODEL)] * 2
    order = [out_shape[i] for i in (0, 1, 2, 3, 4, 5, 6, 7, 8, 9, 10)]
    return pl.pallas_call(
        _premix_kernel,
        grid=(b, s // tm),
        in_specs=[tok(d), mod_spec, mod_spec, _const_spec(g), tab_spec, tab_spec]
        + [_const_spec(w) for w in weights],
        out_specs=out_specs,
        out_shape=order,
        compiler_params=_params("arbitrary", "arbitrary"),
        name="premix",
    )(x, sh, sc, g, ctab, stab, *weights)


def _flash_kernel(q_ref, k_ref, v_ref, o_ref, m_ref, l_ref, acc_ref, *, tq, heads):
    qi = pl.program_id(2)
    m_ref[...] = jnp.full(m_ref.shape, NEG_INF, F32)
    l_ref[...] = jnp.zeros(l_ref.shape, F32)
    acc_ref[...] = jnp.zeros(acc_ref.shape, F32)
    row = lax.broadcasted_iota(jnp.int32, (tq, tq), 0)
    col = lax.broadcasted_iota(jnp.int32, (tq, tq), 1)
    c2 = ATTN_SCALE * LOG2_E

    def tile(kj, on_diagonal):
        koff = pl.multiple_of(kj * tq, tq)
        for hh in range(heads):
            sl = slice(hh * HEAD_PAD, (hh + 1) * HEAD_PAD)
            vsl = slice((hh // 2) * 2 * V_DIM, (hh // 2 + 1) * 2 * V_DIM)
            s = _dot_nt(q_ref[0, :, sl], k_ref[0, pl.ds(koff, tq), sl])
            if on_diagonal:
                s = jnp.where(col <= row, s, NEG_INF)
            m_prev = m_ref[hh]
            m_new = jnp.maximum(m_prev, jnp.max(s, axis=-1, keepdims=True))
            alpha = jnp.exp2((m_prev - m_new) * c2)
            p_parts = [jnp.exp2((s[:, j * LANES:(j + 1) * LANES] - m_new) * c2)
                       for j in range(tq // LANES)]
            l_ref[hh] = alpha * l_ref[hh] + sum(p_parts)
            p = jnp.concatenate(p_parts, axis=1).astype(BF16)
            acc_ref[hh] = alpha * acc_ref[hh] + _dot(p, v_ref[0, pl.ds(koff, tq), vsl])
            m_ref[hh] = m_new

    def body(kj, carry):
        tile(kj, False)
        return carry

    lax.fori_loop(0, qi, body, 0)
    tile(qi, True)
    lane = lax.broadcasted_iota(jnp.int32, (tq, 2 * V_DIM), 1)
    for pair in range(heads // 2):
        o0 = acc_ref[2 * pair] / jnp.sum(l_ref[2 * pair], axis=-1, keepdims=True)
        o1 = acc_ref[2 * pair + 1] / jnp.sum(l_ref[2 * pair + 1], axis=-1, keepdims=True)
        o_ref[0, :, pair * 2 * V_DIM:(pair + 1) * 2 * V_DIM] = jnp.where(lane < V_DIM, o0, o1).astype(BF16)


def _flash_attention(q, k, v, tq, heads):
    b, s, _ = q.shape
    return pl.pallas_call(
        functools.partial(_flash_kernel, tq=tq, heads=heads),
        grid=(b, A_HEADS // heads, s // tq),
        in_specs=[pl.BlockSpec((1, tq, heads * HEAD_PAD), lambda bi, hg, qi: (bi, qi, hg)),
                  pl.BlockSpec((1, s, heads * HEAD_PAD), lambda bi, hg, qi: (bi, 0, hg)),
                  pl.BlockSpec((1, s, heads * V_DIM), lambda bi, hg, qi: (bi, 0, hg))],
        out_specs=pl.BlockSpec((1, tq, heads * V_DIM), lambda bi, hg, qi: (bi, qi, hg)),
        out_shape=jax.ShapeDtypeStruct((b, s, A_HEADS * V_DIM), BF16),
        scratch_shapes=[pltpu.VMEM((heads, tq, LANES), F32), pltpu.VMEM((heads, tq, LANES), F32),
                        pltpu.VMEM((heads, tq, 2 * V_DIM), F32)],
        compiler_params=_params("arbitrary", "arbitrary", "arbitrary"),
        name="flash_attention",
    )(q, k, v)


def _ssd_chunk_kernel(xbc_ref, z_ref, dt_ref, dtt_ref, cw_ref, cb_ref, dtb_ref, dtbc_ref,
                      na_ref, nac_ref, dsk_ref, gs_ref, e3_ref, y_o, st_o, xc_ref, state_ref, y_ref):
    c = pl.program_id(1)
    nc = pl.num_programs(1)

    @pl.when(c == 0)
    def _():
        xc_ref[0:8, :] = jnp.zeros((8, CONV_CH), F32)
        state_ref[...] = jnp.zeros(state_ref.shape, F32)

    xc_ref[8:8 + CHUNK, :] = xbc_ref[0]
    conv = cb_ref[...] + cw_ref[CONV_W - 1:CONV_W, :] * xc_ref[8:8 + CHUNK, :]
    for j in range(1, CONV_W):
        conv = conv + cw_ref[CONV_W - 1 - j:CONV_W - j, :] * xc_ref[8 - j:8 - j + CHUNK, :]
    xc_ref[0:8, :] = xc_ref[CHUNK:CHUNK + 8, :]
    xbc = _silu(conv)

    dt_c = _softplus(dt_ref[0] + dtb_ref[...])
    a_c = dt_c * na_ref[...]
    dt_r = _softplus(dtt_ref[0] + dtbc_ref[...])
    a_r = dt_r * nac_ref[...]
    li = lax.broadcasted_iota(jnp.int32, (CHUNK, CHUNK), 0)
    si = lax.broadcasted_iota(jnp.int32, (CHUNK, CHUNK), 1)
    lower = (si <= li)
    tri = lower.astype(BF16)
    acum_c = sum(_dot(tri, part) for part in _split3(a_c))
    acum_r = sum(_dot_nt(part, tri) for part in _split3(a_r))
    a_last = acum_c[CHUNK - 1:CHUNK, :]
    dec_end = jnp.exp(a_last - acum_c)
    exp_ac = jnp.exp(acum_c)

    head_lane = lax.broadcasted_iota(jnp.int32, (CHUNK, LANES), 1) < S_HEADS

    def per_channel(v):
        hi, mid, lo = (part.astype(F32) for part in _split3(jnp.where(head_lane, v, 0.0)))
        packed = hi + pltpu.roll(mid, S_HEADS, axis=1) + pltpu.roll(lo, 2 * S_HEADS, axis=1)
        return _dot(packed.astype(BF16), e3_ref[...])

    dt_x = per_channel(dt_c)
    w_end_x = per_channel(dt_c * dec_end)
    ex_x = per_channel(exp_ac)
    xs_all = xbc[:, :D_INNER]
    xdt_all = (xs_all * dt_x).astype(BF16)
    xde_all = (xs_all * w_end_x).astype(BF16)
    pair_lane = lax.broadcasted_iota(jnp.int32, (CHUNK, 2 * S_HEAD_DIM), 1) < S_HEAD_DIM

    for g in range(S_GROUPS):
        bm = xbc[:, D_INNER + g * D_STATE:D_INNER + (g + 1) * D_STATE]
        cm = xbc[:, D_INNER + (S_GROUPS + g) * D_STATE:D_INNER + (S_GROUPS + g + 1) * D_STATE]
        bmb = bm.astype(BF16)
        cmb = cm.astype(BF16)
        gmat = _dot_nt(cmb, bmb)
        gsl = slice(g * GROUP_CH, (g + 1) * GROUP_CH)
        y_off = _dot(cmb, state_ref[:, gsl].astype(BF16)) * ex_x[:, gsl]
        for pr in range(GROUP_HEADS // 2):
            h0 = g * GROUP_HEADS + 2 * pr
            psl = slice(h0 * S_HEAD_DIM, (h0 + 2) * S_HEAD_DIM)
            xdt_pair = xdt_all[:, psl]
            halves = []
            for hd in (h0, h0 + 1):
                seg = acum_c[:, hd:hd + 1] - acum_r[hd:hd + 1, :]
                lmat = jnp.exp(jnp.where(lower, seg, NEG_INF))
                halves.append(_dot((gmat * lmat).astype(BF16), xdt_pair))
            lsl = slice(2 * pr * S_HEAD_DIM, (2 * pr + 2) * S_HEAD_DIM)
            y_ref[:, psl] = jnp.where(pair_lane, halves[0], halves[1]) + y_off[:, lsl]
        new_states = _dot_tn(bmb, xde_all[:, gsl])
        state_ref[:, gsl] = state_ref[:, gsl] * ex_x[CHUNK - 1:CHUNK, gsl] + new_states

    y = y_ref[...] + dsk_ref[...] * xbc[:, :D_INNER]
    y = y * _silu(z_ref[0])
    for g in range(S_GROUPS):
        gsl = slice(g * GROUP_CH, (g + 1) * GROUP_CH)
        y_o[0, :, gsl] = _rms(y[:, gsl], gs_ref[:, gsl]).astype(BF16)

    @pl.when(c == nc - 1)
    def _():
        st_o[0] = state_ref[...]


def _ssd_prompt(xbc, z, dt, dtt, conv_w, conv_b, dtb_row, dtb_col, na_row, na_col, dskip, g_ssd):
    b, s, _ = xbc.shape
    nc = s // CHUNK
    r = jnp.arange(LANES)[:, None]
    ch = jnp.arange(D_INNER)[None, :]
    e3 = jnp.logical_and(r < 3 * S_HEADS, r % S_HEADS == ch // S_HEAD_DIM).astype(BF16)
    consts = (conv_w, conv_b, dtb_row, dtb_col, na_row, na_col, dskip, g_ssd, e3)
    return pl.pallas_call(
        _ssd_chunk_kernel,
        grid=(b, nc),
        in_specs=[pl.BlockSpec((1, CHUNK, CONV_CH), lambda i, j: (i, j, 0)),
                  pl.BlockSpec((1, CHUNK, D_INNER), lambda i, j: (i, j, 0)),
                  pl.BlockSpec((1, CHUNK, LANES), lambda i, j: (i, j, 0)),
                  pl.BlockSpec((1, S_HEADS, CHUNK), lambda i, j: (i, 0, j))]
        + [_const_spec(a) for a in consts],
        out_specs=[pl.BlockSpec((1, CHUNK, D_INNER), lambda i, j: (i, j, 0)),
                   pl.BlockSpec((1, D_STATE, D_INNER), lambda i, j: (i, 0, 0))],
        out_shape=[jax.ShapeDtypeStruct((b, s, D_INNER), BF16),
                   jax.ShapeDtypeStruct((b, D_STATE, D_INNER), F32)],
        scratch_shapes=[pltpu.VMEM((CHUNK + 8, CONV_CH), F32),
                        pltpu.VMEM((D_STATE, D_INNER), F32),
                        pltpu.VMEM((CHUNK, D_INNER), F32)],
        compiler_params=_params("arbitrary", "arbitrary"),
        name="ssd_prompt",
    )(xbc, z, dt, dtt, *consts)


def _merge_kernel(x_ref, oa_ref, ys_ref, ga_ref, gb_ref, g1_ref, sh2_ref, sc2_ref,
                  woa, wos, wout, gpost, gpre, x1_o, h2_o):
    o_attn = _dot(oa_ref[0], woa[...])
    o_ssd = _dot(ys_ref[0], wos[...])
    merged = _sigmoid(ga_ref[0]) * o_attn + _sigmoid(gb_ref[0]) * o_ssd
    m = _dot(merged.astype(BF16), wout[...])
    x1 = x_ref[0] + g1_ref[0] * _rms(m, gpost[...])
    x1_o[0] = x1
    h2_o[0] = (_rms(x1, gpre[...]) * (1.0 + sc2_ref[0]) + sh2_ref[0]).astype(BF16)


def _merge(x, o_attn, y_ssd, ga, gb, g1, sh2, sc2, weights, tm):
    b, s, d = x.shape
    per_row_mod = g1.shape[1] != 1
    mod_spec = (pl.BlockSpec((1, tm, d), lambda i, j: (i, j, 0)) if per_row_mod
                else pl.BlockSpec((1, 1, d), lambda i, j: (i, 0, 0)))

    def tok(n):
        return pl.BlockSpec((1, tm, n), lambda i, j: (i, j, 0))

    return pl.pallas_call(
        _merge_kernel,
        grid=(b, s // tm),
        in_specs=[tok(d), tok(A_HEADS * V_DIM), tok(D_INNER), tok(d), tok(d),
                  mod_spec, mod_spec, mod_spec] + [_const_spec(w) for w in weights],
        out_specs=[tok(d), tok(d)],
        out_shape=[jax.ShapeDtypeStruct((b, s, d), F32), jax.ShapeDtypeStruct((b, s, d), BF16)],
        compiler_params=_params("arbitrary", "arbitrary"),
        name="merge",
    )(x, o_attn, y_ssd, ga, gb, g1, sh2, sc2, *weights)


def _first_max(vals, idx, big):
    m = vals[0]
    for v in vals[1:]:
        m = jnp.maximum(m, v)
    m = jnp.max(m, axis=0, keepdims=True)
    cand = [jnp.where(v == m, i, big) for v, i in zip(vals, idx)]
    a = cand[0]
    for cnd in cand[1:]:
        a = jnp.minimum(a, cnd)
    a = jnp.min(a, axis=0, keepdims=True)
    return m, a


def _router_kernel(h_ref, wt_ref, b_ref, o_ref, slot_o, w_o, cnt_o, *, tm):
    logits = _dot_nt(wt_ref[...], h_ref[0])
    scores = _sigmoid(logits)
    biased = scores + b_ref[...]
    sub = lax.broadcasted_iota(jnp.int32, (EXP_PER_GROUP, tm), 0)
    slabs = [biased[g * EXP_PER_GROUP:(g + 1) * EXP_PER_GROUP, :] for g in range(N_EXP_GROUPS)]
    big = jnp.int32(N_EXPERTS)

    gscore = []
    for g in range(N_EXP_GROUPS):
        m1, a1 = _first_max([slabs[g]], [sub], big)
        rest = jnp.where(sub == a1, NEG_INF, slabs[g])
        m2 = jnp.max(rest, axis=0, keepdims=True)
        gscore.append(m1 + m2)
    gs = jnp.full((N_EXP_GROUPS, tm), NEG_INF, F32)
    for g in range(N_EXP_GROUPS):
        gs = jnp.where(sub == g, gscore[g], gs)
    gsel = jnp.zeros((N_EXP_GROUPS, tm), F32)
    for _ in range(TOPK_GROUPS):
        _, a = _first_max([gs], [sub], big)
        hit = sub == a
        gsel = jnp.where(hit, 1.0, gsel)
        gs = jnp.where(hit, NEG_INF, gs)

    masked = [jnp.where(gsel[g:g + 1, :] > 0.5, slabs[g], NEG_INF) for g in range(N_EXP_GROUPS)]
    flat = [sub + g * EXP_PER_GROUP for g in range(N_EXP_GROUPS)]
    chosen = [jnp.zeros((EXP_PER_GROUP, tm), jnp.bool_) for _ in range(N_EXP_GROUPS)]
    picks = []
    for _ in range(TOP_K):
        _, a = _first_max(masked, flat, big)
        picks.append(a)
        for g in range(N_EXP_GROUPS):
            hit = flat[g] == a
            chosen[g] = jnp.logical_or(chosen[g], hit)
            masked[g] = jnp.where(hit, NEG_INF, masked[g])
    score_slabs = [scores[g * EXP_PER_GROUP:(g + 1) * EXP_PER_GROUP, :] for g in range(N_EXP_GROUPS)]
    w = [jnp.where(chosen[g], score_slabs[g], 0.0) for g in range(N_EXP_GROUPS)]
    tot = w[0]
    for part in w[1:]:
        tot = tot + part
    tot = jnp.sum(tot, axis=0, keepdims=True)

    sel_t = jnp.concatenate([jnp.where(c, 1.0, 0.0) for c in chosen], axis=0)
    t_row = lax.broadcasted_iota(jnp.int32, (tm, tm), 0)
    t_col = lax.broadcasted_iota(jnp.int32, (tm, tm), 1)
    rank_t = _dot(sel_t.astype(BF16), (t_row < t_col).astype(BF16))
    cnt = jnp.sum(sel_t, axis=-1, keepdims=True)
    pc = jnp.floor((cnt + (CHUNK_ROWS - 1.0)) * (1.0 / CHUNK_ROWS)) * CHUNK_ROWS
    total = jnp.sum(pc, axis=0, keepdims=True)
    fill = jnp.ceil(total * (1.0 / ROW_TILE)) * ROW_TILE - total
    e_row = lax.broadcasted_iota(jnp.int32, (N_EXPERTS, 1), 0)
    pc = pc + jnp.where(e_row == N_EXPERTS - 1, fill, 0.0)
    e_r = lax.broadcasted_iota(jnp.int32, (N_EXPERTS, N_EXPERTS), 0)
    e_c = lax.broadcasted_iota(jnp.int32, (N_EXPERTS, N_EXPERTS), 1)
    pc_rep = jnp.broadcast_to(pc, (N_EXPERTS, LANES))
    off = _dot((e_c < e_r).astype(BF16), pc_rep.astype(BF16))
    slot_t = off[:, :1] + rank_t
    slot8 = jnp.full((8, tm), -1.0, F32)
    w8 = jnp.zeros((8, tm), F32)
    for k, a in enumerate(picks):
        s_acc = jnp.zeros((EXP_PER_GROUP, tm), F32)
        w_acc = jnp.zeros((EXP_PER_GROUP, tm), F32)
        for g in range(N_EXP_GROUPS):
            hit = flat[g] == a
            s_acc = jnp.where(hit, slot_t[g * EXP_PER_GROUP:(g + 1) * EXP_PER_GROUP, :], s_acc)
            w_acc = jnp.where(hit, score_slabs[g], w_acc)
        slot8 = jnp.where(sub == k, jnp.sum(s_acc, axis=0, keepdims=True), slot8)
        w8 = jnp.where(sub == k, jnp.sum(w_acc, axis=0, keepdims=True) / tot * ROUTE_SCALE, w8)

    info_t = jnp.concatenate([part / tot * ROUTE_SCALE for part in w] + [slot8, w8]
                             + [jnp.zeros((LANES - N_EXPERTS - 16, tm), F32)], axis=0)
    o_ref[0] = info_t.T
    slot_o[0] = slot8
    w_o[0] = w8
    cnt_o[0] = pc_rep


def _router(h2, w_router_t, b_col, tm):
    b, s, d = h2.shape
    nj = s // tm
    return pl.pallas_call(
        functools.partial(_router_kernel, tm=tm),
        grid=(b, nj),
        in_specs=[pl.BlockSpec((1, tm, d), lambda i, j: (i, j, 0)),
                  _const_spec(w_router_t), _const_spec(b_col)],
        out_specs=[pl.BlockSpec((1, tm, LANES), lambda i, j: (i, j, 0)),
                   pl.BlockSpec((1, 8, tm), lambda i, j: (i, 0, j)),
                   pl.BlockSpec((1, 8, tm), lambda i, j: (i, 0, j)),
                   pl.BlockSpec((1, N_EXPERTS, LANES), lambda i, j: (i * nj + j, 0, 0))],
        out_shape=[jax.ShapeDtypeStruct((b, s, LANES), F32),
                   jax.ShapeDtypeStruct((b, 8, s), F32),
                   jax.ShapeDtypeStruct((b, 8, s), F32),
                   jax.ShapeDtypeStruct((b * nj, N_EXPERTS, LANES), F32)],
        compiler_params=_params("arbitrary", "arbitrary"),
        name="router",
    )(h2, w_router_t, b_col)


def _dispatch_kernel(nk_ref, h_ref, slot_ref, w_ref, xs_ref, os_ref, *, tm, cap):
    os_ref[...] = jnp.zeros(os_ref.shape, BF16)
    nk = nk_ref[pl.program_id(0) * pl.num_programs(1) + pl.program_id(1)]
    h = h_ref[0]
    slots = slot_ref[0]
    ws = w_ref[0]
    lane = lax.broadcasted_iota(jnp.int32, (ROW_TILE, LANES), 1)
    for rt in range(cap // ROW_TILE):
        rsl = slice(rt * ROW_TILE, (rt + 1) * ROW_TILE)

        @pl.when(rt < nk)
        def _():
            rows = (rt * ROW_TILE + lax.broadcasted_iota(jnp.int32, (ROW_TILE, tm), 0)).astype(F32)
            p = jnp.zeros((ROW_TILE, tm), F32)
            pw = jnp.zeros((ROW_TILE, tm), F32)
            for k in range(TOP_K):
                hit = rows == slots[k:k + 1, :]
                p = jnp.where(hit, 1.0, p)
                pw = jnp.where(hit, ws[k:k + 1, :], pw)
            x = _dot(p.astype(BF16), h)
            hi, mid, lo = (part.astype(F32) for part in _split3(jnp.sum(pw, axis=-1, keepdims=True)))
            extra = jnp.where(lane == 0, hi, jnp.where(lane == 1, mid, jnp.where(lane == 2, lo, 0.0)))
            xs_ref[0, rsl, :D_MODEL] = x.astype(BF16)
            xs_ref[0, rsl, D_MODEL:] = extra.astype(BF16)

        @pl.when(rt >= nk)
        def _():
            xs_ref[0, rsl, :] = jnp.zeros((ROW_TILE, D_MODEL + LANES), BF16)


def _dispatch(nk, h2, slotk, wk, tm, cap):
    b, s, d = h2.shape
    nj = s // tm
    grid_spec = pltpu.PrefetchScalarGridSpec(
        num_scalar_prefetch=1,
        grid=(b, nj),
        in_specs=[pl.BlockSpec((1, tm, d), lambda i, j, n_: (i, j, 0)),
                  pl.BlockSpec((1, 8, tm), lambda i, j, n_: (i, 0, j)),
                  pl.BlockSpec((1, 8, tm), lambda i, j, n_: (i, 0, j))],
        out_specs=[pl.BlockSpec((1, cap, d + LANES), lambda i, j, n_: (i * nj + j, 0, 0)),
                   pl.BlockSpec((1, cap, d), lambda i, j, n_: (i * nj + j, 0, 0))],
    )
    return pl.pallas_call(
        functools.partial(_dispatch_kernel, tm=tm, cap=cap),
        grid_spec=grid_spec,
        out_shape=[jax.ShapeDtypeStruct((b * nj, cap, d + LANES), BF16),
                   jax.ShapeDtypeStruct((b * nj, cap, d), BF16)],
        compiler_params=_params("arbitrary", "arbitrary"),
        name="moe_dispatch",
    )(nk, h2, slotk, wk)


def _expert_kernel(te_ref, src_ref, dst_ref, nt_ref, xs_hbm, wg_ref, wu_ref, wd_ref, os_zero_hbm,
                   os_hbm, xbuf, obuf, wgb, wub, wdb, sem):
    del os_zero_hbm
    i = pl.program_id(0)
    nt = nt_ref[0]

    def in_copies(tile, slot):
        return [pltpu.make_async_copy(xs_hbm.at[src_ref[tile * TILE_CHUNKS + q]], xbuf.at[slot, q],
                                      sem.at[0, slot]) for q in range(TILE_CHUNKS)]

    def out_copies(tile, slot):
        return [pltpu.make_async_copy(obuf.at[slot, q], os_hbm.at[dst_ref[tile * TILE_CHUNKS + q]],
                                      sem.at[1, slot]) for q in range(TILE_CHUNKS)]

    @pl.when(i < nt)
    def _():
        slot = lax.rem(i, 2)

        @pl.when(i == 0)
        def _():
            for cpy in in_copies(0, 0):
                cpy.start()

        @pl.when(i + 1 < nt)
        def _():
            for cpy in in_copies(i + 1, 1 - slot):
                cpy.start()

        @pl.when(jnp.logical_or(i == 0, te_ref[i] != te_ref[jnp.maximum(i - 1, 0)]))
        def _():
            wgb[...] = wg_ref[0].astype(BF16)
            wub[...] = wu_ref[0].astype(BF16)
            wdb[...] = wd_ref[0].astype(BF16)

        for cpy in in_copies(i, slot):
            cpy.wait()
        outs = []
        for half in range(TILE_CHUNKS // SUB_CHUNKS):
            csl = slice(half * SUB_CHUNKS, (half + 1) * SUB_CHUNKS)
            x = xbuf[slot, csl].reshape(ROW_TILE, D_MODEL + LANES)
            xb = x[:, :D_MODEL]
            wrow = jnp.sum(x[:, D_MODEL:].astype(F32), axis=-1, keepdims=True)
            gte = _dot(xb, wgb[...])
            upe = _dot(xb, wub[...])
            act = (_silu(gte) * upe * wrow).astype(BF16)
            outs.append(_dot(act, wdb[...]).astype(BF16))

        @pl.when(i >= 2)
        def _():
            for cpy in out_copies(i - 2, slot):
                cpy.wait()

        for half, out in enumerate(outs):
            csl = slice(half * SUB_CHUNKS, (half + 1) * SUB_CHUNKS)
            obuf[slot, csl] = out.reshape(SUB_CHUNKS, CHUNK_ROWS, D_MODEL)
        for cpy in out_copies(i, slot):
            cpy.start()

        @pl.when(i == nt - 1)
        def _():
            @pl.when(i >= 1)
            def _():
                for cpy in out_copies(i - 1, 1 - slot):
                    cpy.wait()

            for cpy in out_copies(i, slot):
                cpy.wait()


def _experts(tile_expert, src, dst, n_tiles, xs_chunks, os_zero, w_gate_e, w_up_e, w_down_e, max_tiles):
    n_chunks, _, width = xs_chunks.shape
    d = width - LANES
    grid_spec = pltpu.PrefetchScalarGridSpec(
        num_scalar_prefetch=4,
        grid=(max_tiles,),
        in_specs=[pl.BlockSpec(memory_space=pl.ANY),
                  pl.BlockSpec((1, d, F_EXPERT), lambda i, te, s_, d_, n_: (te[i], 0, 0)),
                  pl.BlockSpec((1, d, F_EXPERT), lambda i, te, s_, d_, n_: (te[i], 0, 0)),
                  pl.BlockSpec((1, F_EXPERT, d), lambda i, te, s_, d_, n_: (te[i], 0, 0)),
                  pl.BlockSpec(memory_space=pl.ANY)],
        out_specs=pl.BlockSpec(memory_space=pl.ANY),
        scratch_shapes=[pltpu.VMEM((2, TILE_CHUNKS, CHUNK_ROWS, width), BF16),
                        pltpu.VMEM((2, TILE_CHUNKS, CHUNK_ROWS, d), BF16),
                        pltpu.VMEM((d, F_EXPERT), BF16), pltpu.VMEM((d, F_EXPERT), BF16),
                        pltpu.VMEM((F_EXPERT, d), BF16),
                        pltpu.SemaphoreType.DMA((2, 2))],
    )
    return pl.pallas_call(
        _expert_kernel,
        grid_spec=grid_spec,
        out_shape=jax.ShapeDtypeStruct((n_chunks, CHUNK_ROWS, d), BF16),
        input_output_aliases={8: 0},
        compiler_params=_params("arbitrary"),
        name="moe_experts",
    )(tile_expert, src, dst, n_tiles, xs_chunks, w_gate_e, w_up_e, w_down_e, os_zero)


def _combine_kernel(nk_ref, cum_ref, info_ref, h_ref, x1_ref, g2_ref, os_hbm, wgs, wus, wds, gpost,
                    y_o, obuf, sem, *, tm):
    nj = pl.num_programs(1)
    blk = pl.program_id(0) * nj + pl.program_id(1)
    nblk = pl.num_programs(0) * nj
    nk = nk_ref[blk]

    def fetch(bk, kt, slot):
        rows = pl.ds(pl.multiple_of(kt * ROW_TILE, ROW_TILE), ROW_TILE)
        return pltpu.make_async_copy(os_hbm.at[bk, rows], obuf.at[slot], sem.at[slot])

    ahead = COMBINE_RING - 1
    assert TOP_K * tm >= ahead * ROW_TILE

    @pl.when(blk == 0)
    def _():
        for t in range(ahead):
            fetch(0, t, t).start()

    hb = h_ref[0]
    hid = _silu(_dot(hb, wgs[...])) * _dot(hb, wus[...])
    acc0 = _dot(hid.astype(BF16), wds[...])
    info = info_ref[0]
    slot_cols = [jnp.broadcast_to(info[:, N_EXPERTS + k:N_EXPERTS + k + 1], (tm, ROW_TILE))
                 for k in range(TOP_K)]
    lane_row = lax.broadcasted_iota(jnp.int32, (tm, ROW_TILE), 1).astype(F32)

    def body(kt, acc):
        g = cum_ref[blk] + kt
        slot = lax.rem(g, COMBINE_RING)
        nslot = lax.rem(g + ahead, COMBINE_RING)

        @pl.when(kt + ahead < nk)
        def _():
            fetch(blk, kt + ahead, nslot).start()

        @pl.when(jnp.logical_and(kt + ahead >= nk, blk + 1 < nblk))
        def _():
            fetch(blk + 1, kt + ahead - nk, nslot).start()

        fetch(blk, kt, slot).wait()
        rows = lane_row + (kt * ROW_TILE).astype(F32)
        pt = jnp.zeros((tm, ROW_TILE), F32)
        for col in slot_cols:
            pt = jnp.where(rows == col, 1.0, pt)
        return acc + _dot(pt.astype(BF16), obuf[slot])

    acc = lax.fori_loop(0, nk, body, acc0)
    y_o[0] = x1_ref[0] + g2_ref[0] * _rms(acc, gpost[...])


def _combine(nk, cum_nk, info, h2, x1, g2, os_blocks, shared, gpost, tm):
    b, s, d = h2.shape
    nj = s // tm

    def tok(n):
        return pl.BlockSpec((1, tm, n), lambda i, j, *_: (i, j, 0))

    grid_spec = pltpu.PrefetchScalarGridSpec(
        num_scalar_prefetch=2,
        grid=(b, nj),
        in_specs=[tok(LANES), tok(d), tok(d), pl.BlockSpec((1, 1, d), lambda i, j, *_: (i, 0, 0)),
                  pl.BlockSpec(memory_space=pl.ANY)]
        + [_const_spec(w) for w in shared] + [_const_spec(gpost)],
        out_specs=tok(d),
        scratch_shapes=[pltpu.VMEM((COMBINE_RING, ROW_TILE, d), BF16),
                        pltpu.SemaphoreType.DMA((COMBINE_RING,))],
    )
    return pl.pallas_call(
        functools.partial(_combine_kernel, tm=tm),
        grid_spec=grid_spec,
        out_shape=jax.ShapeDtypeStruct((b, s, d), F32),
        compiler_params=_params("arbitrary", "arbitrary"),
        name="moe_combine",
    )(nk, cum_nk, info, h2, x1, g2, os_blocks, *shared, gpost)


def _moe_plan(pc_rep, cap_chunks, max_tiles):
    pc = pc_rep[:, :, 0].astype(jnp.int32)
    nblk = pc.shape[0]
    nch = pc // CHUNK_ROWS
    off = jnp.cumsum(nch, axis=1) - nch
    nk = jnp.sum(pc, axis=1) // ROW_TILE
    cum_nk = jnp.cumsum(nk) - nk
    cum_blk = jnp.cumsum(nch, axis=0)
    tot = cum_blk[-1]
    tiles = (tot + TILE_CHUNKS - 1) // TILE_CHUNKS
    tile_end = jnp.cumsum(tiles)
    n_tiles = tile_end[-1]
    tile_ids = jnp.arange(max_tiles, dtype=jnp.int32)
    last_expert = jnp.sum((tile_end < n_tiles).astype(jnp.int32))
    te = jnp.sum((tile_end[None, :] <= tile_ids[:, None]).astype(jnp.int32), axis=1)
    te = jnp.minimum(te, last_expert)
    oh_e = (te[:, None] == jnp.arange(N_EXPERTS, dtype=jnp.int32)[None, :]).astype(jnp.int32)

    def by_expert(v):
        return jnp.sum(oh_e * v[None, :], axis=1) if v.ndim == 1 else jnp.sum(
            oh_e[:, :, None] * v[None, :, :], axis=1)

    pos = ((tile_ids - by_expert(tile_end - tiles))[:, None] * TILE_CHUNKS
           + jnp.arange(TILE_CHUNKS, dtype=jnp.int32)[None, :])
    valid = jnp.logical_and(tile_ids[:, None] < n_tiles, pos < by_expert(tot)[:, None])
    cum_e = by_expert(cum_blk.T)
    nch_e = by_expert(nch.T)
    off_e = by_expert(off.T)
    blk = jnp.sum((cum_e[:, None, :] <= pos[:, :, None]).astype(jnp.int32), axis=-1)
    blk = jnp.minimum(blk, nblk - 1)
    oh_b = (blk[:, :, None] == jnp.arange(nblk, dtype=jnp.int32)[None, None, :]).astype(jnp.int32)
    before = jnp.sum(oh_b * (cum_e - nch_e)[:, None, :], axis=-1)
    local = jnp.sum(oh_b * off_e[:, None, :], axis=-1) + (pos - before)
    chunk = blk * cap_chunks + local
    assert nblk * SUB_CHUNKS >= TILE_CHUNKS
    q = jnp.arange(TILE_CHUNKS, dtype=jnp.int32)[None, :]
    spare = (q // SUB_CHUNKS) * cap_chunks + cap_chunks - SUB_CHUNKS + q % SUB_CHUNKS
    src = jnp.where(valid, chunk, cap_chunks - 1)
    dst = jnp.where(valid, chunk, spare)
    return (te.astype(jnp.int32), src.reshape(-1).astype(jnp.int32), dst.reshape(-1).astype(jnp.int32),
            n_tiles.reshape(1).astype(jnp.int32), nk.astype(jnp.int32), cum_nk.astype(jnp.int32))


def _sparse_moe(h2, info, slotk, wk, pc_rep, x1, g2, w_gate_e, w_up_e, w_down_e, shared, gpost, tm):
    b, s, d = h2.shape
    nblk = b * (s // tm)
    cap = TOP_K * tm + N_EXPERTS * CHUNK_ROWS + ROW_TILE
    cap = -(-cap // ROW_TILE) * ROW_TILE
    cap_chunks = cap // CHUNK_ROWS
    max_tiles = nblk * (cap_chunks - SUB_CHUNKS) // TILE_CHUNKS + N_EXPERTS
    te, src, dst, n_tiles, nk, cum_nk = _moe_plan(pc_rep, cap_chunks, max_tiles)
    xs, os_zero = _dispatch(nk, h2, slotk, wk, tm, cap)
    os_chunks = _experts(te, src, dst, n_tiles, xs.reshape(nblk * cap_chunks, CHUNK_ROWS, d + LANES),
                         os_zero.reshape(nblk * cap_chunks, CHUNK_ROWS, d),
                         w_gate_e, w_up_e, w_down_e, max_tiles)
    return _combine(nk, cum_nk, info, h2, x1, g2, os_chunks.reshape(nblk, cap, d), shared, gpost, tm)


def _moe_kernel(h_ref, c_ref, x1_ref, g2_ref, wg_ref, wu_ref, wd_ref, wgs, wus, wds, gpost,
                y_o, acc_ref, *, tm):
    e = pl.program_id(2)
    ne = pl.num_programs(2)
    hb = h_ref[0]

    @pl.when(e == 0)
    def _():
        hid = _silu(_dot(hb, wgs[...])) * _dot(hb, wus[...])
        acc_ref[...] = _dot(hid.astype(BF16), wds[...])

    lane = lax.broadcasted_iota(jnp.int32, (tm, LANES), 1)
    ce = jnp.sum(jnp.where(lane == e, c_ref[0], 0.0), axis=-1, keepdims=True)
    gte = _dot(hb, wg_ref[0].astype(BF16))
    upe = _dot(hb, wu_ref[0].astype(BF16))
    act = (_silu(gte) * upe * ce).astype(BF16)
    acc_ref[...] += _dot(act, wd_ref[0].astype(BF16))

    @pl.when(e == ne - 1)
    def _():
        y_o[0] = x1_ref[0] + g2_ref[0] * _rms(acc_ref[...], gpost[...])


def _moe(h2, comb, x1, g2, w_gate_e, w_up_e, w_down_e, shared, gpost, tm):
    b, s, d = h2.shape
    per_row_mod = g2.shape[1] != 1
    mod_spec = (pl.BlockSpec((1, tm, d), lambda i, j, e: (i, j, 0)) if per_row_mod
                else pl.BlockSpec((1, 1, d), lambda i, j, e: (i, 0, 0)))

    def tok(n):
        return pl.BlockSpec((1, tm, n), lambda i, j, e: (i, j, 0))

    return pl.pallas_call(
        functools.partial(_moe_kernel, tm=tm),
        grid=(b, s // tm, N_EXPERTS),
        in_specs=[tok(d), tok(LANES), tok(d), mod_spec,
                  pl.BlockSpec((1, d, F_EXPERT), lambda i, j, e: (e, 0, 0)),
                  pl.BlockSpec((1, d, F_EXPERT), lambda i, j, e: (e, 0, 0)),
                  pl.BlockSpec((1, F_EXPERT, d), lambda i, j, e: (e, 0, 0))]
        + [_const_spec(w) for w in shared] + [_const_spec(gpost)],
        out_specs=tok(d),
        out_shape=jax.ShapeDtypeStruct((b, s, d), F32),
        scratch_shapes=[pltpu.VMEM((tm, d), F32)],
        compiler_params=_params("arbitrary", "arbitrary", "arbitrary"),
        name="moe",
    )(h2, comb, x1, g2, w_gate_e, w_up_e, w_down_e, *shared, gpost)


def _qlat_kernel(q_ref, wuk_ref, o_ref):
    for hd in range(A_HEADS):
        qn = q_ref[:, hd * HEAD_PAD:hd * HEAD_PAD + QK_NOPE]
        o_ref[hd] = _dot_nt(qn, wuk_ref[hd]).astype(BF16)


def _qlat(q, wuk_heads):
    nb = q.shape[0]
    return pl.pallas_call(
        _qlat_kernel,
        in_specs=[_const_spec(q), _const_spec(wuk_heads)],
        out_specs=pl.BlockSpec((A_HEADS, nb, KV_LORA), lambda: (0, 0, 0)),
        out_shape=jax.ShapeDtypeStruct((A_HEADS, nb, KV_LORA), BF16),
        grid=(),
        name="q_latent",
    )(q, wuk_heads)


def _decode_kernel(pt_ref, ql_ref, qp_ref, cn_ref, kn_ref, ckv_hbm, kpet_hbm, o_ref,
                   cbuf, kbuf, sem, *, pages_per_chunk, n_chunks):
    b = pl.program_id(0)
    nb = pl.num_programs(0)
    cp = pages_per_chunk
    halves = 2 if cp % 2 == 0 else 1
    hp = cp // halves
    total = nb * n_chunks
    ahead = DECODE_RING - 1

    def copies(g):
        g = jnp.asarray(g, jnp.int32)
        bb = lax.div(g, jnp.int32(n_chunks))
        chunk = g - bb * n_chunks
        slot = lax.rem(g, jnp.int32(DECODE_RING))
        out = []
        for p in range(cp):
            page = pt_ref[bb, chunk * cp + p]
            out.append(pltpu.make_async_copy(ckv_hbm.at[page], cbuf.at[slot, p], sem.at[0, slot]))
            out.append(pltpu.make_async_copy(kpet_hbm.at[page], kbuf.at[slot, p], sem.at[1, slot]))
        return out

    def start(g):
        for n, cpy in enumerate(copies(g)):
            cpy.start(priority=(n // 2) % 2)

    @pl.when(b == 0)
    def _():
        for g in range(ahead):
            @pl.when(g < total)
            def _():
                start(g)

    ql = ql_ref[0]
    qp = qp_ref[0]
    c_new = cn_ref[0]
    k_new = kn_ref[0]
    s_new = (jnp.sum(ql.astype(F32) * c_new, axis=-1, keepdims=True)
             + jnp.sum(qp.astype(F32) * k_new, axis=-1, keepdims=True)) * ATTN_SCALE
    m0 = s_new
    l0 = jnp.ones((A_HEADS, 1), F32)
    acc0 = jnp.broadcast_to(c_new, (A_HEADS, KV_LORA))

    def body(chunk, carry):
        m_prev, l_prev, acc = carry
        g = b * n_chunks + chunk
        slot = lax.rem(g, DECODE_RING)

        @pl.when(g + ahead < total)
        def _():
            start(g + ahead)

        for cpy in copies(g):
            cpy.wait()
        for hf in range(halves):
            kc = cbuf[slot, hf * hp:(hf + 1) * hp].reshape(hp * PAGE_SIZE, KV_LORA).astype(BF16)
            kt = jnp.concatenate([kbuf[slot, hf * hp + p] for p in range(hp)], axis=1).astype(BF16)
            s = (_dot_nt(ql, kc) + _dot(qp, kt)) * ATTN_SCALE
            m_new = jnp.maximum(m_prev, jnp.max(s, axis=-1, keepdims=True))
            alpha = jnp.exp(m_prev - m_new)
            p = jnp.exp(s - m_new)
            l_prev = alpha * l_prev + jnp.sum(p, axis=-1, keepdims=True)
            acc = alpha * acc + _dot(p.astype(BF16), kc)
            m_prev = m_new
        return m_prev, l_prev, acc

    _, l_fin, acc = lax.fori_loop(0, n_chunks, body, (m0, l0, acc0))
    o_ref[0] = acc / l_fin


def _decode_attention(page_table, qlat, qpe, ckv_new, kpe_new, cache_ckv, cache_kpe_t):
    nb, n_pages = page_table.shape
    cp = min(16, n_pages)
    n_chunks = n_pages // cp
    grid_spec = pltpu.PrefetchScalarGridSpec(
        num_scalar_prefetch=1,
        grid=(nb,),
        in_specs=[pl.BlockSpec((1, A_HEADS, KV_LORA), lambda i, pt: (i, 0, 0)),
                  pl.BlockSpec((1, A_HEADS, QK_ROPE), lambda i, pt: (i, 0, 0)),
                  pl.BlockSpec((1, 1, KV_LORA), lambda i, pt: (i, 0, 0)),
                  pl.BlockSpec((1, 1, QK_ROPE), lambda i, pt: (i, 0, 0)),
                  pl.BlockSpec(memory_space=pl.ANY),
                  pl.BlockSpec(memory_space=pl.ANY)],
        out_specs=pl.BlockSpec((1, A_HEADS, KV_LORA), lambda i, pt: (i, 0, 0)),
        scratch_shapes=[pltpu.VMEM((DECODE_RING, cp, PAGE_SIZE, KV_LORA), F32),
                        pltpu.VMEM((DECODE_RING, cp, QK_ROPE, PAGE_SIZE), F32),
                        pltpu.SemaphoreType.DMA((2, DECODE_RING))],
    )
    return pl.pallas_call(
        functools.partial(_decode_kernel, pages_per_chunk=cp, n_chunks=n_chunks),
        grid_spec=grid_spec,
        out_shape=jax.ShapeDtypeStruct((nb, A_HEADS, KV_LORA), F32),
        compiler_params=_params("arbitrary"),
        name="decode_attention",
    )(page_table, qlat, qpe, ckv_new, kpe_new, cache_ckv, cache_kpe_t)


def _vup_kernel(ol_ref, wuv_ref, o_ref):
    for hd in range(A_HEADS):
        o_ref[:, hd * V_DIM:(hd + 1) * V_DIM] = _dot(ol_ref[hd].astype(BF16), wuv_ref[hd]).astype(BF16)


def _value_up(o_lat_heads, wuv_heads):
    nb = o_lat_heads.shape[1]
    return pl.pallas_call(
        _vup_kernel,
        in_specs=[_const_spec(o_lat_heads), _const_spec(wuv_heads)],
        out_specs=pl.BlockSpec((nb, A_HEADS * V_DIM), lambda: (0, 0)),
        out_shape=jax.ShapeDtypeStruct((nb, A_HEADS * V_DIM), BF16),
        grid=(),
        name="value_up",
    )(o_lat_heads, wuv_heads)


def _ssd_step_kernel(*refs):
    for s in range(refs[0].shape[0]):
        _ssd_step_one(s, *refs)


def _ssd_step_one(s, xbc_ref, cs_ref, z_ref, dt_ref, st_ref, cw_ref, cb_ref, dtb_ref, na_ref,
                  dsk_ref, gs_ref, y_o, st_o, y_ref):
    conv = cb_ref[...] + cw_ref[CONV_W - 1:CONV_W, :] * xbc_ref[s]
    for j in range(CONV_W - 1):
        conv = conv + cw_ref[j:j + 1, :] * cs_ref[s, j:j + 1, :]
    xbc = _silu(conv)
    dt = _softplus(dt_ref[s] + dtb_ref[...])
    dec = jnp.exp(dt * na_ref[...])
    hi, mid, lo = (part.astype(F32) for part in _split3(xbc[:, :D_INNER]))
    r8 = lax.broadcasted_iota(jnp.int32, (8, D_INNER), 0)
    x8 = jnp.where(r8 == 0, hi, jnp.where(r8 == 1, mid, jnp.where(r8 == 2, lo, 0.0))).astype(BF16)
    ones8 = (lax.broadcasted_iota(jnp.int32, (8, D_STATE), 0) < 3).astype(BF16)
    x_col = _dot_tn(x8, ones8)
    for hd in range(S_HEADS):
        g = hd // GROUP_HEADS
        hsl = slice(hd * S_HEAD_DIM, (hd + 1) * S_HEAD_DIM)
        bm = xbc[:, D_INNER + g * D_STATE:D_INNER + (g + 1) * D_STATE]
        st_o[s, hd] = st_ref[s, hd] * dec[:, hd:hd + 1] + x_col[hsl, :] * (bm * dt[:, hd:hd + 1])
    for g in range(S_GROUPS):
        cm = xbc[:, D_INNER + (S_GROUPS + g) * D_STATE:D_INNER + (S_GROUPS + g + 1) * D_STATE]
        cb8 = jnp.broadcast_to(cm, (8, D_STATE)).astype(BF16)
        new_g = st_o[s, g * GROUP_HEADS:(g + 1) * GROUP_HEADS].reshape(GROUP_CH, D_STATE)
        y_ref[:, g * GROUP_CH:(g + 1) * GROUP_CH] = _dot_nt(cb8, new_g.astype(BF16))
    y = y_ref[0:1, :] + dsk_ref[...] * xbc[:, :D_INNER]
    y = y * _silu(z_ref[s])
    for g in range(S_GROUPS):
        gsl = slice(g * GROUP_CH, (g + 1) * GROUP_CH)
        y_o[s, :, gsl] = _rms(y[:, gsl], gs_ref[:, gsl]).astype(BF16)


def _ssd_sample(xbc, conv_state, z, dt, ssm_state, conv_w, conv_b, dtb_row, na_row, dskip, g_ssd):
    nb = xbc.shape[0]
    ns = _pick_tile(nb, 4)
    consts = (conv_w, conv_b, dtb_row, na_row, dskip, g_ssd)
    st_spec = pl.BlockSpec((ns, S_HEADS, S_HEAD_DIM, D_STATE), lambda i: (i, 0, 0, 0))
    return pl.pallas_call(
        _ssd_step_kernel,
        grid=(nb // ns,),
        in_specs=[pl.BlockSpec((ns, 1, CONV_CH), lambda i: (i, 0, 0)),
                  pl.BlockSpec((ns, CONV_W - 1, CONV_CH), lambda i: (i, 0, 0)),
                  pl.BlockSpec((ns, 1, D_INNER), lambda i: (i, 0, 0)),
                  pl.BlockSpec((ns, 1, LANES), lambda i: (i, 0, 0)),
                  st_spec] + [_const_spec(a) for a in consts],
        out_specs=[pl.BlockSpec((ns, 1, D_INNER), lambda i: (i, 0, 0)), st_spec],
        out_shape=[jax.ShapeDtypeStruct((nb, 1, D_INNER), BF16),
                   jax.ShapeDtypeStruct(ssm_state.shape, F32)],
        scratch_shapes=[pltpu.VMEM((8, D_INNER), F32)],
        compiler_params=_params("arbitrary"),
        name="ssd_sample",
    )(xbc, conv_state, z, dt, ssm_state, *consts)


def _rot_half(w):
    half = QK_ROPE // 2
    return jnp.concatenate([-w[..., half:], w[..., :half]], axis=-1)


def _pad_cols(w, start, total):
    return jnp.pad(w, ((0, 0), (start, total - start - w.shape[1])))


def _head_pad(w_nope, w_rope):
    k = w_nope.shape[0]
    pad = jnp.zeros((k, A_HEADS, HEAD_PAD - QK_NOPE - QK_ROPE), w_nope.dtype)
    return jnp.concatenate([w_nope, w_rope, pad], axis=-1).reshape(k, A_HEADS * HEAD_PAD)


def _rope_tables(pos):
    half = QK_ROPE // 2
    inv = ROPE_BASE ** (-jnp.arange(half, dtype=F32) / half)
    ang = pos.astype(F32)[:, None] * inv[None, :]
    cos, sin = jnp.cos(ang), jnp.sin(ang)
    n = pos.shape[0]
    ctab = jnp.concatenate([jnp.ones((n, QK_NOPE), F32), cos, cos,
                            jnp.zeros((n, HEAD_PAD - QK_NOPE - QK_ROPE), F32)], axis=1)
    stab = jnp.concatenate([jnp.zeros((n, QK_NOPE), F32), sin, sin,
                            jnp.zeros((n, HEAD_PAD - QK_NOPE - QK_ROPE), F32)], axis=1)
    return ctab, stab


def _pick_tile(n, target):
    t = min(n, target)
    assert n % t == 0, (n, t)
    return t


def kernel(x_prompt, x_sample, cache_ckv, cache_kpe, state_conv, state_ssm, page_table, c_prompt, c_sample, w_ada, b_ada, g_pre_mix, g_post_mix, g_pre_ffn, g_post_ffn, w_in, g_q_a, w_q_b, g_kv_a, w_uk, w_uv, w_o_attn, conv_w, conv_b, dt_bias, a_log, d_skip, g_ssd, w_o_ssd, w_out, w_router, b_router, w_gate_e, w_up_e, w_down_e, w_gate_s, w_up_s, w_down_s):
    bp, sp, d = x_prompt.shape
    nb, ds, _ = x_sample.shape
    depth = w_in.shape[0]
    assert depth == 1 and ds == 1 and d == D_MODEL
    n_pages = page_table.shape[1]
    past_len = n_pages * PAGE_SIZE
    lyr = 0

    offs = [0]
    for n in IN_SPLITS:
        offs.append(offs[-1] + n)
    win = w_in[lyr]
    w_qa, w_kva, w_kpe, w_z, w_xbc, w_dt, w_ga, w_gb = (win[:, offs[i]:offs[i + 1]] for i in range(8))
    wqb = w_q_b[lyr].reshape(Q_LORA, A_HEADS, QK_NOPE + QK_ROPE)
    wq_pad = _head_pad(wqb[..., :QK_NOPE], wqb[..., QK_NOPE:])
    wq_rot = _head_pad(jnp.zeros_like(wqb[..., :QK_NOPE]), _rot_half(wqb[..., QK_NOPE:]))
    wuk_pad = _head_pad(w_uk[lyr], jnp.zeros((KV_LORA, A_HEADS, QK_ROPE), F32))
    premix_w = [
        w_qa.astype(BF16), w_kva.astype(BF16),
        _pad_cols(w_kpe, QK_NOPE, LANES).astype(BF16),
        _pad_cols(_rot_half(w_kpe), QK_NOPE, LANES).astype(BF16),
        w_z.astype(BF16), w_xbc.astype(BF16),
        _pad_cols(w_dt, 0, LANES).astype(BF16), w_dt.T.astype(BF16),
        w_ga.astype(BF16), w_gb.astype(BF16),
        g_q_a[lyr][None, :], wq_pad.astype(BF16), wq_rot.astype(BF16),
        g_kv_a[lyr][None, :], wuk_pad.astype(BF16),
        w_uv[lyr].reshape(KV_LORA, A_HEADS * V_DIM).astype(BF16),
    ]
    merge_w = [w_o_attn[lyr].astype(BF16), w_o_ssd[lyr].astype(BF16), w_out[lyr].astype(BF16),
               g_post_mix[lyr][None, :], g_pre_ffn[lyr][None, :]]
    shared_w = [w_gate_s[lyr].astype(BF16), w_up_s[lyr].astype(BF16), w_down_s[lyr].astype(BF16)]
    w_router_t = w_router[lyr].T.astype(BF16)
    b_router_col = b_router[lyr][:, None]
    g_pre = g_pre_mix[lyr][None, :]
    g_post_ffn_row = g_post_ffn[lyr][None, :]
    cw = conv_w[lyr]
    cb = conv_b[lyr][None, :]
    neg_a = -jnp.exp(a_log[lyr].astype(F32))
    dtb_row = jnp.pad(dt_bias[lyr], (0, LANES - S_HEADS))[None, :]
    na_row = jnp.pad(neg_a, (0, LANES - S_HEADS))[None, :]
    dtb_col = dt_bias[lyr][:, None]
    na_col = neg_a[:, None]
    dskip_row = jnp.repeat(d_skip[lyr].astype(F32), S_HEAD_DIM)[None, :]
    gssd_row = g_ssd[lyr][None, :]

    n_mod_rows = bp + nb
    pad_rows = (-n_mod_rows) % 16
    c_all = jnp.concatenate([c_prompt, c_sample, jnp.zeros((pad_rows, d), F32)], axis=0)
    mod = _adaln(c_all, w_ada[lyr], b_ada[lyr][None, :])
    mods = [mod[:, i * d:(i + 1) * d] for i in range(6)]
    mods_p = [m[:bp][:, None, :] for m in mods]
    mods_s = [m[bp:bp + nb][None, :, :] for m in mods]

    ctab_p, stab_p = _rope_tables(jnp.arange(sp))
    tm = _pick_tile(sp, 256)
    (q_p, k_p, v_p, ckv_p, kpe_pad_p, z_p, xbc_p, dt_p, dtt_p, ga_p, gb_p) = _premix(
        x_prompt, mods_p[0], mods_p[1], g_pre, ctab_p, stab_p, premix_w, tm)
    o_attn_p = _flash_attention(q_p, k_p, v_p, _pick_tile(sp, 512), 4)
    y_ssd_p, ssm_t_p = _ssd_prompt(xbc_p, z_p, dt_p, dtt_p, cw, cb, dtb_row, dtb_col, na_row, na_col,
                                   dskip_row, gssd_row)
    x1_p, h2_p = _merge(x_prompt, o_attn_p, y_ssd_p, ga_p, gb_p, mods_p[2], mods_p[3], mods_p[4],
                        merge_w, _pick_tile(sp, 512))
    tb = _pick_tile(sp, MOE_BLOCK)
    info_p, slotk_p, wk_p, pc_p = _router(h2_p, w_router_t, b_router_col, tb)
    y_prompt = _sparse_moe(h2_p, info_p, slotk_p, wk_p, pc_p, x1_p, mods_p[5],
                           w_gate_e[lyr], w_up_e[lyr], w_down_e[lyr], shared_w, g_post_ffn_row, tb)

    xs = x_sample.reshape(1, nb, d)
    ctab_s, stab_s = _rope_tables(past_len + jnp.arange(ds))
    (q_s, _, _, ckv_s, kpe_pad_s, z_s, xbc_s, dt_s, _, ga_s, gb_s) = _premix(
        xs, mods_s[0], mods_s[1], g_pre, ctab_s, stab_s, premix_w, nb)
    kpe_s = kpe_pad_s[0, :, QK_NOPE:QK_NOPE + QK_ROPE]
    wuk_heads = jnp.transpose(w_uk[lyr], (1, 0, 2)).astype(BF16)
    wuv_heads = jnp.transpose(w_uv[lyr], (1, 0, 2)).astype(BF16)
    qlat = jnp.transpose(_qlat(q_s[0], wuk_heads), (1, 0, 2))
    qpe = q_s[0].reshape(nb, A_HEADS, HEAD_PAD)[:, :, QK_NOPE:QK_NOPE + QK_ROPE]
    o_lat = _decode_attention(page_table, qlat, qpe,
                              ckv_s.reshape(nb, 1, KV_LORA), kpe_s.reshape(nb, 1, QK_ROPE),
                              cache_ckv.reshape(cache_ckv.shape[1:]),
                              jnp.swapaxes(cache_kpe.reshape(cache_kpe.shape[1:]), 1, 2))
    o_attn_s = _value_up(jnp.transpose(o_lat, (1, 0, 2)), wuv_heads)[None]
    y_ssd_s, ssm_s = _ssd_sample(xbc_s.reshape(nb, 1, CONV_CH), state_conv[lyr],
                                 z_s.reshape(nb, 1, D_INNER), dt_s.reshape(nb, 1, LANES),
                                 state_ssm.reshape(state_ssm.shape[1:]), cw, cb, dtb_row, na_row,
                                 dskip_row, gssd_row)
    x1_s, h2_s = _merge(xs, o_attn_s, y_ssd_s.reshape(1, nb, D_INNER), ga_s, gb_s,
                        mods_s[2], mods_s[3], mods_s[4], merge_w, nb)
    comb_s = _router(h2_s, w_router_t, b_router_col, nb)[0]
    y_s = _moe(h2_s, comb_s, x1_s, mods_s[5], w_gate_e[lyr], w_up_e[lyr], w_down_e[lyr],
               shared_w, g_post_ffn_row, nb)

    kpe_p = kpe_pad_p[:, :, QK_NOPE:QK_NOPE + QK_ROPE]
    conv_p = xbc_p[:, sp - (CONV_W - 1):, :]
    ssm_p = jnp.transpose(ssm_t_p.reshape(bp, D_STATE, S_HEADS, S_HEAD_DIM), (0, 2, 3, 1))
    conv_s = jnp.concatenate([state_conv[lyr][:, 1:, :], xbc_s.reshape(nb, 1, CONV_CH)], axis=1)
    return (y_prompt, y_s.reshape(nb, ds, d),
            ckv_p[None], kpe_p[None], conv_p[None], ssm_p[None].astype(x_prompt.dtype),
            ckv_s.reshape(1, nb, ds, KV_LORA), kpe_s.reshape(1, nb, ds, QK_ROPE),
            conv_s[None], ssm_s[None].astype(x_sample.dtype))
```
